```python
import math
import jax, jax.numpy as jnp
from jax import lax
import numpy as np

D_MODEL = 2048
BATCH = 8
SEQ = 2048
DEPTH = 1

CHUNK = 64
Q_BLOCK = 128
SB_HEADS = 8
SB_HEAD_DIM = 128
SB_WIDTH = SB_HEADS * SB_HEAD_DIM
DA_HEADS = 8
DA_HEAD_DIM = 64
DA_V_DIM = 2 * DA_HEAD_DIM
DA_QK_WIDTH = DA_HEADS * 2 * DA_HEAD_DIM
DA_WIDTH = DA_HEADS * DA_V_DIM
N_BRANCHES = 2
IN_COLS = 3 * SB_WIDTH + 2 * DA_QK_WIDTH + DA_WIDTH + N_BRANCHES * D_MODEL
ROPE_THETA = 10000.0
N_GROUPS = 4
EXPERTS_PER_GROUP = 8
N_EXPERTS = N_GROUPS * EXPERTS_PER_GROUP
TOP_K_IN_GROUP = 2
D_EXPERT = D_MODEL // 4
MOE_BLOCK = 128
EPS = 1e-6

kernel_name = "hybrid_stickbreak_diffattn_hiermoe"


def rmsnorm(x, g):
    xf = x.astype(jnp.float32)
    y = xf * lax.rsqrt(jnp.mean(xf * xf, axis=-1, keepdims=True) + EPS)
    return (y * g.astype(jnp.float32)).astype(x.dtype)


def to_heads(t, n_heads, head_dim):
    b, s, _ = t.shape
    return t.reshape(b, s, n_heads, head_dim).transpose(0, 2, 1, 3)


def rope(x, pos):
    d = x.shape[-1]
    half = d // 2
    inv_freq = ROPE_THETA ** (-(jnp.arange(half, dtype=jnp.float32) * 2.0 / d))
    ang = pos.astype(jnp.float32)[:, None] * inv_freq[None, :]
    cos, sin = jnp.cos(ang), jnp.sin(ang)
    xf = x.astype(jnp.float32)
    x1, x2 = xf[..., :half], xf[..., half:]
    return jnp.concatenate([x1 * cos - x2 * sin, x2 * cos + x1 * sin], axis=-1).astype(x.dtype)


def stick_breaking_block(q, k, v, t0):
    qb, L, d = q.shape[2], k.shape[2], q.shape[3]
    z = jnp.einsum('bhqd,bhkd->bhqk', q, k).astype(jnp.float32) / math.sqrt(d)
    t = t0 + jnp.arange(qb)
    s = jnp.arange(L)
    mask = s[None, :] < t[:, None]
    log_1m = jnp.where(mask, jax.nn.log_sigmoid(-z), 0.0)
    suffix = lax.cumsum(log_1m, axis=3, reverse=True) - log_1m
    a = jnp.where(mask, jnp.exp(jax.nn.log_sigmoid(z) + suffix), 0.0)
    return jnp.einsum('bhqk,bhkd->bhqd', a.astype(v.dtype), v)


def diff_attn_block(q1, q2, k1, k2, v, t0, lam):
    qb, L, d = q1.shape[2], k1.shape[2], q1.shape[3]
    t = t0 + jnp.arange(qb)
    s = jnp.arange(L)
    mask = (s // CHUNK)[None, :] <= (t // CHUNK)[:, None]

    def probs(q, k):
        sc = jnp.einsum('bhqd,bhkd->bhqk', q, k).astype(jnp.float32) / math.sqrt(d)
        return jax.nn.softmax(jnp.where(mask, sc, -jnp.inf), axis=-1)

    attn = probs(q1, k1) - lam * probs(q2, k2)
    return jnp.einsum('bhqk,bhkd->bhqd', attn.astype(v.dtype), v)


def hierarchical_moe(h, w_group_router, w_expert_router, w_gate, w_up, w_down):
    b, s, d = h.shape
    n = b * s
    hf = h.reshape(n, d)
    g_logits = (hf @ w_group_router).astype(jnp.float32)
    g_prob = jax.nn.softmax(g_logits, axis=-1)
    grp = jnp.argmax(g_logits, axis=-1)
    p_grp = jnp.take_along_axis(g_prob, grp[:, None], axis=1)
    e_logits = (hf @ w_expert_router).astype(jnp.float32).reshape(n, N_GROUPS, EXPERTS_PER_GROUP)
    idx = jnp.broadcast_to(grp[:, None, None], (n, 1, EXPERTS_PER_GROUP))
    e_sel = jnp.take_along_axis(e_logits, idx, axis=1)[:, 0]
    top_val, top_idx = lax.top_k(e_sel, TOP_K_IN_GROUP)
    gate = p_grp * jax.nn.softmax(top_val, axis=-1)
    expert = grp[:, None] * EXPERTS_PER_GROUP + top_idx

    a_total = n * TOP_K_IN_GROUP
    e_flat = expert.reshape(a_total)
    tok_flat = jnp.repeat(jnp.arange(n, dtype=jnp.int32), TOP_K_IN_GROUP)
    w_flat = gate.reshape(a_total)
    order = jnp.argsort(e_flat)
    e_sorted, tok_sorted, w_sorted = e_flat[order], tok_flat[order], w_flat[order]
    counts = jnp.zeros((N_EXPERTS,), jnp.int32).at[e_flat].add(1)
    padded = (counts + MOE_BLOCK - 1) // MOE_BLOCK * MOE_BLOCK
    starts = jnp.cumsum(counts) - counts
    pad_ends = jnp.cumsum(padded)
    pad_starts = pad_ends - padded
    dest = pad_starts[e_sorted] + jnp.arange(a_total, dtype=jnp.int32) - starts[e_sorted]
    n_blocks = -(-a_total // MOE_BLOCK) + N_EXPERTS
    p_rows = n_blocks * MOE_BLOCK
    buf_tok = jnp.full((p_rows,), n, jnp.int32).at[dest].set(tok_sorted)
    buf_w = jnp.zeros((p_rows,), jnp.float32).at[dest].set(w_sorted)
    block_expert = jnp.minimum(
        jnp.searchsorted(pad_ends, jnp.arange(n_blocks, dtype=jnp.int32) * MOE_BLOCK, side='right'),
        N_EXPERTS - 1)
    h_pad = jnp.concatenate([hf, jnp.zeros((1, d), hf.dtype)], axis=0)
    xb = h_pad[buf_tok].reshape(n_blocks, MOE_BLOCK, d)

    def expert_block(args):
        xblk, e = args
        return (jax.nn.silu(xblk @ w_gate[e]) * (xblk @ w_up[e])) @ w_down[e]

    yb = lax.map(expert_block, (xb, block_expert)).reshape(p_rows, d)
    y = jnp.zeros((n + 1, d), h.dtype).at[buf_tok].add(yb * buf_w[:, None].astype(h.dtype))
    return y[:n].reshape(b, s, d)


def setup_inputs(seed: int = 0) -> dict:
    key = jax.random.key(seed)
    ks = jax.random.split(key, 20)
    f32 = jnp.float32

    def nrm(k, shape, scale):
        return jax.random.normal(k, shape, f32) * scale

    def gain(k, shape):
        return 1.0 + 0.02 * jax.random.normal(k, shape, f32)

    return {
        "x": nrm(ks[0], (BATCH, SEQ, D_MODEL), 1.0),
        "mix_norm_g": gain(ks[1], (DEPTH, D_MODEL)),
        "w_in": nrm(ks[2], (DEPTH, D_MODEL, IN_COLS), D_MODEL ** -0.5),
        "lambda_q1": nrm(ks[3], (DEPTH, DA_HEAD_DIM), 0.1),
        "lambda_k1": nrm(ks[4], (DEPTH, DA_HEAD_DIM), 0.1),
        "lambda_q2": nrm(ks[5], (DEPTH, DA_HEAD_DIM), 0.1),
        "lambda_k2": nrm(ks[6], (DEPTH, DA_HEAD_DIM), 0.1),
        "diff_subnorm_g": gain(ks[7], (DEPTH, DA_V_DIM)),
        "w_branch_sb": nrm(ks[8], (DEPTH, SB_WIDTH, D_MODEL), SB_WIDTH ** -0.5),
        "w_branch_diff": nrm(ks[9], (DEPTH, DA_WIDTH, D_MODEL), DA_WIDTH ** -0.5),
        "w_out": nrm(ks[10], (DEPTH, D_MODEL, D_MODEL), D_MODEL ** -0.5),
        "ffn_norm_g": gain(ks[11], (DEPTH, D_MODEL)),
        "w_group_router": nrm(ks[12], (DEPTH, D_MODEL, N_GROUPS), D_MODEL ** -0.5),
        "w_expert_router": nrm(ks[13], (DEPTH, D_MODEL, N_EXPERTS), D_MODEL ** -0.5),
        "w_gate": nrm(ks[14], (DEPTH, N_EXPERTS, D_MODEL, D_EXPERT), D_MODEL ** -0.5),
        "w_up": nrm(ks[15], (DEPTH, N_EXPERTS, D_MODEL, D_EXPERT), D_MODEL ** -0.5),
        "w_down": nrm(ks[16], (DEPTH, N_EXPERTS, D_EXPERT, D_MODEL), D_EXPERT ** -0.5),
        "final_norm_g": gain(ks[17], (D_MODEL,)),
    }


def reference(x, mix_norm_g, w_in, lambda_q1, lambda_k1, lambda_q2, lambda_k2, diff_subnorm_g,
              w_branch_sb, w_branch_diff, w_out, ffn_norm_g, w_group_router, w_expert_router,
              w_gate, w_up, w_down, final_norm_g):
    b, s, _ = x.shape
    pos = jnp.arange(s, dtype=jnp.int32)
    split_at = list(np.cumsum([SB_WIDTH, SB_WIDTH, SB_WIDTH, DA_QK_WIDTH, DA_QK_WIDTH, DA_WIDTH, D_MODEL]))
    for l in range(DEPTH):
        h = rmsnorm(x, mix_norm_g[l])
        proj = h @ w_in[l]
        q_sb, k_sb, v_sb, q_da, k_da, v_da, g_sb, g_da = jnp.split(proj, split_at, axis=-1)
        q_sb = to_heads(q_sb, SB_HEADS, SB_HEAD_DIM)
        k_sb = to_heads(k_sb, SB_HEADS, SB_HEAD_DIM)
        v_sb = to_heads(v_sb, SB_HEADS, SB_HEAD_DIM)
        q_da = to_heads(q_da, DA_HEADS, 2 * DA_HEAD_DIM)
        k_da = to_heads(k_da, DA_HEADS, 2 * DA_HEAD_DIM)
        v_da = to_heads(v_da, DA_HEADS, DA_V_DIM)
        q1 = rope(q_da[..., :DA_HEAD_DIM], pos)
        q2 = rope(q_da[..., DA_HEAD_DIM:], pos)
        k1 = rope(k_da[..., :DA_HEAD_DIM], pos)
        k2 = rope(k_da[..., DA_HEAD_DIM:], pos)
        lam_init = 0.8 - 0.6 * math.exp(-0.3 * l)
        lam = (jnp.exp(jnp.sum(lambda_q1[l].astype(jnp.float32) * lambda_k1[l].astype(jnp.float32)))
               - jnp.exp(jnp.sum(lambda_q2[l].astype(jnp.float32) * lambda_k2[l].astype(jnp.float32)))
               + lam_init)

        sb_blocks, da_blocks = [], []
        for i in range(s // Q_BLOCK):
            t0, t1 = i * Q_BLOCK, (i + 1) * Q_BLOCK
            sb_blocks.append(stick_breaking_block(q_sb[:, :, t0:t1], k_sb[:, :, :t1], v_sb[:, :, :t1], t0))
            da_blocks.append(diff_attn_block(q1[:, :, t0:t1], q2[:, :, t0:t1], k1[:, :, :t1], k2[:, :, :t1],
                                             v_da[:, :, :t1], t0, lam))
        o_sb = jnp.concatenate(sb_blocks, axis=2).transpose(0, 2, 1, 3).reshape(b, s, SB_WIDTH)
        o_da = rmsnorm(jnp.concatenate(da_blocks, axis=2), diff_subnorm_g[l]) * (1.0 - lam_init)
        o_da = o_da.transpose(0, 2, 1, 3).reshape(b, s, DA_WIDTH)

        merged = (jax.nn.sigmoid(g_sb) * (o_sb @ w_branch_sb[l])
                  + jax.nn.sigmoid(g_da) * (o_da @ w_branch_diff[l]))
        x = x + merged @ w_out[l]

        h2 = rmsnorm(x, ffn_norm_g[l])
        x = x + hierarchical_moe(h2, w_group_router[l], w_expert_router[l], w_gate[l], w_up[l], w_down[l])
    return rmsnorm(x, final_norm_g)
```

```python
import functools
import math

import jax
import jax.numpy as jnp
from jax import lax
from jax.experimental import pallas as pl
from jax.experimental.pallas import tpu as pltpu

F32 = jnp.float32
BF16 = jnp.bfloat16

CHUNK = 64
SB_HEADS = 8
SB_HEAD_DIM = 128
DA_HEADS = 8
DA_HEAD_DIM = 64
DA_V_DIM = 2 * DA_HEAD_DIM
N_GROUPS = 4
EXPERTS_PER_GROUP = 8
N_EXPERTS = N_GROUPS * EXPERTS_PER_GROUP
ROPE_THETA = 10000.0
EPS = 1e-6

LANES = 128
SUBLANES = 8
ROUTER_ROWS = 40
VMEM_LIMIT_BYTES = 56 * 1024 * 1024

SB_LOG_WEIGHT_FLOOR = -104.0

TM_IN = 1024
TN_IN = 1024
TQ_SB = 256
TK_SB = 128
TQ_DA = 256
TM_MERGE = 512
TN_MERGE = 256
T_ROUTE = 512
TM_DISPATCH = 1024
TM_EXPERT = 256
TM_COMBINE = 256


def _params(sem, vmem=VMEM_LIMIT_BYTES):
    return pltpu.CompilerParams(dimension_semantics=sem, vmem_limit_bytes=vmem)


def _nt_dot(a, b, **kw):
    return lax.dot_general(a, b, (((1,), (1,)), ((), ())), preferred_element_type=F32, **kw)


def _in_proj_kernel(x_ref, g_ref, w_ref, cos_ref, sin_ref, o_ref, h_ref, *, rope_lo, rope_hi):
    j = pl.program_id(1)

    @pl.when(j == 0)
    def _():
        x = x_ref[...]
        ms = jnp.mean(x * x, axis=-1, keepdims=True)
        h_ref[...] = (x * lax.rsqrt(ms + EPS) * g_ref[...]).astype(BF16)

    y = jnp.dot(h_ref[...], w_ref[...], preferred_element_type=F32)
    is_rope = jnp.logical_and(j >= rope_lo, j < rope_hi)

    @pl.when(is_rope)
    def _():
        cos = cos_ref[...]
        sin = sin_ref[...]
        lane = lax.broadcasted_iota(jnp.int32, cos.shape, 1)
        first_half = (lane % DA_HEAD_DIM) < (DA_HEAD_DIM // 2)
        for hh in range(y.shape[1] // LANES):
            yh = y[:, hh * LANES:(hh + 1) * LANES]
            partner = jnp.where(first_half,
                                pltpu.roll(yh, LANES - DA_HEAD_DIM // 2, 1),
                                pltpu.roll(yh, DA_HEAD_DIM // 2, 1))
            o_ref[:, hh * LANES:(hh + 1) * LANES] = (yh * cos + partner * sin).astype(BF16)

    @pl.when(jnp.logical_not(is_rope))
    def _():
        o_ref[...] = y.astype(BF16)


def _in_proj(x2d, g, w_bf16, cos_tab, sin_tab, seq, rope_lo, rope_hi):
    n, d = x2d.shape
    cols = w_bf16.shape[1]
    tm, tn = min(TM_IN, seq), TN_IN
    pos_tiles = seq // tm
    return pl.pallas_call(
        functools.partial(_in_proj_kernel, rope_lo=rope_lo, rope_hi=rope_hi),
        out_shape=jax.ShapeDtypeStruct((n, cols), BF16),
        grid=(n // tm, cols // tn),
        in_specs=[
            pl.BlockSpec((tm, d), lambda i, j: (i, 0)),
            pl.BlockSpec((1, d), lambda i, j: (0, 0)),
            pl.BlockSpec((d, tn), lambda i, j: (0, j)),
            pl.BlockSpec((tm, LANES), lambda i, j: (i % pos_tiles, 0)),
            pl.BlockSpec((tm, LANES), lambda i, j: (i % pos_tiles, 0)),
        ],
        out_specs=pl.BlockSpec((tm, tn), lambda i, j: (i, j)),
        scratch_shapes=[pltpu.VMEM((tm, d), BF16)],
        compiler_params=_params(("parallel", "arbitrary")),
        name="in_proj",
    )(x2d, g, w_bf16, cos_tab, sin_tab)


def _sb_kernel(q_ref, k_ref, v_ref, o_ref, *, tq, tk, scale):
    i = pl.program_id(2)
    q = q_ref[...]
    nd = tq // tk
    row = lax.broadcasted_iota(jnp.int32, (tk, 2 * tk), 0)
    col = lax.broadcasted_iota(jnp.int32, (tk, 2 * tk), 1)
    u2 = jnp.where(jnp.logical_or(col >= tk, row > col), 1.0, 0.0).astype(BF16)

    def block(j, running, acc, diag_off):
        start = pl.multiple_of(j * tk, tk)
        k = k_ref[pl.ds(start, tk), :]
        v = v_ref[pl.ds(start, tk), :]
        z = _nt_dot(q, k) * scale
        lsn = jnp.minimum(-z, 0.0) - jnp.log1p(jnp.exp(-jnp.abs(z)))
        if diag_off is not None:
            r = lax.broadcasted_iota(jnp.int32, (tq, tk), 0)
            c = lax.broadcasted_iota(jnp.int32, (tq, tk), 1)
            mask = (c + diag_off * tk) < r
            lsn = jnp.where(mask, lsn, 0.0)
        hi = lsn.astype(BF16)
        lo = (lsn - hi.astype(F32)).astype(BF16)
        cs = (jnp.dot(hi, u2, preferred_element_type=F32)
              + jnp.dot(lo, u2, preferred_element_type=F32))
        a = jnp.exp(z + lsn + cs[:, :tk] + running)
        if diag_off is not None:
            a = jnp.where(mask, a, 0.0)
        acc = acc + jnp.dot(a.astype(BF16), v, preferred_element_type=F32)
        running = running + cs[:, tk:]
        return running, acc

    running = jnp.zeros((tq, tk), F32)
    acc = jnp.zeros((tq, q.shape[1]), F32)
    for off in range(nd - 1, -1, -1):
        running, acc = block(i * nd + off, running, acc, off)

    def cond(c):
        j, alive, _, _ = c
        return jnp.logical_and(j >= 0, alive)

    def body(c):
        j, _, running, acc = c
        running, acc = block(j, running, acc, None)
        alive = jnp.max(running) > SB_LOG_WEIGHT_FLOOR
        return j - 1, alive, running, acc

    _, _, _, acc = lax.while_loop(cond, body, (i * nd - 1, jnp.bool_(True), running, acc))
    o_ref[...] = acc.astype(o_ref.dtype)


def _sb_attention(proj3, batch, seq, col_q, col_k, col_v):
    d = SB_HEAD_DIM
    tq, tk = min(TQ_SB, seq), min(TK_SB, seq)
    return pl.pallas_call(
        functools.partial(_sb_kernel, tq=tq, tk=tk, scale=1.0 / math.sqrt(d)),
        out_shape=jax.ShapeDtypeStruct((batch, seq, SB_HEADS * d), BF16),
        grid=(batch, SB_HEADS, seq // tq),
        in_specs=[
            pl.BlockSpec((None, tq, d), lambda b, h, i: (b, i, col_q + h)),
            pl.BlockSpec((None, seq, d), lambda b, h, i: (b, 0, col_k + h)),
            pl.BlockSpec((None, seq, d), lambda b, h, i: (b, 0, col_v + h)),
        ],
        out_specs=pl.BlockSpec((None, tq, d), lambda b, h, i: (b, i, h)),
        compiler_params=_params(("parallel", "parallel", "arbitrary")),
        name="sb_attn",
    )(proj3, proj3, proj3)


def _da_kernel(lam_ref, q_ref, k_ref, v_ref, g_ref, o_ref, *, tq, lam_init):
    i = pl.program_id(2)
    lv = lam_ref[...]
    lam = (jnp.exp(jnp.sum(lv[0:1] * lv[1:2], axis=-1, keepdims=True))
           - jnp.exp(jnp.sum(lv[2:3] * lv[3:4], axis=-1, keepdims=True)) + lam_init)

    q = q_ref[...]
    lane = lax.broadcasted_iota(jnp.int32, q.shape, 1)
    qs = q * (1.0 / math.sqrt(DA_HEAD_DIM))
    zero = jnp.zeros_like(qs)
    q1 = jnp.where(lane < DA_HEAD_DIM, qs, zero)
    q2 = jnp.where(lane >= DA_HEAD_DIM, qs, zero)
    ones = jnp.ones((tq, DA_V_DIM), BF16)

    def one_map(qm, k, vext, mask, m, acc):
        s = _nt_dot(qm, k)
        if mask is not None:
            s = jnp.where(mask, s, -jnp.inf)
        m_new = jnp.maximum(m, jnp.max(s, axis=-1, keepdims=True))
        alpha = jnp.exp(m - m_new)
        p = jnp.exp(s - m_new)
        acc = alpha * acc + jnp.dot(p.astype(BF16), vext, preferred_element_type=F32)
        return m_new, acc

    def step(j, carry, mask):
        m1, a1, m2, a2 = carry
        start = pl.multiple_of(j * tq, tq)
        k = k_ref[pl.ds(start, tq), :]
        v = v_ref[pl.ds(start, tq), :]
        vext = jnp.concatenate([v, ones], axis=1)
        m1, a1 = one_map(q1, k, vext, mask, m1, a1)
        m2, a2 = one_map(q2, k, vext, mask, m2, a2)
        return m1, a1, m2, a2

    r = lax.broadcasted_iota(jnp.int32, (tq, tq), 0)
    c = lax.broadcasted_iota(jnp.int32, (tq, tq), 1)
    diag_mask = (c // CHUNK) <= (r // CHUNK)
    neg = jnp.full((tq, 1), -jnp.inf, F32)
    zacc = jnp.zeros((tq, 2 * DA_V_DIM), F32)
    carry = step(i, (neg, zacc, neg, zacc), diag_mask)
    carry = lax.fori_loop(0, i, lambda j, cr: step(j, cr, None), carry)
    _, a1, _, a2 = carry
    o = a1[:, :DA_V_DIM] / a1[:, DA_V_DIM:] - lam * (a2[:, :DA_V_DIM] / a2[:, DA_V_DIM:])
    ms = jnp.mean(o * o, axis=-1, keepdims=True)
    o_ref[...] = (o * lax.rsqrt(ms + EPS) * g_ref[...] * (1.0 - lam_init)).astype(o_ref.dtype)


def _da_attention(lam_vecs, proj3, sub_g, batch, seq, col_q, col_k, col_v, lam_init):
    d = DA_V_DIM
    tq = min(TQ_DA, seq)
    return pl.pallas_call(
        functools.partial(_da_kernel, tq=tq, lam_init=lam_init),
        out_shape=jax.ShapeDtypeStruct((batch, seq, DA_HEADS * d), BF16),
        grid=(batch, DA_HEADS, seq // tq),
        in_specs=[
            pl.BlockSpec(lam_vecs.shape, lambda b, h, i: (0, 0)),
            pl.BlockSpec((None, tq, d), lambda b, h, i: (b, i, col_q + h)),
            pl.BlockSpec((None, seq, d), lambda b, h, i: (b, 0, col_k + h)),
            pl.BlockSpec((None, seq, d), lambda b, h, i: (b, 0, col_v + h)),
            pl.BlockSpec((1, d), lambda b, h, i: (0, 0)),
        ],
        out_specs=pl.BlockSpec((None, tq, d), lambda b, h, i: (b, i, h)),
        compiler_params=_params(("parallel", "parallel", "arbitrary")),
        name="da_attn",
    )(lam_vecs, proj3, proj3, proj3, sub_g)


def _merge_kernel(x_ref, gmix_ref, osb_ref, oda_ref, wgs_ref, wgd_ref, wbs_ref, wbd_ref, wo_ref,
                  gffn_ref, wr_ref, x1_ref, h2_ref, lt_ref, h_scr, acc_scr):
    j = pl.program_id(1)

    @pl.when(j == 0)
    def _():
        x = x_ref[...]
        ms = jnp.mean(x * x, axis=-1, keepdims=True)
        h_scr[...] = (x * lax.rsqrt(ms + EPS) * gmix_ref[...]).astype(BF16)
        acc_scr[...] = jnp.zeros_like(acc_scr)

    h = h_scr[...]
    gate_sb = jax.nn.sigmoid(jnp.dot(h, wgs_ref[...], preferred_element_type=F32))
    gate_da = jax.nn.sigmoid(jnp.dot(h, wgd_ref[...], preferred_element_type=F32))
    br_sb = jnp.dot(osb_ref[...], wbs_ref[...], preferred_element_type=F32)
    br_da = jnp.dot(oda_ref[...], wbd_ref[...], preferred_element_type=F32)
    merged = gate_sb * br_sb + gate_da * br_da
    acc_scr[...] += jnp.dot(merged.astype(BF16), wo_ref[...], preferred_element_type=F32)

    @pl.when(j == pl.num_programs(1) - 1)
    def _():
        x1 = x_ref[...] + acc_scr[...]
        x1_ref[...] = x1
        ms = jnp.mean(x1 * x1, axis=-1, keepdims=True)
        h2 = x1 * lax.rsqrt(ms + EPS) * gffn_ref[...]
        h2_ref[...] = h2
        lt_ref[...] = _nt_dot(wr_ref[...], h2, precision=lax.Precision.HIGHEST)


def _merge(x2d, gmix, o_sb, o_da, wgs, wgd, wbs, wbd, wo, gffn, wr_t):
    n, d = x2d.shape
    tm, tn = min(TM_MERGE, n), TN_MERGE
    nj = d // tn
    wsb, wda = o_sb.shape[1], o_da.shape[1]
    return pl.pallas_call(
        _merge_kernel,
        out_shape=(jax.ShapeDtypeStruct((n, d), F32),
                   jax.ShapeDtypeStruct((n, d), F32),
                   jax.ShapeDtypeStruct((ROUTER_ROWS, n), F32)),
        grid=(n // tm, nj),
        in_specs=[
            pl.BlockSpec((tm, d), lambda i, j: (i, 0)),
            pl.BlockSpec((1, d), lambda i, j: (0, 0)),
            pl.BlockSpec((tm, wsb), lambda i, j: (i, 0)),
            pl.BlockSpec((tm, wda), lambda i, j: (i, 0)),
            pl.BlockSpec((d, tn), lambda i, j: (0, j)),
            pl.BlockSpec((d, tn), lambda i, j: (0, j)),
            pl.BlockSpec((wsb, tn), lambda i, j: (0, j)),
            pl.BlockSpec((wda, tn), lambda i, j: (0, j)),
            pl.BlockSpec((tn, d), lambda i, j: (j, 0)),
            pl.BlockSpec((1, d), lambda i, j: (0, 0)),
            pl.BlockSpec((ROUTER_ROWS, d), lambda i, j: (0, 0)),
        ],
        out_specs=(pl.BlockSpec((tm, d), lambda i, j: (i, 0)),
                   pl.BlockSpec((tm, d), lambda i, j: (i, 0)),
                   pl.BlockSpec((ROUTER_ROWS, tm), lambda i, j: (0, i))),
        scratch_shapes=[pltpu.VMEM((tm, d), BF16), pltpu.VMEM((tm, d), F32)],
        compiler_params=_params(("parallel", "arbitrary")),
        name="merge",
    )(x2d, gmix, o_sb, o_da, wgs, wgd, wbs, wbd, wo, gffn, wr_t)


def _route_kernel(lt_ref, ids_ref, rank_ref, gate_ref, cnt_ref, carry_ref):
    step = pl.program_id(0)

    @pl.when(step == 0)
    def _():
        carry_ref[...] = jnp.zeros_like(carry_ref)

    t = lt_ref.shape[1]
    el = lt_ref[0:N_EXPERTS, :]
    gl = lt_ref[N_EXPERTS:N_EXPERTS + 8, :]
    grow = lax.broadcasted_iota(jnp.int32, gl.shape, 0)
    gl = jnp.where(grow < N_GROUPS, gl, -jnp.inf)
    gmax = jnp.max(gl, axis=0, keepdims=True)
    grp = jnp.min(jnp.where(gl == gmax, grow, N_GROUPS), axis=0, keepdims=True)
    p_grp = 1.0 / jnp.sum(jnp.exp(gl - gmax), axis=0, keepdims=True)

    eidx = lax.broadcasted_iota(jnp.int32, el.shape, 0)
    in_grp = (eidx // EXPERTS_PER_GROUP) == grp
    e1 = jnp.where(in_grp, el, -jnp.inf)
    top1 = jnp.max(e1, axis=0, keepdims=True)
    idx1 = jnp.min(jnp.where(e1 == top1, eidx, N_EXPERTS), axis=0, keepdims=True)
    e2 = jnp.where(eidx == idx1, -jnp.inf, e1)
    top2 = jnp.max(e2, axis=0, keepdims=True)
    idx2 = jnp.min(jnp.where(e2 == top2, eidx, N_EXPERTS), axis=0, keepdims=True)
    dlt = jnp.exp(top2 - top1)
    den = 1.0 + dlt
    gate_ref[0:1, :] = p_grp / den
    gate_ref[1:2, :] = p_grp * dlt / den
    ids_ref[0:1, :] = idx1
    ids_ref[1:2, :] = idx2

    oh1 = jnp.where(eidx == idx1, 1.0, 0.0)
    oh2 = jnp.where(eidx == idx2, 1.0, 0.0)
    rr = lax.broadcasted_iota(jnp.int32, (t, t), 0)
    cc = lax.broadcasted_iota(jnp.int32, (t, t), 1)
    before = jnp.where(rr < cc, 1.0, 0.0).astype(BF16)
    pre1 = jnp.dot(oh1.astype(BF16), before, preferred_element_type=F32)
    pre2 = jnp.dot(oh2.astype(BF16), before, preferred_element_type=F32)
    cnt1 = jnp.sum(oh1, axis=1, keepdims=True)
    cnt2 = jnp.sum(oh2, axis=1, keepdims=True)
    carry = carry_ref[:, 0:1]
    rank1 = jnp.sum(oh1 * (carry + pre1), axis=0, keepdims=True)
    rank2 = jnp.sum(oh2 * (carry + cnt1 + pre2), axis=0, keepdims=True)
    rank_ref[0:1, :] = rank1.astype(jnp.int32)
    rank_ref[1:2, :] = rank2.astype(jnp.int32)
    new_carry = carry_ref[...] + cnt1 + cnt2
    carry_ref[...] = new_carry
    cnt_ref[...] = new_carry


def _route(logits_t):
    n = logits_t.shape[1]
    t = min(T_ROUTE, n)
    return pl.pallas_call(
        _route_kernel,
        out_shape=(jax.ShapeDtypeStruct((2, n), jnp.int32),
                   jax.ShapeDtypeStruct((2, n), jnp.int32),
                   jax.ShapeDtypeStruct((2, n), F32),
                   jax.ShapeDtypeStruct((N_EXPERTS, LANES), F32)),
        grid=(n // t,),
        in_specs=[pl.BlockSpec((ROUTER_ROWS, t), lambda i: (0, i))],
        out_specs=(pl.BlockSpec((2, t), lambda i: (0, i)),
                   pl.BlockSpec((2, t), lambda i: (0, i)),
                   pl.BlockSpec((2, t), lambda i: (0, i)),
                   pl.BlockSpec((N_EXPERTS, LANES), lambda i: (0, 0))),
        scratch_shapes=[pltpu.VMEM((N_EXPERTS, LANES), F32)],
        compiler_params=_params(("arbitrary",)),
        name="route",
    )(logits_t)


def _dispatch_kernel(zs_ref, nu_ref, dest_ref, h2_hbm, xb_hbm, zeros_ref, sem, tail_sem, *, tm, te, n_tiles):
    step = pl.program_id(0)

    def tail_copy(t):
        return pltpu.make_async_copy(zeros_ref, xb_hbm.at[pl.ds(pl.multiple_of(t * te, te), te)], tail_sem)

    @pl.when(step == 0)
    def _():
        zeros_ref[...] = jnp.zeros_like(zeros_ref)

        def zero_copy(e):
            start = pl.multiple_of(zs_ref[e], SUBLANES)
            return pltpu.make_async_copy(zeros_ref, xb_hbm.at[pl.ds(start, te)], sem)

        for e in range(N_EXPERTS):
            zero_copy(e).start()
        for e in range(N_EXPERTS):
            zero_copy(e).wait()
        lax.fori_loop(nu_ref[0], n_tiles, lambda t, c: (tail_copy(t).start(), c)[1], 0)

    base = step * tm

    def issue(r, _):
        src = h2_hbm.at[pl.ds(base + r, 1)]
        pltpu.make_async_copy(src, xb_hbm.at[pl.ds(dest_ref[0, 0, r], 1)], sem).start()
        pltpu.make_async_copy(src, xb_hbm.at[pl.ds(dest_ref[0, 0, tm + r], 1)], sem).start()
        return 0

    lax.fori_loop(0, tm, issue, 0)
    pltpu.make_async_copy(h2_hbm.at[pl.ds(0, 2 * tm)], xb_hbm.at[pl.ds(0, 2 * tm)], sem).wait()

    @pl.when(step == pl.num_programs(0) - 1)
    def _():
        lax.fori_loop(nu_ref[0], n_tiles, lambda t, c: (tail_copy(t).wait(), c)[1], 0)


def _dispatch(zero_starts, n_used, dest_tiles, h2, n_tiles, tm, te):
    n, d = h2.shape
    return pl.pallas_call(
        functools.partial(_dispatch_kernel, tm=tm, te=te, n_tiles=n_tiles),
        out_shape=jax.ShapeDtypeStruct((n_tiles * te, d), h2.dtype),
        grid_spec=pltpu.PrefetchScalarGridSpec(
            num_scalar_prefetch=2,
            grid=(n // tm,),
            in_specs=[
                pl.BlockSpec((1, 1, 2 * tm), lambda i, zs, nu: (i, 0, 0), memory_space=pltpu.SMEM),
                pl.BlockSpec(memory_space=pl.ANY),
            ],
            out_specs=pl.BlockSpec(memory_space=pl.ANY),
            scratch_shapes=[pltpu.VMEM((te, d), h2.dtype), pltpu.SemaphoreType.DMA,
                            pltpu.SemaphoreType.DMA],
        ),
        compiler_params=_params(("arbitrary",)),
        name="dispatch",
    )(zero_starts, n_used, dest_tiles, h2)


def _expert_kernel(te_ref, nu_ref, xb_ref, wg_ref, wu_ref, wd_ref, yb_ref):
    used = pl.program_id(0) < nu_ref[0]

    @pl.when(used)
    def _():
        x = xb_ref[...].astype(BF16)
        g = jnp.dot(x, wg_ref[...], preferred_element_type=F32)
        u = jnp.dot(x, wu_ref[...], preferred_element_type=F32)
        hid = (g * jax.nn.sigmoid(g)) * u
        yb_ref[...] = jnp.dot(hid.astype(BF16), wd_ref[...], preferred_element_type=F32)

    @pl.when(jnp.logical_not(used))
    def _():
        yb_ref[...] = jnp.zeros_like(yb_ref)


def _experts(tile_expert, n_used, xb, wg, wu, wd, n_tiles, te):
    d = xb.shape[1]
    de = wg.shape[2]

    def row_map(t, te_ref, nu_ref):
        return (jnp.minimum(t, nu_ref[0] - 1), 0)

    def w_map(t, te_ref, nu_ref):
        return (te_ref[jnp.minimum(t, nu_ref[0] - 1)], 0, 0)

    return pl.pallas_call(
        _expert_kernel,
        out_shape=jax.ShapeDtypeStruct((n_tiles * te, d), F32),
        grid_spec=pltpu.PrefetchScalarGridSpec(
            num_scalar_prefetch=2,
            grid=(n_tiles,),
            in_specs=[
                pl.BlockSpec((te, d), row_map),
                pl.BlockSpec((None, d, de), w_map),
                pl.BlockSpec((None, d, de), w_map),
                pl.BlockSpec((None, de, d), w_map),
            ],
            out_specs=pl.BlockSpec((te, d), lambda t, te_ref, nu_ref: (t, 0)),
        ),
        compiler_params=_params(("arbitrary",)),
        name="experts",
    )(tile_expert, n_used, xb, wg, wu, wd)


def _combine_kernel(dest_ref, x1_ref, gate_ref, g_ref, yb_hbm, o_ref, buf, sem, *, tm, final_norm):
    def issue(r, _):
        pltpu.make_async_copy(yb_hbm.at[pl.ds(dest_ref[0, 0, r], 1)], buf.at[pl.ds(r, 1)], sem).start()
        pltpu.make_async_copy(yb_hbm.at[pl.ds(dest_ref[0, 0, tm + r], 1)], buf.at[pl.ds(tm + r, 1)],
                              sem).start()
        return 0

    lax.fori_loop(0, tm, issue, 0)
    pltpu.make_async_copy(yb_hbm.at[pl.ds(0, 2 * tm)], buf, sem).wait()
    gates = gate_ref[...]
    x = x1_ref[...] + (buf[0:tm, :] * gates[:, 0:1] + buf[tm:2 * tm, :] * gates[:, 1:2])
    if final_norm:
        ms = jnp.mean(x * x, axis=-1, keepdims=True)
        x = x * lax.rsqrt(ms + EPS) * g_ref[...]
    o_ref[...] = x


def _combine(dest_tiles, x1, gates, g_final, yb, tm, final_norm):
    n, d = x1.shape
    return pl.pallas_call(
        functools.partial(_combine_kernel, tm=tm, final_norm=final_norm),
        out_shape=jax.ShapeDtypeStruct((n, d), F32),
        grid=(n // tm,),
        in_specs=[
            pl.BlockSpec((1, 1, 2 * tm), lambda i: (i, 0, 0), memory_space=pltpu.SMEM),
            pl.BlockSpec((tm, d), lambda i: (i, 0)),
            pl.BlockSpec((tm, 2), lambda i: (i, 0)),
            pl.BlockSpec((1, d), lambda i: (0, 0)),
            pl.BlockSpec(memory_space=pl.ANY),
        ],
        out_specs=pl.BlockSpec((tm, d), lambda i: (i, 0)),
        scratch_shapes=[pltpu.VMEM((2 * tm, d), F32), pltpu.SemaphoreType.DMA],
        compiler_params=_params(("arbitrary",)),
        name="combine",
    )(dest_tiles, x1, gates, g_final, yb)


def _rope_tables(seq):
    half = DA_HEAD_DIM // 2
    inv_freq = ROPE_THETA ** (-(jnp.arange(half, dtype=F32) * 2.0 / DA_HEAD_DIM))
    ang = jnp.arange(seq, dtype=F32)[:, None] * inv_freq[None, :]
    cos, sin = jnp.cos(ang), jnp.sin(ang)
    cos_tab = jnp.tile(cos, (1, LANES // half))
    sin_tab = jnp.tile(jnp.concatenate([-sin, sin], axis=1), (1, LANES // DA_HEAD_DIM))
    return cos_tab, sin_tab


def _tile_indices(dest, tm):
    n = dest.shape[1]
    return dest.reshape(2, n // tm, tm).transpose(1, 0, 2).reshape(n // tm, 1, 2 * tm)


def _layer(x2d, batch, seq, layer, mix_g, w_in, lam_vecs, sub_g, w_bsb, w_bda, w_out, ffn_g,
           w_grp, w_exp, w_gate, w_up, w_down):
    n, d = x2d.shape
    sbw = SB_HEADS * SB_HEAD_DIM
    daw = DA_HEADS * DA_V_DIM
    qkv_cols = 3 * sbw + 3 * daw
    lam_init = 0.8 - 0.6 * math.exp(-0.3 * layer)

    w_qkv = w_in[:, :qkv_cols].astype(BF16)
    w_gsb = w_in[:, qkv_cols:qkv_cols + d].astype(BF16)
    w_gda = w_in[:, qkv_cols + d:].astype(BF16)
    cos_tab, sin_tab = _rope_tables(seq)
    rope_lo = 3 * sbw // TN_IN
    rope_hi = (3 * sbw + 2 * daw) // TN_IN
    proj = _in_proj(x2d, mix_g[None, :], w_qkv, cos_tab, sin_tab, seq, rope_lo, rope_hi)
    proj3 = proj.reshape(batch, seq, qkv_cols)
    nsb = sbw // SB_HEAD_DIM
    nda = daw // DA_V_DIM
    o_sb = _sb_attention(proj3, batch, seq, 0, nsb, 2 * nsb)
    o_da = _da_attention(lam_vecs, proj3, sub_g[None, :], batch, seq,
                         3 * nsb, 3 * nsb + nda, 3 * nsb + 2 * nda, lam_init)

    wr_t = jnp.zeros((ROUTER_ROWS, d), F32)
    wr_t = wr_t.at[:N_EXPERTS].set(w_exp.T).at[N_EXPERTS:N_EXPERTS + N_GROUPS].set(w_grp.T)
    x1, h2, logits_t = _merge(x2d, mix_g[None, :], o_sb.reshape(n, sbw), o_da.reshape(n, daw),
                              w_gsb, w_gda, w_bsb.astype(BF16), w_bda.astype(BF16),
                              w_out.astype(BF16), ffn_g[None, :], wr_t)

    ids, ranks, gates, counts = _route(logits_t)
    te = TM_EXPERT
    counts = counts[:, 0].astype(jnp.int32)
    padded = (counts + te - 1) // te * te
    pad_ends = jnp.cumsum(padded)
    pad_starts = pad_ends - padded
    dest = pad_starts[ids] + ranks
    n_tiles = (2 * n) // te + N_EXPERTS
    n_used = (pad_ends[-1] // te).astype(jnp.int32)[None]
    tile_expert = jnp.minimum(
        jnp.searchsorted(pad_ends, jnp.arange(n_tiles, dtype=jnp.int32) * te, side="right"),
        N_EXPERTS - 1).astype(jnp.int32)
    zero_starts = ((pad_starts + counts) // SUBLANES * SUBLANES).astype(jnp.int32)

    tmd = min(TM_DISPATCH, n)
    xb = _dispatch(zero_starts, n_used, _tile_indices(dest, tmd), h2, n_tiles + 1, tmd, te)
    yb = _experts(tile_expert, n_used, xb, w_gate.astype(BF16), w_up.astype(BF16),
                  w_down.astype(BF16), n_tiles, te)
    return x1, gates, dest, yb


def kernel(x, mix_norm_g, w_in, lambda_q1, lambda_k1, lambda_q2, lambda_k2, diff_subnorm_g, w_branch_sb, w_branch_diff, w_out, ffn_norm_g, w_group_router, w_expert_router, w_gate, w_up, w_down, final_norm_g):
    batch, seq, d = x.shape
    depth = w_in.shape[0]
    n = batch * seq
    x2d = x.reshape(n, d)
    tmc = min(TM_COMBINE, n)
    for l in range(depth):
        lam_vecs = jnp.stack([lambda_q1[l], lambda_k1[l], lambda_q2[l], lambda_k2[l]]).astype(F32)
        x1, gates, dest, yb = _layer(
            x2d, batch, seq, l, mix_norm_g[l], w_in[l], lam_vecs, diff_subnorm_g[l], w_branch_sb[l],
            w_branch_diff[l], w_out[l], ffn_norm_g[l], w_group_router[l], w_expert_router[l],
            w_gate[l], w_up[l], w_down[l])
        x2d = _combine(_tile_indices(dest, tmc), x1, gates.T, final_norm_g[None, :], yb, tmc,
                       final_norm=(l == depth - 1))
    return x2d.reshape(batch, seq, d)
```

```python
import functools
import math

import jax
import jax.numpy as jnp
from jax import lax
from jax.experimental import pallas as pl
from jax.experimental.pallas import tpu as pltpu

F32 = jnp.float32
BF16 = jnp.bfloat16

CHUNK = 64
SB_HEADS = 8
SB_HEAD_DIM = 128
DA_HEADS = 8
DA_HEAD_DIM = 64
DA_V_DIM = 2 * DA_HEAD_DIM
N_GROUPS = 4
EXPERTS_PER_GROUP = 8
N_EXPERTS = N_GROUPS * EXPERTS_PER_GROUP
ROPE_THETA = 10000.0
EPS = 1e-6

LANES = 128
SUBLANES = 8
ROUTER_ROWS = 40
VMEM_LIMIT_BYTES = 56 * 1024 * 1024

SB_LOG_WEIGHT_FLOOR = -104.0

TM_IN = 1024
TN_IN = 1024
TQ_SB = 256
TK_SB = 128
TQ_DA = 256
TM_MERGE = 512
TN_MERGE = 256
T_ROUTE = 512
TM_DISPATCH = 1024
TM_EXPERT = 256
TM_COMBINE = 256


def _params(sem, vmem=VMEM_LIMIT_BYTES):
    return pltpu.CompilerParams(dimension_semantics=sem, vmem_limit_bytes=vmem)


def _nt_dot(a, b, **kw):
    return lax.dot_general(a, b, (((1,), (1,)), ((), ())), preferred_element_type=F32, **kw)


def _in_proj_kernel(x_ref, g_ref, w_ref, cos_ref, sin_ref, o_ref, h_ref, *, rope_lo, rope_hi):
    j = pl.program_id(1)

    @pl.when(j == 0)
    def _():
        x = x_ref[...]
        ms = jnp.mean(x * x, axis=-1, keepdims=True)
        h_ref[...] = (x * lax.rsqrt(ms + EPS) * g_ref[...]).astype(BF16)

    y = jnp.dot(h_ref[...], w_ref[...], preferred_element_type=F32)
    is_rope = jnp.logical_and(j >= rope_lo, j < rope_hi)

    @pl.when(is_rope)
    def _():
        cos = cos_ref[...]
        sin = sin_ref[...]
        lane = lax.broadcasted_iota(jnp.int32, cos.shape, 1)
        first_half = (lane % DA_HEAD_DIM) < (DA_HEAD_DIM // 2)
        for hh in range(y.shape[1] // LANES):
            yh = y[:, hh * LANES:(hh + 1) * LANES]
            partner = jnp.where(first_half,
                                pltpu.roll(yh, LANES - DA_HEAD_DIM // 2, 1),
                                pltpu.roll(yh, DA_HEAD_DIM // 2, 1))
            o_ref[:, hh * LANES:(hh + 1) * LANES] = (yh * cos + partner * sin).astype(BF16)

    @pl.when(jnp.logical_not(is_rope))
    def _():
        o_ref[...] = y.astype(BF16)


def _in_proj(x2d, g, w_bf16, cos_tab, sin_tab, seq, rope_lo, rope_hi):
    n, d = x2d.shape
    cols = w_bf16.shape[1]
    tm, tn = min(TM_IN, seq), TN_IN
    pos_tiles = seq // tm
    return pl.pallas_call(
        functools.partial(_in_proj_kernel, rope_lo=rope_lo, rope_hi=rope_hi),
        out_shape=jax.ShapeDtypeStruct((n, cols), BF16),
        grid=(n // tm, cols // tn),
        in_specs=[
            pl.BlockSpec((tm, d), lambda i, j: (i, 0)),
            pl.BlockSpec((1, d), lambda i, j: (0, 0)),
            pl.BlockSpec((d, tn), lambda i, j: (0, j)),
            pl.BlockSpec((tm, LANES), lambda i, j: (i % pos_tiles, 0)),
            pl.BlockSpec((tm, LANES), lambda i, j: (i % pos_tiles, 0)),
        ],
        out_specs=pl.BlockSpec((tm, tn), lambda i, j: (i, j)),
        scratch_shapes=[pltpu.VMEM((tm, d), BF16)],
        compiler_params=_params(("parallel", "arbitrary")),
        name="in_proj",
    )(x2d, g, w_bf16, cos_tab, sin_tab)


def _sb_kernel(q_ref, k_ref, v_ref, o_ref, *, tq, tk, scale):
    i = pl.program_id(2)
    q = q_ref[...]
    row = lax.broadcasted_iota(jnp.int32, (tk, 2 * tk), 0)
    col = lax.broadcasted_iota(jnp.int32, (tk, 2 * tk), 1)
    u2 = jnp.where(jnp.logical_or(col >= tk, row > col), 1.0, 0.0).astype(BF16)

    def span(start, width, diag_cols, running):
        k = k_ref[pl.ds(start, width), :]
        v = v_ref[pl.ds(start, width), :]
        z = _nt_dot(q, k) * scale
        lsn = jnp.minimum(-z, 0.0) - jnp.log1p(jnp.exp(-jnp.abs(z)))
        if diag_cols is not None:
            r = lax.broadcasted_iota(jnp.int32, z.shape, 0)
            c = lax.broadcasted_iota(jnp.int32, z.shape, 1)
            mask = (c - diag_cols) < r
            lsn = jnp.where(mask, lsn, 0.0)
        hi = lsn.astype(BF16)
        lo = (lsn - hi.astype(F32)).astype(BF16)
        nb = width // tk
        suffix = [None] * nb
        for blk in range(nb - 1, -1, -1):
            sl = slice(blk * tk, (blk + 1) * tk)
            cs = (jnp.dot(hi[:, sl], u2, preferred_element_type=F32)
                  + jnp.dot(lo[:, sl], u2, preferred_element_type=F32))
            suffix[blk] = cs[:, :tk] + running
            running = running + cs[:, tk:]
        a = jnp.exp(z + lsn + jnp.concatenate(suffix, axis=1))
        if diag_cols is not None:
            a = jnp.where(mask, a, 0.0)
        return jnp.dot(a.astype(BF16), v, preferred_element_type=F32), running

    zero_run = jnp.zeros((tq, tk), F32)

    @pl.when(i == 0)
    def _():
        out, _ = span(0, tq, 0, zero_run)
        o_ref[...] = out.astype(o_ref.dtype)

    @pl.when(i > 0)
    def _():
        near_start = pl.multiple_of((i - 1) * tq, tq)
        acc, running = span(near_start, 2 * tq, tq, zero_run)

        def alive_after(running):
            return jnp.max(running) > SB_LOG_WEIGHT_FLOOR

        def cond(c):
            j, alive, _, _ = c
            return jnp.logical_and(j >= 0, alive)

        def body(c):
            j, _, running, acc = c
            out, running = span(pl.multiple_of(j * tq, tq), tq, None, running)
            return j - 1, alive_after(running), running, acc + out

        _, _, _, acc = lax.while_loop(cond, body, (i - 2, alive_after(running), running, acc))
        o_ref[...] = acc.astype(o_ref.dtype)


def _sb_attention(proj3, batch, seq, col_q, col_k, col_v):
    d = SB_HEAD_DIM
    tq, tk = min(TQ_SB, seq), min(TK_SB, seq)
    return pl.pallas_call(
        functools.partial(_sb_kernel, tq=tq, tk=tk, scale=1.0 / math.sqrt(d)),
        out_shape=jax.ShapeDtypeStruct((batch, seq, SB_HEADS * d), BF16),
        grid=(batch, SB_HEADS, seq // tq),
        in_specs=[
            pl.BlockSpec((None, tq, d), lambda b, h, i: (b, i, col_q + h)),
            pl.BlockSpec((None, seq, d), lambda b, h, i: (b, 0, col_k + h)),
            pl.BlockSpec((None, seq, d), lambda b, h, i: (b, 0, col_v + h)),
        ],
        out_specs=pl.BlockSpec((None, tq, d), lambda b, h, i: (b, i, h)),
        compiler_params=_params(("parallel", "parallel", "arbitrary")),
        name="sb_attn",
    )(proj3, proj3, proj3)


def _da_kernel(lam_ref, q_ref, k_ref, v_ref, g_ref, o_ref, vext_ref, *, tq, n_q, lam_init):
    i = pl.program_id(2)

    @pl.when(i == 0)
    def _():
        vext_ref[:, :DA_V_DIM] = v_ref[...]
        vext_ref[:, DA_V_DIM:] = jnp.ones((vext_ref.shape[0], DA_V_DIM), BF16)

    lv = lam_ref[...]
    lam = (jnp.exp(jnp.sum(lv[0:1] * lv[1:2], axis=-1, keepdims=True))
           - jnp.exp(jnp.sum(lv[2:3] * lv[3:4], axis=-1, keepdims=True)) + lam_init)

    q = q_ref[...]
    lane = lax.broadcasted_iota(jnp.int32, q.shape, 1)
    qs = q * (1.0 / math.sqrt(DA_HEAD_DIM))
    zero = jnp.zeros_like(qs)
    q_maps = (jnp.where(lane < DA_HEAD_DIM, qs, zero), jnp.where(lane >= DA_HEAD_DIM, qs, zero))
    r = lax.broadcasted_iota(jnp.int32, (tq, tq), 0)
    c = lax.broadcasted_iota(jnp.int32, (tq, tq), 1)
    diag_mask = (c // CHUNK) <= (r // CHUNK)

    def softmax_av(qm, lo):
        sd = jnp.where(diag_mask, _nt_dot(qm, k_ref[lo:lo + tq, :]), -jnp.inf)
        m = jnp.max(sd, axis=-1, keepdims=True)
        if lo > 0:
            sl = _nt_dot(qm, k_ref[0:lo, :])
            m = jnp.maximum(m, jnp.max(sl, axis=-1, keepdims=True))
        acc = jnp.dot(jnp.exp(sd - m).astype(BF16), vext_ref[lo:lo + tq, :], preferred_element_type=F32)
        if lo > 0:
            acc = acc + jnp.dot(jnp.exp(sl - m).astype(BF16), vext_ref[0:lo, :],
                                preferred_element_type=F32)
        return acc[:, :DA_V_DIM] / acc[:, DA_V_DIM:]

    for tile in range(n_q):
        @pl.when(i == tile)
        def _(tile=tile):
            o = softmax_av(q_maps[0], tile * tq) - lam * softmax_av(q_maps[1], tile * tq)
            ms = jnp.mean(o * o, axis=-1, keepdims=True)
            o_ref[...] = (o * lax.rsqrt(ms + EPS) * g_ref[...] * (1.0 - lam_init)).astype(o_ref.dtype)


def _da_attention(lam_vecs, proj3, sub_g, batch, seq, col_q, col_k, col_v, lam_init):
    d = DA_V_DIM
    tq = min(TQ_DA, seq)
    return pl.pallas_call(
        functools.partial(_da_kernel, tq=tq, n_q=seq // tq, lam_init=lam_init),
        out_shape=jax.ShapeDtypeStruct((batch, seq, DA_HEADS * d), BF16),
        grid=(batch, DA_HEADS, seq // tq),
        in_specs=[
            pl.BlockSpec(lam_vecs.shape, lambda b, h, i: (0, 0)),
            pl.BlockSpec((None, tq, d), lambda b, h, i: (b, i, col_q + h)),
            pl.BlockSpec((None, seq, d), lambda b, h, i: (b, 0, col_k + h)),
            pl.BlockSpec((None, seq, d), lambda b, h, i: (b, 0, col_v + h)),
            pl.BlockSpec((1, d), lambda b, h, i: (0, 0)),
        ],
        out_specs=pl.BlockSpec((None, tq, d), lambda b, h, i: (b, i, h)),
        scratch_shapes=[pltpu.VMEM((seq, 2 * d), BF16)],
        compiler_params=_params(("parallel", "parallel", "arbitrary")),
        name="da_attn",
    )(lam_vecs, proj3, proj3, proj3, sub_g)


def _merge_kernel(x_ref, gmix_ref, osb_ref, oda_ref, wgs_ref, wgd_ref, wbs_ref, wbd_ref, wo_ref,
                  gffn_ref, wr_ref, x1_ref, h2_ref, lt_ref, h_scr, acc_scr):
    j = pl.program_id(1)

    @pl.when(j == 0)
    def _():
        x = x_ref[...]
        ms = jnp.mean(x * x, axis=-1, keepdims=True)
        h_scr[...] = (x * lax.rsqrt(ms + EPS) * gmix_ref[...]).astype(BF16)
        acc_scr[...] = jnp.zeros_like(acc_scr)

    h = h_scr[...]
    gate_sb = jax.nn.sigmoid(jnp.dot(h, wgs_ref[...], preferred_element_type=F32))
    gate_da = jax.nn.sigmoid(jnp.dot(h, wgd_ref[...], preferred_element_type=F32))
    br_sb = jnp.dot(osb_ref[...], wbs_ref[...], preferred_element_type=F32)
    br_da = jnp.dot(oda_ref[...], wbd_ref[...], preferred_element_type=F32)
    merged = gate_sb * br_sb + gate_da * br_da
    acc_scr[...] += jnp.dot(merged.astype(BF16), wo_ref[...], preferred_element_type=F32)

    @pl.when(j == pl.num_programs(1) - 1)
    def _():
        x1 = x_ref[...] + acc_scr[...]
        x1_ref[...] = x1
        ms = jnp.mean(x1 * x1, axis=-1, keepdims=True)
        h2 = x1 * lax.rsqrt(ms + EPS) * gffn_ref[...]
        h2_ref[...] = h2
        lt_ref[...] = _nt_dot(wr_ref[...], h2, precision=lax.Precision.HIGHEST)


def _merge(x2d, gmix, o_sb, o_da, wgs, wgd, wbs, wbd, wo, gffn, wr_t):
    n, d = x2d.shape
    tm, tn = min(TM_MERGE, n), TN_MERGE
    nj = d // tn
    wsb, wda = o_sb.shape[1], o_da.shape[1]
    return pl.pallas_call(
        _merge_kernel,
        out_shape=(jax.ShapeDtypeStruct((n, d), F32),
                   jax.ShapeDtypeStruct((n, d), F32),
                   jax.ShapeDtypeStruct((ROUTER_ROWS, n), F32)),
        grid=(n // tm, nj),
        in_specs=[
            pl.BlockSpec((tm, d), lambda i, j: (i, 0)),
            pl.BlockSpec((1, d), lambda i, j: (0, 0)),
            pl.BlockSpec((tm, wsb), lambda i, j: (i, 0)),
            pl.BlockSpec((tm, wda), lambda i, j: (i, 0)),
            pl.BlockSpec((d, tn), lambda i, j: (0, j)),
            pl.BlockSpec((d, tn), lambda i, j: (0, j)),
            pl.BlockSpec((wsb, tn), lambda i, j: (0, j)),
            pl.BlockSpec((wda, tn), lambda i, j: (0, j)),
            pl.BlockSpec((tn, d), lambda i, j: (j, 0)),
            pl.BlockSpec((1, d), lambda i, j: (0, 0)),
            pl.BlockSpec((ROUTER_ROWS, d), lambda i, j: (0, 0)),
        ],
        out_specs=(pl.BlockSpec((tm, d), lambda i, j: (i, 0)),
                   pl.BlockSpec((tm, d), lambda i, j: (i, 0)),
                   pl.BlockSpec((ROUTER_ROWS, tm), lambda i, j: (0, i))),
        scratch_shapes=[pltpu.VMEM((tm, d), BF16), pltpu.VMEM((tm, d), F32)],
        compiler_params=_params(("parallel", "arbitrary")),
        name="merge",
    )(x2d, gmix, o_sb, o_da, wgs, wgd, wbs, wbd, wo, gffn, wr_t)


def _route_kernel(lt_ref, ids_ref, rank_ref, gate_ref, cnt_ref, carry_ref):
    step = pl.program_id(0)

    @pl.when(step == 0)
    def _():
        carry_ref[...] = jnp.zeros_like(carry_ref)

    t = lt_ref.shape[1]
    el = lt_ref[0:N_EXPERTS, :]
    gl = lt_ref[N_EXPERTS:N_EXPERTS + 8, :]
    grow = lax.broadcasted_iota(jnp.int32, gl.shape, 0)
    gl = jnp.where(grow < N_GROUPS, gl, -jnp.inf)
    gmax = jnp.max(gl, axis=0, keepdims=True)
    grp = jnp.min(jnp.where(gl == gmax, grow, N_GROUPS), axis=0, keepdims=True)
    p_grp = 1.0 / jnp.sum(jnp.exp(gl - gmax), axis=0, keepdims=True)

    eidx = lax.broadcasted_iota(jnp.int32, el.shape, 0)
    in_grp = (eidx // EXPERTS_PER_GROUP) == grp
    e1 = jnp.where(in_grp, el, -jnp.inf)
    top1 = jnp.max(e1, axis=0, keepdims=True)
    idx1 = jnp.min(jnp.where(e1 == top1, eidx, N_EXPERTS), axis=0, keepdims=True)
    e2 = jnp.where(eidx == idx1, -jnp.inf, e1)
    top2 = jnp.max(e2, axis=0, keepdims=True)
    idx2 = jnp.min(jnp.where(e2 == top2, eidx, N_EXPERTS), axis=0, keepdims=True)
    dlt = jnp.exp(top2 - top1)
    den = 1.0 + dlt
    gate_ref[0:1, :] = p_grp / den
    gate_ref[1:2, :] = p_grp * dlt / den
    ids_ref[0:1, :] = idx1
    ids_ref[1:2, :] = idx2

    oh1 = jnp.where(eidx == idx1, 1.0, 0.0)
    oh2 = jnp.where(eidx == idx2, 1.0, 0.0)
    rr = lax.broadcasted_iota(jnp.int32, (t, t), 0)
    cc = lax.broadcasted_iota(jnp.int32, (t, t), 1)
    before = jnp.where(rr < cc, 1.0, 0.0).astype(BF16)
    pre1 = jnp.dot(oh1.astype(BF16), before, preferred_element_type=F32)
    pre2 = jnp.dot(oh2.astype(BF16), before, preferred_element_type=F32)
    cnt1 = jnp.sum(oh1, axis=1, keepdims=True)
    cnt2 = jnp.sum(oh2, axis=1, keepdims=True)
    carry = carry_ref[:, 0:1]
    rank1 = jnp.sum(oh1 * (carry + pre1), axis=0, keepdims=True)
    rank2 = jnp.sum(oh2 * (carry + cnt1 + pre2), axis=0, keepdims=True)
    rank_ref[0:1, :] = rank1.astype(jnp.int32)
    rank_ref[1:2, :] = rank2.astype(jnp.int32)
    new_carry = carry_ref[...] + cnt1 + cnt2
    carry_ref[...] = new_carry
    cnt_ref[...] = new_carry


def _route(logits_t):
    n = logits_t.shape[1]
    t = min(T_ROUTE, n)
    return pl.pallas_call(
        _route_kernel,
        out_shape=(jax.ShapeDtypeStruct((2, n), jnp.int32),
                   jax.ShapeDtypeStruct((2, n), jnp.int32),
                   jax.ShapeDtypeStruct((2, n), F32),
                   jax.ShapeDtypeStruct((N_EXPERTS, LANES), F32)),
        grid=(n // t,),
        in_specs=[pl.BlockSpec((ROUTER_ROWS, t), lambda i: (0, i))],
        out_specs=(pl.BlockSpec((2, t), lambda i: (0, i)),
                   pl.BlockSpec((2, t), lambda i: (0, i)),
                   pl.BlockSpec((2, t), lambda i: (0, i)),
                   pl.BlockSpec((N_EXPERTS, LANES), lambda i: (0, 0))),
        scratch_shapes=[pltpu.VMEM((N_EXPERTS, LANES), F32)],
        compiler_params=_params(("arbitrary",)),
        name="route",
    )(logits_t)


def _dispatch_kernel(zs_ref, nu_ref, dest_ref, h2_ref, xb_hbm, zeros_ref, sem, tail_sem, *, tm, te, n_tiles):
    step = pl.program_id(0)

    def tail_copy(t):
        return pltpu.make_async_copy(zeros_ref, xb_hbm.at[pl.ds(pl.multiple_of(t * te, te), te)], tail_sem)

    @pl.when(step == 0)
    def _():
        zeros_ref[...] = jnp.zeros_like(zeros_ref)

        def zero_copy(e):
            start = pl.multiple_of(zs_ref[e], SUBLANES)
            return pltpu.make_async_copy(zeros_ref, xb_hbm.at[pl.ds(start, te)], sem)

        for e in range(N_EXPERTS):
            zero_copy(e).start()
        for e in range(N_EXPERTS):
            zero_copy(e).wait()
        lax.fori_loop(nu_ref[0], n_tiles, lambda t, c: (tail_copy(t).start(), c)[1], 0)

    def issue(r, _):
        src = h2_ref.at[pl.ds(r, 1)]
        pltpu.make_async_copy(src, xb_hbm.at[pl.ds(dest_ref[0, 0, r], 1)], sem).start()
        pltpu.make_async_copy(src, xb_hbm.at[pl.ds(dest_ref[0, 0, tm + r], 1)], sem).start()
        return 0

    lax.fori_loop(0, tm, issue, 0, unroll=8)
    for _ in range(2):
        pltpu.make_async_copy(h2_ref, xb_hbm.at[pl.ds(0, tm)], sem).wait()

    @pl.when(step == pl.num_programs(0) - 1)
    def _():
        lax.fori_loop(nu_ref[0], n_tiles, lambda t, c: (tail_copy(t).wait(), c)[1], 0)


def _dispatch(zero_starts, n_used, dest_tiles, h2, n_tiles, tm, te):
    n, d = h2.shape
    return pl.pallas_call(
        functools.partial(_dispatch_kernel, tm=tm, te=te, n_tiles=n_tiles),
        out_shape=jax.ShapeDtypeStruct((n_tiles * te, d), h2.dtype),
        grid_spec=pltpu.PrefetchScalarGridSpec(
            num_scalar_prefetch=2,
            grid=(n // tm,),
            in_specs=[
                pl.BlockSpec((1, 1, 2 * tm), lambda i, zs, nu: (i, 0, 0), memory_space=pltpu.SMEM),
                pl.BlockSpec((tm, d), lambda i, zs, nu: (i, 0)),
            ],
            out_specs=pl.BlockSpec(memory_space=pl.ANY),
            scratch_shapes=[pltpu.VMEM((te, d), h2.dtype), pltpu.SemaphoreType.DMA,
                            pltpu.SemaphoreType.DMA],
        ),
        compiler_params=_params(("arbitrary",)),
        name="dispatch",
    )(zero_starts, n_used, dest_tiles, h2)


def _expert_kernel(te_ref, nu_ref, xb_ref, wg_ref, wu_ref, wd_ref, yb_ref):
    used = pl.program_id(0) < nu_ref[0]

    @pl.when(used)
    def _():
        x = xb_ref[...].astype(BF16)
        g = jnp.dot(x, wg_ref[...], preferred_element_type=F32)
        u = jnp.dot(x, wu_ref[...], preferred_element_type=F32)
        hid = (g * jax.nn.sigmoid(g)) * u
        yb_ref[...] = jnp.dot(hid.astype(BF16), wd_ref[...], preferred_element_type=F32)

    @pl.when(jnp.logical_not(used))
    def _():
        yb_ref[...] = jnp.zeros_like(yb_ref)


def _experts(tile_expert, n_used, xb, wg, wu, wd, n_tiles, te):
    d = xb.shape[1]
    de = wg.shape[2]

    def row_map(t, te_ref, nu_ref):
        return (jnp.minimum(t, nu_ref[0] - 1), 0)

    def w_map(t, te_ref, nu_ref):
        return (te_ref[jnp.minimum(t, nu_ref[0] - 1)], 0, 0)

    return pl.pallas_call(
        _expert_kernel,
        out_shape=jax.ShapeDtypeStruct((n_tiles * te, d), F32),
        grid_spec=pltpu.PrefetchScalarGridSpec(
            num_scalar_prefetch=2,
            grid=(n_tiles,),
            in_specs=[
                pl.BlockSpec((te, d), row_map),
                pl.BlockSpec((None, d, de), w_map),
                pl.BlockSpec((None, d, de), w_map),
                pl.BlockSpec((None, de, d), w_map),
            ],
            out_specs=pl.BlockSpec((te, d), lambda t, te_ref, nu_ref: (t, 0)),
        ),
        compiler_params=_params(("arbitrary",)),
        name="experts",
    )(tile_expert, n_used, xb, wg, wu, wd)


def _combine_kernel(dest_ref, x1_ref, gate_ref, g_ref, yb_hbm, o_ref, buf, sem, *, tm, final_norm):
    def issue(r, _):
        pltpu.make_async_copy(yb_hbm.at[pl.ds(dest_ref[0, 0, r], 1)], buf.at[pl.ds(r, 1)], sem).start()
        pltpu.make_async_copy(yb_hbm.at[pl.ds(dest_ref[0, 0, tm + r], 1)], buf.at[pl.ds(tm + r, 1)],
                              sem).start()
        return 0

    lax.fori_loop(0, tm, issue, 0)
    pltpu.make_async_copy(yb_hbm.at[pl.ds(0, 2 * tm)], buf, sem).wait()
    gates = gate_ref[...]
    x = x1_ref[...] + (buf[0:tm, :] * gates[:, 0:1] + buf[tm:2 * tm, :] * gates[:, 1:2])
    if final_norm:
        ms = jnp.mean(x * x, axis=-1, keepdims=True)
        x = x * lax.rsqrt(ms + EPS) * g_ref[...]
    o_ref[...] = x


def _combine(dest_tiles, x1, gates, g_final, yb, tm, final_norm):
    n, d = x1.shape
    return pl.pallas_call(
        functools.partial(_combine_kernel, tm=tm, final_norm=final_norm),
        out_shape=jax.ShapeDtypeStruct((n, d), F32),
        grid=(n // tm,),
        in_specs=[
            pl.BlockSpec((1, 1, 2 * tm), lambda i: (i, 0, 0), memory_space=pltpu.SMEM),
            pl.BlockSpec((tm, d), lambda i: (i, 0)),
            pl.BlockSpec((tm, 2), lambda i: (i, 0)),
            pl.BlockSpec((1, d), lambda i: (0, 0)),
            pl.BlockSpec(memory_space=pl.ANY),
        ],
        out_specs=pl.BlockSpec((tm, d), lambda i: (i, 0)),
        scratch_shapes=[pltpu.VMEM((2 * tm, d), F32), pltpu.SemaphoreType.DMA],
        compiler_params=_params(("arbitrary",)),
        name="combine",
    )(dest_tiles, x1, gates, g_final, yb)


def _rope_tables(seq):
    half = DA_HEAD_DIM // 2
    inv_freq = ROPE_THETA ** (-(jnp.arange(half, dtype=F32) * 2.0 / DA_HEAD_DIM))
    ang = jnp.arange(seq, dtype=F32)[:, None] * inv_freq[None, :]
    cos, sin = jnp.cos(ang), jnp.sin(ang)
    cos_tab = jnp.tile(cos, (1, LANES // half))
    sin_tab = jnp.tile(jnp.concatenate([-sin, sin], axis=1), (1, LANES // DA_HEAD_DIM))
    return cos_tab, sin_tab


def _tile_indices(dest, tm):
    n = dest.shape[1]
    return dest.reshape(2, n // tm, tm).transpose(1, 0, 2).reshape(n // tm, 1, 2 * tm)


def _layer(x2d, batch, seq, layer, mix_g, w_in, lam_vecs, sub_g, w_bsb, w_bda, w_out, ffn_g,
           w_grp, w_exp, w_gate, w_up, w_down):
    n, d = x2d.shape
    sbw = SB_HEADS * SB_HEAD_DIM
    daw = DA_HEADS * DA_V_DIM
    qkv_cols = 3 * sbw + 3 * daw
    lam_init = 0.8 - 0.6 * math.exp(-0.3 * layer)

    w_qkv = w_in[:, :qkv_cols].astype(BF16)
    w_gsb = w_in[:, qkv_cols:qkv_cols + d].astype(BF16)
    w_gda = w_in[:, qkv_cols + d:].astype(BF16)
    cos_tab, sin_tab = _rope_tables(seq)
    rope_lo = 3 * sbw // TN_IN
    rope_hi = (3 * sbw + 2 * daw) // TN_IN
    proj = _in_proj(x2d, mix_g[None, :], w_qkv, cos_tab, sin_tab, seq, rope_lo, rope_hi)
    proj3 = proj.reshape(batch, seq, qkv_cols)
    nsb = sbw // SB_HEAD_DIM
    nda = daw // DA_V_DIM
    o_sb = _sb_attention(proj3, batch, seq, 0, nsb, 2 * nsb)
    o_da = _da_attention(lam_vecs, proj3, sub_g[None, :], batch, seq,
                         3 * nsb, 3 * nsb + nda, 3 * nsb + 2 * nda, lam_init)

    wr_t = jnp.zeros((ROUTER_ROWS, d), F32)
    wr_t = wr_t.at[:N_EXPERTS].set(w_exp.T).at[N_EXPERTS:N_EXPERTS + N_GROUPS].set(w_grp.T)
    x1, h2, logits_t = _merge(x2d, mix_g[None, :], o_sb.reshape(n, sbw), o_da.reshape(n, daw),
                              w_gsb, w_gda, w_bsb.astype(BF16), w_bda.astype(BF16),
                              w_out.astype(BF16), ffn_g[None, :], wr_t)

    ids, ranks, gates, counts = _route(logits_t)
    te = TM_EXPERT
    counts = counts[:, 0].astype(jnp.int32)
    padded = (counts + te - 1) // te * te
    pad_ends = jnp.cumsum(padded)
    pad_starts = pad_ends - padded
    expert_ids = jnp.arange(N_EXPERTS, dtype=jnp.int32)
    dest = ranks + jnp.sum(jnp.where(ids[..., None] == expert_ids, pad_starts, 0), axis=-1)
    n_tiles = (2 * n) // te + N_EXPERTS
    n_used = (pad_ends[-1] // te).astype(jnp.int32)[None]
    tile_row0 = jnp.arange(n_tiles, dtype=jnp.int32) * te
    tile_expert = jnp.minimum(jnp.sum((pad_ends[None, :] <= tile_row0[:, None]).astype(jnp.int32), axis=1),
                              N_EXPERTS - 1)
    zero_starts = ((pad_starts + counts) // SUBLANES * SUBLANES).astype(jnp.int32)

    tmd = min(TM_DISPATCH, n)
    xb = _dispatch(zero_starts, n_used, _tile_indices(dest, tmd), h2, n_tiles + 1, tmd, te)
    yb = _experts(tile_expert, n_used, xb, w_gate.astype(BF16), w_up.astype(BF16),
                  w_down.astype(BF16), n_tiles, te)
    return x1, gates, dest, yb


def kernel(x, mix_norm_g, w_in, lambda_q1, lambda_k1, lambda_q2, lambda_k2, diff_subnorm_g, w_branch_sb, w_branch_diff, w_out, ffn_norm_g, w_group_router, w_expert_router, w_gate, w_up, w_down, final_norm_g):
    batch, seq, d = x.shape
    depth = w_in.shape[0]
    n = batch * seq
    x2d = x.reshape(n, d)
    tmc = min(TM_COMBINE, n)
    for l in range(depth):
        lam_vecs = jnp.stack([lambda_q1[l], lambda_k1[l], lambda_q2[l], lambda_k2[l]]).astype(F32)
        x1, gates, dest, yb = _layer(
            x2d, batch, seq, l, mix_norm_g[l], w_in[l], lam_vecs, diff_subnorm_g[l], w_branch_sb[l],
            w_branch_diff[l], w_out[l], ffn_norm_g[l], w_group_router[l], w_expert_router[l],
            w_gate[l], w_up[l], w_down[l])
        x2d = _combine(_tile_indices(dest, tmc), x1, gates.T, final_norm_g[None, :], yb, tmc,
                       final_norm=(l == depth - 1))
    return x2d.reshape(batch, seq, d)
```

```python
import functools
import math

import jax
import jax.numpy as jnp
from jax import lax
from jax.experimental import pallas as pl
from jax.experimental.pallas import tpu as pltpu

F32 = jnp.float32
BF16 = jnp.bfloat16

CHUNK = 64
SB_HEADS = 8
SB_HEAD_DIM = 128
DA_HEADS = 8
DA_HEAD_DIM = 64
DA_V_DIM = 2 * DA_HEAD_DIM
N_GROUPS = 4
EXPERTS_PER_GROUP = 8
N_EXPERTS = N_GROUPS * EXPERTS_PER_GROUP
ROPE_THETA = 10000.0
EPS = 1e-6

LANES = 128
SUBLANES = 8
ROUTER_ROWS = 40
VMEM_LIMIT_BYTES = 56 * 1024 * 1024

SB_LOG_WEIGHT_FLOOR = -104.0
SB_MASKED_LOGIT = -1e30
LOG2_E = 1.4426950408889634

TM_IN = 1024
TN_IN = 1024
TQ_SB = 256
TK_SB = 128
SB_TILES_PER_STEP = 4
TQ_DA = 256
DA_TILES_PER_STEP = 4
TM_GATE = 1024
TN_GATE = 256
TM_OUT = 512
OUT_SUB_TILES = 1
T_ROUTE = 512
TM_DISPATCH = 1024
TM_EXPERT = 256
TM_COMBINE = 256


def _params(sem, vmem=VMEM_LIMIT_BYTES):
    return pltpu.CompilerParams(dimension_semantics=sem, vmem_limit_bytes=vmem)


def _nt_dot(a, b, **kw):
    return lax.dot_general(a, b, (((1,), (1,)), ((), ())), preferred_element_type=F32, **kw)


def _in_proj_kernel(x_ref, g_ref, w_ref, cos_ref, sin_ref, o_ref, h_ref, *, rope_lo, rope_hi):
    j = pl.program_id(1)

    @pl.when(j == 0)
    def _():
        x = x_ref[...]
        ms = jnp.mean(x * x, axis=-1, keepdims=True)
        h_ref[...] = (x * lax.rsqrt(ms + EPS) * g_ref[...]).astype(BF16)

    is_rope = jnp.logical_and(j >= rope_lo, j < rope_hi)

    @pl.when(is_rope)
    def _():
        y = jnp.dot(h_ref[...], w_ref[...], preferred_element_type=F32)
        cos = cos_ref[...]
        sin = sin_ref[...]
        lane = lax.broadcasted_iota(jnp.int32, cos.shape, 1)
        first_half = (lane % DA_HEAD_DIM) < (DA_HEAD_DIM // 2)
        for hh in range(y.shape[1] // LANES):
            yh = y[:, hh * LANES:(hh + 1) * LANES]
            partner = jnp.where(first_half,
                                pltpu.roll(yh, LANES - DA_HEAD_DIM // 2, 1),
                                pltpu.roll(yh, DA_HEAD_DIM // 2, 1))
            o_ref[:, hh * LANES:(hh + 1) * LANES] = (yh * cos + partner * sin).astype(BF16)

    @pl.when(jnp.logical_not(is_rope))
    def _():
        o_ref[...] = jnp.dot(h_ref[...], w_ref[...], preferred_element_type=F32).astype(BF16)


def _in_proj(x2d, g, w_bf16, cos_tab, sin_tab, seq, rope_lo, rope_hi):
    n, d = x2d.shape
    cols = w_bf16.shape[1]
    tm, tn = min(TM_IN, seq), TN_IN
    pos_tiles = seq // tm
    return pl.pallas_call(
        functools.partial(_in_proj_kernel, rope_lo=rope_lo, rope_hi=rope_hi),
        out_shape=jax.ShapeDtypeStruct((n, cols), BF16),
        grid=(n // tm, cols // tn),
        in_specs=[
            pl.BlockSpec((tm, d), lambda i, j: (i, 0)),
            pl.BlockSpec((1, d), lambda i, j: (0, 0)),
            pl.BlockSpec((d, tn), lambda i, j: (0, j)),
            pl.BlockSpec((tm, LANES), lambda i, j: (i % pos_tiles, 0)),
            pl.BlockSpec((tm, LANES), lambda i, j: (i % pos_tiles, 0)),
        ],
        out_specs=pl.BlockSpec((tm, tn), lambda i, j: (i, j)),
        scratch_shapes=[pltpu.VMEM((tm, d), BF16)],
        compiler_params=_params(("parallel", "arbitrary")),
        name="in_proj",
    )(x2d, g, w_bf16, cos_tab, sin_tab)


def _sb_kernel(q_ref, k_ref, v_ref, o_ref, *, tq, tk, tiles_per_step, scale):
    i = pl.program_id(2)
    row = lax.broadcasted_iota(jnp.int32, (2 * tk, 2 * tk), 0) % tk
    col = lax.broadcasted_iota(jnp.int32, (2 * tk, 2 * tk), 1)
    u2 = jnp.where(jnp.logical_or(col >= tk, row > col), 1.0, 0.0).astype(BF16)

    def span(q, start, width, diag_cols, running):
        k = k_ref[pl.ds(start, width), :]
        v = v_ref[pl.ds(start, width), :]
        z = _nt_dot(q, k) * scale
        if diag_cols is not None:
            r = lax.broadcasted_iota(jnp.int32, z.shape, 0)
            c = lax.broadcasted_iota(jnp.int32, z.shape, 1)
            z = jnp.where((c - diag_cols) < r, z, SB_MASKED_LOGIT)
        sp = jnp.maximum(z, 0.0) + jnp.log(1.0 + jnp.exp2(jnp.abs(z) * (-LOG2_E)))
        hi = sp.astype(BF16)
        lo = (sp - hi.astype(F32)).astype(BF16)
        nb = width // tk
        suffix = [None] * nb
        for blk in range(nb - 1, -1, -1):
            sl = slice(blk * tk, (blk + 1) * tk)
            cs = jnp.dot(jnp.concatenate([hi[:, sl], lo[:, sl]], axis=1), u2, preferred_element_type=F32)
            suffix[blk] = cs[:, :tk] + running
            running = running + cs[:, tk:]
        a = jnp.exp((z - sp) - jnp.concatenate(suffix, axis=1))
        return jnp.dot(a.astype(BF16), v, preferred_element_type=F32), running

    zero_run = jnp.zeros((tq, tk), F32)

    def far_keys(q, first, running, acc):
        def alive_after(running):
            return jnp.min(running) < -SB_LOG_WEIGHT_FLOOR

        def cond(c):
            j, alive, _, _ = c
            return jnp.logical_and(j >= 0, alive)

        def body(c):
            j, _, running, acc = c
            out, running = span(q, pl.multiple_of(j * tq, tq), tq, None, running)
            return j - 1, alive_after(running), running, acc + out

        return lax.while_loop(cond, body, (first, alive_after(running), running, acc))[3]

    def tiles(first_tile, head_has_no_past):
        near = []
        for sub in range(tiles_per_step):
            q = q_ref[sub * tq:(sub + 1) * tq, :]
            if sub == 0 and head_has_no_past:
                near.append(span(q, 0, tq, 0, zero_run))
            else:
                start = (first_tile + sub - 1) * tq
                if not isinstance(start, int):
                    start = pl.multiple_of(start, tq)
                near.append(span(q, start, 2 * tq, tq, zero_run))
        for sub in range(tiles_per_step):
            acc, running = near[sub]
            if not (head_has_no_past and sub < 2):
                acc = far_keys(q_ref[sub * tq:(sub + 1) * tq, :], first_tile + sub - 2, running, acc)
            o_ref[sub * tq:(sub + 1) * tq, :] = acc.astype(o_ref.dtype)

    @pl.when(i == 0)
    def _():
        tiles(0, True)

    @pl.when(i > 0)
    def _():
        tiles(i * tiles_per_step, False)


def _sb_attention(proj3, batch, seq, col_q, col_k, col_v):
    d = SB_HEAD_DIM
    tq, tk = min(TQ_SB, seq), min(TK_SB, seq)
    tps = min(SB_TILES_PER_STEP, seq // tq)
    rows = tq * tps
    return pl.pallas_call(
        functools.partial(_sb_kernel, tq=tq, tk=tk, tiles_per_step=tps, scale=1.0 / math.sqrt(d)),
        out_shape=jax.ShapeDtypeStruct((batch, seq, SB_HEADS * d), BF16),
        grid=(batch, SB_HEADS, seq // rows),
        in_specs=[
            pl.BlockSpec((None, rows, d), lambda b, h, i: (b, i, col_q + h)),
            pl.BlockSpec((None, seq, d), lambda b, h, i: (b, 0, col_k + h)),
            pl.BlockSpec((None, seq, d), lambda b, h, i: (b, 0, col_v + h)),
        ],
        out_specs=pl.BlockSpec((None, rows, d), lambda b, h, i: (b, i, h)),
        compiler_params=_params(("parallel", "parallel", "arbitrary")),
        name="sb_attn",
    )(proj3, proj3, proj3)


def _da_kernel(lam_ref, q_ref, k_ref, v_ref, g_ref, o_ref, vext_ref, *, tq, tiles_per_step, n_steps,
               lam_init):
    i = pl.program_id(2)

    @pl.when(i == 0)
    def _():
        vext_ref[:, :DA_V_DIM] = v_ref[...]
        vext_ref[:, DA_V_DIM:] = jnp.ones((vext_ref.shape[0], DA_V_DIM), BF16)

    lv = lam_ref[...]
    lam = (jnp.exp(jnp.sum(lv[0:1] * lv[1:2], axis=-1, keepdims=True))
           - jnp.exp(jnp.sum(lv[2:3] * lv[3:4], axis=-1, keepdims=True)) + lam_init)

    q = q_ref[...]
    lane = lax.broadcasted_iota(jnp.int32, q.shape, 1)
    qs = q * (1.0 / math.sqrt(DA_HEAD_DIM))
    zero = jnp.zeros_like(qs)
    q_maps = (jnp.where(lane < DA_HEAD_DIM, qs, zero), jnp.where(lane >= DA_HEAD_DIM, qs, zero))
    r = lax.broadcasted_iota(jnp.int32, (tq, tq), 0)
    c = lax.broadcasted_iota(jnp.int32, (tq, tq), 1)
    diag_mask = (c // CHUNK) <= (r // CHUNK)

    def softmax_av(qm, lo):
        sd = jnp.where(diag_mask, _nt_dot(qm, k_ref[lo:lo + tq, :]), -jnp.inf)
        m = jnp.max(sd, axis=-1, keepdims=True)
        if lo > 0:
            sl = _nt_dot(qm, k_ref[0:lo, :])
            m = jnp.maximum(m, jnp.max(sl, axis=-1, keepdims=True))
        acc = jnp.dot(jnp.exp(sd - m).astype(BF16), vext_ref[lo:lo + tq, :], preferred_element_type=F32)
        if lo > 0:
            acc = acc + jnp.dot(jnp.exp(sl - m).astype(BF16), vext_ref[0:lo, :],
                                preferred_element_type=F32)
        return acc[:, :DA_V_DIM] / acc[:, DA_V_DIM:]

    for step in range(n_steps):
        @pl.when(i == step)
        def _(step=step):
            for sub in range(tiles_per_step):
                rows = slice(sub * tq, (sub + 1) * tq)
                lo = (step * tiles_per_step + sub) * tq
                o = (softmax_av(q_maps[0][rows], lo) - lam * softmax_av(q_maps[1][rows], lo))
                ms = jnp.mean(o * o, axis=-1, keepdims=True)
                o_ref[rows, :] = (o * lax.rsqrt(ms + EPS) * g_ref[...] * (1.0 - lam_init)).astype(o_ref.dtype)


def _da_attention(lam_vecs, proj3, sub_g, batch, seq, col_q, col_k, col_v, lam_init):
    d = DA_V_DIM
    tq = min(TQ_DA, seq)
    tps = min(DA_TILES_PER_STEP, seq // tq)
    rows = tq * tps
    return pl.pallas_call(
        functools.partial(_da_kernel, tq=tq, tiles_per_step=tps, n_steps=seq // rows, lam_init=lam_init),
        out_shape=jax.ShapeDtypeStruct((batch, seq, DA_HEADS * d), BF16),
        grid=(batch, DA_HEADS, seq // rows),
        in_specs=[
            pl.BlockSpec(lam_vecs.shape, lambda b, h, i: (0, 0)),
            pl.BlockSpec((None, rows, d), lambda b, h, i: (b, i, col_q + h)),
            pl.BlockSpec((None, seq, d), lambda b, h, i: (b, 0, col_k + h)),
            pl.BlockSpec((None, seq, d), lambda b, h, i: (b, 0, col_v + h)),
            pl.BlockSpec((1, d), lambda b, h, i: (0, 0)),
        ],
        out_specs=pl.BlockSpec((None, rows, d), lambda b, h, i: (b, i, h)),
        scratch_shapes=[pltpu.VMEM((seq, 2 * d), BF16)],
        compiler_params=_params(("parallel", "parallel", "arbitrary")),
        name="da_attn",
    )(lam_vecs, proj3, proj3, proj3, sub_g)


def _gate_branch_kernel(x_ref, gmix_ref, osb_ref, oda_ref, wgs_ref, wgd_ref, wbs_ref, wbd_ref, m_ref, h_scr):
    @pl.when(pl.program_id(1) == 0)
    def _():
        x = x_ref[...]
        ms = jnp.mean(x * x, axis=-1, keepdims=True)
        h_scr[...] = (x * lax.rsqrt(ms + EPS) * gmix_ref[...]).astype(BF16)

    h = h_scr[...]
    gate_sb = jax.nn.sigmoid(jnp.dot(h, wgs_ref[...], preferred_element_type=F32))
    gate_da = jax.nn.sigmoid(jnp.dot(h, wgd_ref[...], preferred_element_type=F32))
    br_sb = jnp.dot(osb_ref[...], wbs_ref[...], preferred_element_type=F32)
    br_da = jnp.dot(oda_ref[...], wbd_ref[...], preferred_element_type=F32)
    m_ref[...] = (gate_sb * br_sb + gate_da * br_da).astype(m_ref.dtype)


def _gate_branch(x2d, gmix, o_sb, o_da, wgs, wgd, wbs, wbd):
    n, d = x2d.shape
    tm, tn = min(TM_GATE, n), TN_GATE
    wsb, wda = o_sb.shape[1], o_da.shape[1]
    return pl.pallas_call(
        _gate_branch_kernel,
        out_shape=jax.ShapeDtypeStruct((n, d), BF16),
        grid=(n // tm, d // tn),
        in_specs=[
            pl.BlockSpec((tm, d), lambda i, j: (i, 0)),
            pl.BlockSpec((1, d), lambda i, j: (0, 0)),
            pl.BlockSpec((tm, wsb), lambda i, j: (i, 0)),
            pl.BlockSpec((tm, wda), lambda i, j: (i, 0)),
            pl.BlockSpec((d, tn), lambda i, j: (0, j)),
            pl.BlockSpec((d, tn), lambda i, j: (0, j)),
            pl.BlockSpec((wsb, tn), lambda i, j: (0, j)),
            pl.BlockSpec((wda, tn), lambda i, j: (0, j)),
        ],
        out_specs=pl.BlockSpec((tm, tn), lambda i, j: (i, j)),
        scratch_shapes=[pltpu.VMEM((tm, d), BF16)],
        compiler_params=_params(("parallel", "arbitrary")),
        name="gate_branch",
    )(x2d, gmix, o_sb, o_da, wgs, wgd, wbs, wbd)


def _out_proj_kernel(m_ref, x_ref, wo_ref, gffn_ref, wr_ref, x1_ref, h2_ref, lt_ref, *, n_sub):
    sub = x_ref.shape[0] // n_sub
    for s in range(n_sub):
        rows = slice(s * sub, (s + 1) * sub)
        x1 = x_ref[rows, :] + jnp.dot(m_ref[rows, :], wo_ref[...], preferred_element_type=F32)
        x1_ref[rows, :] = x1
        ms = jnp.mean(x1 * x1, axis=-1, keepdims=True)
        h2 = x1 * lax.rsqrt(ms + EPS) * gffn_ref[...]
        h2_ref[rows, :] = h2
        lt_ref[:, rows] = _nt_dot(wr_ref[...], h2, precision=lax.Precision.HIGHEST)


def _out_proj(merged, x2d, wo, gffn, wr_t):
    n, d = x2d.shape
    tm = min(TM_OUT, n)
    once = pl.Buffered(1)
    return pl.pallas_call(
        functools.partial(_out_proj_kernel, n_sub=OUT_SUB_TILES),
        out_shape=(jax.ShapeDtypeStruct((n, d), F32),
                   jax.ShapeDtypeStruct((n, d), F32),
                   jax.ShapeDtypeStruct((ROUTER_ROWS, n), F32)),
        grid=(n // tm,),
        in_specs=[
            pl.BlockSpec((tm, d), lambda i: (i, 0)),
            pl.BlockSpec((tm, d), lambda i: (i, 0)),
            pl.BlockSpec((d, d), lambda i: (0, 0), pipeline_mode=once),
            pl.BlockSpec((1, d), lambda i: (0, 0), pipeline_mode=once),
            pl.BlockSpec((ROUTER_ROWS, d), lambda i: (0, 0), pipeline_mode=once),
        ],
        out_specs=(pl.BlockSpec((tm, d), lambda i: (i, 0)),
                   pl.BlockSpec((tm, d), lambda i: (i, 0)),
                   pl.BlockSpec((ROUTER_ROWS, tm), lambda i: (0, i))),
        compiler_params=_params(("parallel",)),
        name="out_proj",
    )(merged, x2d, wo, gffn, wr_t)


def _route_kernel(lt_ref, ids_ref, rank_ref, gate_ref, cnt_ref, carry_ref):
    step = pl.program_id(0)

    @pl.when(step == 0)
    def _():
        carry_ref[...] = jnp.zeros_like(carry_ref)

    t = lt_ref.shape[1]
    el = lt_ref[0:N_EXPERTS, :]
    gl = lt_ref[N_EXPERTS:N_EXPERTS + 8, :]
    grow = lax.broadcasted_iota(jnp.int32, gl.shape, 0)
    gl = jnp.where(grow < N_GROUPS, gl, -jnp.inf)
    gmax = jnp.max(gl, axis=0, keepdims=True)
    grp = jnp.min(jnp.where(gl == gmax, grow, N_GROUPS), axis=0, keepdims=True)
    p_grp = 1.0 / jnp.sum(jnp.exp(gl - gmax), axis=0, keepdims=True)

    eidx = lax.broadcasted_iota(jnp.int32, el.shape, 0)
    in_grp = (eidx // EXPERTS_PER_GROUP) == grp
    e1 = jnp.where(in_grp, el, -jnp.inf)
    top1 = jnp.max(e1, axis=0, keepdims=True)
    idx1 = jnp.min(jnp.where(e1 == top1, eidx, N_EXPERTS), axis=0, keepdims=True)
    e2 = jnp.where(eidx == idx1, -jnp.inf, e1)
    top2 = jnp.max(e2, axis=0, keepdims=True)
    idx2 = jnp.min(jnp.where(e2 == top2, eidx, N_EXPERTS), axis=0, keepdims=True)
    dlt = jnp.exp(top2 - top1)
    den = 1.0 + dlt
    gate_ref[0:1, :] = p_grp / den
    gate_ref[1:2, :] = p_grp * dlt / den
    ids_ref[0:1, :] = idx1
    ids_ref[1:2, :] = idx2

    oh1 = jnp.where(eidx == idx1, 1.0, 0.0)
    oh2 = jnp.where(eidx == idx2, 1.0, 0.0)
    rr = lax.broadcasted_iota(jnp.int32, (t, t), 0)
    cc = lax.broadcasted_iota(jnp.int32, (t, t), 1)
    before = jnp.where(rr < cc, 1.0, 0.0).astype(BF16)
    pre1 = jnp.dot(oh1.astype(BF16), before, preferred_element_type=F32)
    pre2 = jnp.dot(oh2.astype(BF16), before, preferred_element_type=F32)
    cnt1 = jnp.sum(oh1, axis=1, keepdims=True)
    cnt2 = jnp.sum(oh2, axis=1, keepdims=True)
    carry = carry_ref[:, 0:1]
    rank1 = jnp.sum(oh1 * (carry + pre1), axis=0, keepdims=True)
    rank2 = jnp.sum(oh2 * (carry + cnt1 + pre2), axis=0, keepdims=True)
    rank_ref[0:1, :] = rank1.astype(jnp.int32)
    rank_ref[1:2, :] = rank2.astype(jnp.int32)
    new_carry = carry_ref[...] + cnt1 + cnt2
    carry_ref[...] = new_carry
    cnt_ref[...] = new_carry


def _route(logits_t):
    n = logits_t.shape[1]
    t = min(T_ROUTE, n)
    return pl.pallas_call(
        _route_kernel,
        out_shape=(jax.ShapeDtypeStruct((2, n), jnp.int32),
                   jax.ShapeDtypeStruct((2, n), jnp.int32),
                   jax.ShapeDtypeStruct((2, n), F32),
                   jax.ShapeDtypeStruct((N_EXPERTS, LANES), F32)),
        grid=(n // t,),
        in_specs=[pl.BlockSpec((ROUTER_ROWS, t), lambda i: (0, i))],
        out_specs=(pl.BlockSpec((2, t), lambda i: (0, i)),
                   pl.BlockSpec((2, t), lambda i: (0, i)),
                   pl.BlockSpec((2, t), lambda i: (0, i)),
                   pl.BlockSpec((N_EXPERTS, LANES), lambda i: (0, 0))),
        scratch_shapes=[pltpu.VMEM((N_EXPERTS, LANES), F32)],
        compiler_params=_params(("arbitrary",)),
        name="route",
    )(logits_t)


def _dispatch_kernel(zs_ref, nu_ref, dest_ref, h2_ref, xb_hbm, zeros_ref, sem, tail_sem, *, tm, te, n_tiles):
    step = pl.program_id(0)

    def tail_copy(t):
        return pltpu.make_async_copy(zeros_ref, xb_hbm.at[pl.ds(pl.multiple_of(t * te, te), te)], tail_sem)

    @pl.when(step == 0)
    def _():
        zeros_ref[...] = jnp.zeros_like(zeros_ref)

        def zero_copy(e):
            start = pl.multiple_of(zs_ref[e], SUBLANES)
            return pltpu.make_async_copy(zeros_ref, xb_hbm.at[pl.ds(start, te)], sem)

        for e in range(N_EXPERTS):
            zero_copy(e).start()
        for e in range(N_EXPERTS):
            zero_copy(e).wait()
        lax.fori_loop(nu_ref[0], n_tiles, lambda t, c: (tail_copy(t).start(), c)[1], 0)

    def issue(blk, _):
        r0 = pl.multiple_of(blk * SUBLANES, SUBLANES)
        for u in range(SUBLANES):
            src = h2_ref.at[pl.ds(r0 + u, 1)]
            for k in range(2):
                pltpu.make_async_copy(src, xb_hbm.at[pl.ds(dest_ref[0, 0, k * tm + r0 + u], 1)], sem).start()
        return 0

    lax.fori_loop(0, tm // SUBLANES, issue, 0)
    for _ in range(2):
        pltpu.make_async_copy(h2_ref, xb_hbm.at[pl.ds(0, tm)], sem).wait()

    @pl.when(step == pl.num_programs(0) - 1)
    def _():
        lax.fori_loop(nu_ref[0], n_tiles, lambda t, c: (tail_copy(t).wait(), c)[1], 0)


def _dispatch(zero_starts, n_used, dest_tiles, h2, n_tiles, tm, te):
    n, d = h2.shape
    return pl.pallas_call(
        functools.partial(_dispatch_kernel, tm=tm, te=te, n_tiles=n_tiles),
        out_shape=jax.ShapeDtypeStruct((n_tiles * te, d), h2.dtype),
        grid_spec=pltpu.PrefetchScalarGridSpec(
            num_scalar_prefetch=2,
            grid=(n // tm,),
            in_specs=[
                pl.BlockSpec((1, 1, 2 * tm), lambda i, zs, nu: (i, 0, 0), memory_space=pltpu.SMEM),
                pl.BlockSpec((tm, d), lambda i, zs, nu: (i, 0)),
            ],
            out_specs=pl.BlockSpec(memory_space=pl.ANY),
            scratch_shapes=[pltpu.VMEM((te, d), h2.dtype), pltpu.SemaphoreType.DMA,
                            pltpu.SemaphoreType.DMA],
        ),
        compiler_params=_params(("arbitrary",)),
        name="dispatch",
    )(zero_starts, n_used, dest_tiles, h2)


def _expert_kernel(te_ref, nu_ref, slot_ref, next_ref, xb_ref, wg_hbm, wu_hbm, wd_hbm, yb_ref,
                   wg_f, wu_f, wd_f, wg_s, wu_s, wd_s, sems):
    t = pl.program_id(0)
    used = t < nu_ref[0]
    new_run = jnp.logical_or(t == 0, te_ref[t] != te_ref[jnp.maximum(t - 1, 0)])

    def weight_copies(expert, slot):
        return [pltpu.make_async_copy(src.at[expert], dst.at[slot], sems.at[slot, i])
                for i, (src, dst) in enumerate(((wg_hbm, wg_f), (wu_hbm, wu_f), (wd_hbm, wd_f)))]

    @pl.when(t == 0)
    def _():
        for c in weight_copies(te_ref[0], 0):
            c.start()

    @pl.when(jnp.logical_and(used, new_run))
    def _():
        slot = slot_ref[t]
        for c in weight_copies(te_ref[t], slot):
            c.wait()

        @pl.when(next_ref[t] >= 0)
        def _():
            for c in weight_copies(next_ref[t], 1 - slot):
                c.start()

        wg_s[...] = wg_f[slot].astype(BF16)
        wu_s[...] = wu_f[slot].astype(BF16)
        wd_s[...] = wd_f[slot].astype(BF16)

    @pl.when(used)
    def _():
        x = xb_ref[...].astype(BF16)
        g = jnp.dot(x, wg_s[...], preferred_element_type=F32)
        u = jnp.dot(x, wu_s[...], preferred_element_type=F32)
        hid = (g * jax.nn.sigmoid(g)) * u
        yb_ref[...] = jnp.dot(hid.astype(BF16), wd_s[...], preferred_element_type=F32)

    @pl.when(jnp.logical_not(used))
    def _():
        yb_ref[...] = jnp.zeros_like(yb_ref)


def _experts(tile_expert, n_used, run_slot, next_expert, xb, wg, wu, wd, n_tiles, te):
    d = xb.shape[1]
    de = wg.shape[2]

    def row_map(t, te_ref, nu_ref, slot_ref, next_ref):
        return (jnp.minimum(t, nu_ref[0] - 1), 0)

    hbm = pl.BlockSpec(memory_space=pl.ANY)
    return pl.pallas_call(
        _expert_kernel,
        out_shape=jax.ShapeDtypeStruct((n_tiles * te, d), F32),
        grid_spec=pltpu.PrefetchScalarGridSpec(
            num_scalar_prefetch=4,
            grid=(n_tiles,),
            in_specs=[pl.BlockSpec((te, d), row_map), hbm, hbm, hbm],
            out_specs=pl.BlockSpec((te, d), lambda t, *_: (t, 0)),
            scratch_shapes=[pltpu.VMEM((2, d, de), F32), pltpu.VMEM((2, d, de), F32), pltpu.VMEM((2, de, d), F32),
                            pltpu.VMEM((d, de), BF16), pltpu.VMEM((d, de), BF16), pltpu.VMEM((de, d), BF16),
                            pltpu.SemaphoreType.DMA((2, 3))],
        ),
        compiler_params=_params(("arbitrary",)),
        name="experts",
    )(tile_expert, n_used, run_slot, next_expert, xb, wg, wu, wd)


def _combine_kernel(dest_ref, next_dest_ref, x1_ref, gate_ref, g_ref, yb_hbm, o_ref, buf, sems, *, tm,
                    final_norm):
    i = pl.program_id(0)
    slot = i % 2

    def start_rows(idx_ref, slot):
        def issue(blk, _):
            r0 = pl.multiple_of(blk * SUBLANES, SUBLANES)
            for u in range(SUBLANES):
                for k in range(2):
                    row = k * tm + r0 + u
                    pltpu.make_async_copy(yb_hbm.at[pl.ds(idx_ref[0, 0, row], 1)],
                                          buf.at[slot, pl.ds(row, 1)], sems.at[slot]).start()
            return 0
        lax.fori_loop(0, tm // SUBLANES, issue, 0)

    @pl.when(i == 0)
    def _():
        start_rows(dest_ref, 0)

    @pl.when(i + 1 < pl.num_programs(0))
    def _():
        start_rows(next_dest_ref, 1 - slot)

    pltpu.make_async_copy(yb_hbm.at[pl.ds(0, 2 * tm)], buf.at[slot], sems.at[slot]).wait()
    gates = gate_ref[...]
    g0, g1 = gates[:, 0:1], gates[:, 1:2]
    x = x1_ref[...] + (buf[slot, 0:tm, :] * g0 + buf[slot, tm:2 * tm, :] * g1)
    if final_norm:
        ms = jnp.mean(x * x, axis=-1, keepdims=True)
        x = x * lax.rsqrt(ms + EPS) * g_ref[...]
    o_ref[...] = x


def _combine(dest_tiles, x1, gates, g_final, yb, tm, final_norm):
    n, d = x1.shape
    n_steps = n // tm
    return pl.pallas_call(
        functools.partial(_combine_kernel, tm=tm, final_norm=final_norm),
        out_shape=jax.ShapeDtypeStruct((n, d), F32),
        grid=(n_steps,),
        in_specs=[
            pl.BlockSpec((1, 1, 2 * tm), lambda i: (i, 0, 0), memory_space=pltpu.SMEM),
            pl.BlockSpec((1, 1, 2 * tm), lambda i: (jnp.minimum(i + 1, n_steps - 1), 0, 0),
                         memory_space=pltpu.SMEM),
            pl.BlockSpec((tm, d), lambda i: (i, 0)),
            pl.BlockSpec((tm, 2), lambda i: (i, 0)),
            pl.BlockSpec((1, d), lambda i: (0, 0)),
            pl.BlockSpec(memory_space=pl.ANY),
        ],
        out_specs=pl.BlockSpec((tm, d), lambda i: (i, 0)),
        scratch_shapes=[pltpu.VMEM((2, 2 * tm, d), F32), pltpu.SemaphoreType.DMA((2,))],
        compiler_params=_params(("arbitrary",)),
        name="combine",
    )(dest_tiles, dest_tiles, x1, gates, g_final, yb)


def _rope_tables(seq):
    half = DA_HEAD_DIM // 2
    inv_freq = ROPE_THETA ** (-(jnp.arange(half, dtype=F32) * 2.0 / DA_HEAD_DIM))
    ang = jnp.arange(seq, dtype=F32)[:, None] * inv_freq[None, :]
    cos, sin = jnp.cos(ang), jnp.sin(ang)
    cos_tab = jnp.tile(cos, (1, LANES // half))
    sin_tab = jnp.tile(jnp.concatenate([-sin, sin], axis=1), (1, LANES // DA_HEAD_DIM))
    return cos_tab, sin_tab


def _tile_indices(dest, tm):
    n = dest.shape[1]
    return dest.reshape(2, n // tm, tm).transpose(1, 0, 2).reshape(n // tm, 1, 2 * tm)


def _layer(x2d, batch, seq, layer, mix_g, w_in, lam_vecs, sub_g, w_bsb, w_bda, w_out, ffn_g,
           w_grp, w_exp, w_gate, w_up, w_down):
    n, d = x2d.shape
    sbw = SB_HEADS * SB_HEAD_DIM
    daw = DA_HEADS * DA_V_DIM
    qkv_cols = 3 * sbw + 3 * daw
    lam_init = 0.8 - 0.6 * math.exp(-0.3 * layer)

    w_qkv = w_in[:, :qkv_cols].astype(BF16)
    w_gsb = w_in[:, qkv_cols:qkv_cols + d].astype(BF16)
    w_gda = w_in[:, qkv_cols + d:].astype(BF16)
    cos_tab, sin_tab = _rope_tables(seq)
    rope_lo = 3 * sbw // TN_IN
    rope_hi = (3 * sbw + 2 * daw) // TN_IN
    proj = _in_proj(x2d, mix_g[None, :], w_qkv, cos_tab, sin_tab, seq, rope_lo, rope_hi)
    proj3 = proj.reshape(batch, seq, qkv_cols)
    nsb = sbw // SB_HEAD_DIM
    nda = daw // DA_V_DIM
    o_sb = _sb_attention(proj3, batch, seq, 0, nsb, 2 * nsb)
    o_da = _da_attention(lam_vecs, proj3, sub_g[None, :], batch, seq,
                         3 * nsb, 3 * nsb + nda, 3 * nsb + 2 * nda, lam_init)

    wr_t = jnp.zeros((ROUTER_ROWS, d), F32)
    wr_t = wr_t.at[:N_EXPERTS].set(w_exp.T).at[N_EXPERTS:N_EXPERTS + N_GROUPS].set(w_grp.T)
    merged = _gate_branch(x2d, mix_g[None, :], o_sb.reshape(n, sbw), o_da.reshape(n, daw),
                          w_gsb, w_gda, w_bsb.astype(BF16), w_bda.astype(BF16))
    x1, h2, logits_t = _out_proj(merged, x2d, w_out.astype(BF16), ffn_g[None, :], wr_t)

    ids, ranks, gates, counts = _route(logits_t)
    te = TM_EXPERT
    counts = counts[:, 0].astype(jnp.int32)
    padded = (counts + te - 1) // te * te
    pad_ends = jnp.cumsum(padded)
    pad_starts = pad_ends - padded
    expert_ids = jnp.arange(N_EXPERTS, dtype=jnp.int32)
    dest = ranks + jnp.sum(jnp.where(ids[..., None] == expert_ids, pad_starts, 0), axis=-1)
    n_tiles = (2 * n) // te + N_EXPERTS
    n_used = (pad_ends[-1] // te).astype(jnp.int32)[None]
    tile_row0 = jnp.arange(n_tiles, dtype=jnp.int32) * te
    tile_expert = jnp.minimum(jnp.sum((pad_ends[None, :] <= tile_row0[:, None]).astype(jnp.int32), axis=1),
                              N_EXPERTS - 1)
    zero_starts = ((pad_starts + counts) // SUBLANES * SUBLANES).astype(jnp.int32)

    tmd = min(TM_DISPATCH, n)
    xb = _dispatch(zero_starts, n_used, _tile_indices(dest, tmd), h2, n_tiles + 1, tmd, te)
    tile_ids = jnp.arange(n_tiles, dtype=jnp.int32)
    new_run = jnp.logical_or(tile_ids == 0, tile_expert != jnp.roll(tile_expert, 1))
    run_slot = ((jnp.cumsum(new_run.astype(jnp.int32)) - 1) % 2).astype(jnp.int32)
    later = jnp.logical_and(tile_ids[None, :] < n_used, tile_expert[None, :] > tile_expert[:, None])
    next_expert = jnp.min(jnp.where(later, tile_expert[None, :], N_EXPERTS), axis=1)
    next_expert = jnp.where(next_expert == N_EXPERTS, -1, next_expert).astype(jnp.int32)
    yb = _experts(tile_expert, n_used, run_slot, next_expert, xb, w_gate, w_up, w_down, n_tiles, te)
    return x1, gates, dest, yb


def kernel(x, mix_norm_g, w_in, lambda_q1, lambda_k1, lambda_q2, lambda_k2, diff_subnorm_g, w_branch_sb, w_branch_diff, w_out, ffn_norm_g, w_group_router, w_expert_router, w_gate, w_up, w_down, final_norm_g):
    batch, seq, d = x.shape
    depth = w_in.shape[0]
    n = batch * seq
    x2d = x.reshape(n, d)
    tmc = min(TM_COMBINE, n)
    for l in range(depth):
        lam_vecs = jnp.stack([lambda_q1[l], lambda_k1[l], lambda_q2[l], lambda_k2[l]]).astype(F32)
        x1, gates, dest, yb = _layer(
            x2d, batch, seq, l, mix_norm_g[l], w_in[l], lam_vecs, diff_subnorm_g[l], w_branch_sb[l],
            w_branch_diff[l], w_out[l], ffn_norm_g[l], w_group_router[l], w_expert_router[l],
            w_gate[l], w_up[l], w_down[l])
        x2d = _combine(_tile_indices(dest, tmc), x1, gates.T, final_norm_g[None, :], yb, tmc,
                       final_norm=(l == depth - 1))
    return x2d.reshape(batch, seq, d)
```

```python
import functools
import math

import jax
import jax.numpy as jnp
from jax import lax
from jax.experimental import pallas as pl
from jax.experimental.pallas import tpu as pltpu

F32 = jnp.float32
BF16 = jnp.bfloat16

CHUNK = 64
SB_HEADS = 8
SB_HEAD_DIM = 128
DA_HEADS = 8
DA_HEAD_DIM = 64
DA_V_DIM = 2 * DA_HEAD_DIM
N_GROUPS = 4
EXPERTS_PER_GROUP = 8
N_EXPERTS = N_GROUPS * EXPERTS_PER_GROUP
ROPE_THETA = 10000.0
EPS = 1e-6

LANES = 128
SUBLANES = 8
ROUTER_ROWS = 40
VMEM_LIMIT_BYTES = 56 * 1024 * 1024

SB_LOG_WEIGHT_FLOOR = -104.0
SB_MASKED_LOGIT = -1e30
LOG2_E = 1.4426950408889634

TM_IN = 1024
TN_IN = 1024
TQ_SB = 256
TK_SB = 128
SB_TILES_PER_STEP = 4
TQ_DA = 256
DA_TILES_PER_STEP = 4
TM_GATE = 1024
TN_GATE = 256
TM_OUT = 512
T_ROUTE = 512
TM_DISPATCH = 1024
TM_EXPERT = 256
XB_SLOTS = 3
TM_COMBINE = 256


def _params(sem, vmem=VMEM_LIMIT_BYTES):
    return pltpu.CompilerParams(dimension_semantics=sem, vmem_limit_bytes=vmem)


def _nt_dot(a, b, **kw):
    return lax.dot_general(a, b, (((1,), (1,)), ((), ())), preferred_element_type=F32, **kw)


def _in_proj_kernel(x_ref, g_ref, w_ref, cos_ref, sin_ref, o_ref, h_ref, *, rope_lo, rope_hi):
    j = pl.program_id(1)

    @pl.when(j == 0)
    def _():
        x = x_ref[...]
        ms = jnp.mean(x * x, axis=-1, keepdims=True)
        h_ref[...] = (x * lax.rsqrt(ms + EPS) * g_ref[...]).astype(BF16)

    is_rope = jnp.logical_and(j >= rope_lo, j < rope_hi)

    @pl.when(is_rope)
    def _():
        y = jnp.dot(h_ref[...], w_ref[...], preferred_element_type=F32)
        cos = cos_ref[...]
        sin = sin_ref[...]
        lane = lax.broadcasted_iota(jnp.int32, cos.shape, 1)
        first_half = (lane % DA_HEAD_DIM) < (DA_HEAD_DIM // 2)
        for hh in range(y.shape[1] // LANES):
            yh = y[:, hh * LANES:(hh + 1) * LANES]
            partner = jnp.where(first_half,
                                pltpu.roll(yh, LANES - DA_HEAD_DIM // 2, 1),
                                pltpu.roll(yh, DA_HEAD_DIM // 2, 1))
            o_ref[:, hh * LANES:(hh + 1) * LANES] = (yh * cos + partner * sin).astype(BF16)

    @pl.when(jnp.logical_not(is_rope))
    def _():
        o_ref[...] = jnp.dot(h_ref[...], w_ref[...], preferred_element_type=F32).astype(BF16)


def _in_proj(x2d, g, w_bf16, cols, cos_tab, sin_tab, seq, rope_lo, rope_hi):
    n, d = x2d.shape
    tm, tn = min(TM_IN, seq), TN_IN
    pos_tiles = seq // tm
    return pl.pallas_call(
        functools.partial(_in_proj_kernel, rope_lo=rope_lo, rope_hi=rope_hi),
        out_shape=jax.ShapeDtypeStruct((n, cols), BF16),
        grid=(n // tm, cols // tn),
        in_specs=[
            pl.BlockSpec((tm, d), lambda i, j: (i, 0)),
            pl.BlockSpec((1, d), lambda i, j: (0, 0)),
            pl.BlockSpec((d, tn), lambda i, j: (0, j)),
            pl.BlockSpec((tm, LANES), lambda i, j: (i % pos_tiles, 0)),
            pl.BlockSpec((tm, LANES), lambda i, j: (i % pos_tiles, 0)),
        ],
        out_specs=pl.BlockSpec((tm, tn), lambda i, j: (i, j)),
        scratch_shapes=[pltpu.VMEM((tm, d), BF16)],
        compiler_params=_params(("parallel", "arbitrary")),
        name="in_proj",
    )(x2d, g, w_bf16, cos_tab, sin_tab)


def _sb_kernel(q_ref, k_ref, v_ref, o_ref, *, tq, tk, tiles_per_step, scale):
    i = pl.program_id(2)
    row = lax.broadcasted_iota(jnp.int32, (2 * tk, 2 * tk), 0) % tk
    col = lax.broadcasted_iota(jnp.int32, (2 * tk, 2 * tk), 1)
    u2 = jnp.where(jnp.logical_or(col >= tk, row > col), 1.0, 0.0).astype(BF16)

    def span(q, start, width, diag_cols, running):
        k = k_ref[pl.ds(start, width), :]
        v = v_ref[pl.ds(start, width), :]
        z = _nt_dot(q, k) * scale
        if diag_cols is not None:
            r = lax.broadcasted_iota(jnp.int32, z.shape, 0)
            c = lax.broadcasted_iota(jnp.int32, z.shape, 1)
            z = jnp.where((c - diag_cols) < r, z, SB_MASKED_LOGIT)
        sp = jnp.maximum(z, 0.0) + jnp.log(1.0 + jnp.exp2(jnp.abs(z) * (-LOG2_E)))
        hi = sp.astype(BF16)
        lo = (sp - hi.astype(F32)).astype(BF16)
        nb = width // tk
        suffix = [None] * nb
        for blk in range(nb - 1, -1, -1):
            sl = slice(blk * tk, (blk + 1) * tk)
            cs = jnp.dot(jnp.concatenate([hi[:, sl], lo[:, sl]], axis=1), u2, preferred_element_type=F32)
            suffix[blk] = cs[:, :tk] + running
            running = running + cs[:, tk:]
        a = jnp.exp((z - sp) - jnp.concatenate(suffix, axis=1))
        return jnp.dot(a.astype(BF16), v, preferred_element_type=F32), running

    zero_run = jnp.zeros((tq, tk), F32)

    def far_keys(q, first, running, acc):
        def alive_after(running):
            return jnp.min(running) < -SB_LOG_WEIGHT_FLOOR

        def cond(c):
            j, alive, _, _ = c
            return jnp.logical_and(j >= 0, alive)

        def body(c):
            j, _, running, acc = c
            out, running = span(q, pl.multiple_of(j * tq, tq), tq, None, running)
            return j - 1, alive_after(running), running, acc + out

        return lax.while_loop(cond, body, (first, alive_after(running), running, acc))[3]

    def tiles(first_tile, head_has_no_past):
        near = []
        for sub in range(tiles_per_step):
            q = q_ref[sub * tq:(sub + 1) * tq, :]
            if sub == 0 and head_has_no_past:
                near.append(span(q, 0, tq, 0, zero_run))
            else:
                start = (first_tile + sub - 1) * tq
                if not isinstance(start, int):
                    start = pl.multiple_of(start, tq)
                near.append(span(q, start, 2 * tq, tq, zero_run))
        for sub in range(tiles_per_step):
            acc, running = near[sub]
            if not (head_has_no_past and sub < 2):
                acc = far_keys(q_ref[sub * tq:(sub + 1) * tq, :], first_tile + sub - 2, running, acc)
            o_ref[sub * tq:(sub + 1) * tq, :] = acc.astype(o_ref.dtype)

    @pl.when(i == 0)
    def _():
        tiles(0, True)

    @pl.when(i > 0)
    def _():
        tiles(i * tiles_per_step, False)


def _sb_attention(proj3, batch, seq, col_q, col_k, col_v):
    d = SB_HEAD_DIM
    tq, tk = min(TQ_SB, seq), min(TK_SB, seq)
    tps = min(SB_TILES_PER_STEP, seq // tq)
    rows = tq * tps
    return pl.pallas_call(
        functools.partial(_sb_kernel, tq=tq, tk=tk, tiles_per_step=tps, scale=1.0 / math.sqrt(d)),
        out_shape=jax.ShapeDtypeStruct((batch, seq, SB_HEADS * d), BF16),
        grid=(batch, SB_HEADS, seq // rows),
        in_specs=[
            pl.BlockSpec((None, rows, d), lambda b, h, i: (b, i, col_q + h)),
            pl.BlockSpec((None, seq, d), lambda b, h, i: (b, 0, col_k + h)),
            pl.BlockSpec((None, seq, d), lambda b, h, i: (b, 0, col_v + h)),
        ],
        out_specs=pl.BlockSpec((None, rows, d), lambda b, h, i: (b, i, h)),
        compiler_params=_params(("parallel", "parallel", "arbitrary")),
        name="sb_attn",
    )(proj3, proj3, proj3)


def _da_kernel(lam_ref, q_ref, k_ref, v_ref, g_ref, o_ref, vext_ref, *, tq, tiles_per_step, n_steps,
               lam_init):
    i = pl.program_id(2)

    @pl.when(i == 0)
    def _():
        vext_ref[:, :DA_V_DIM] = v_ref[...]
        vext_ref[:, DA_V_DIM:] = jnp.ones((vext_ref.shape[0], DA_V_DIM), BF16)

    lv = lam_ref[...]
    lam = (jnp.exp(jnp.sum(lv[0:1] * lv[1:2], axis=-1, keepdims=True))
           - jnp.exp(jnp.sum(lv[2:3] * lv[3:4], axis=-1, keepdims=True)) + lam_init)

    q = q_ref[...]
    lane = lax.broadcasted_iota(jnp.int32, q.shape, 1)
    qs = q * (1.0 / math.sqrt(DA_HEAD_DIM))
    zero = jnp.zeros_like(qs)
    q_maps = (jnp.where(lane < DA_HEAD_DIM, qs, zero), jnp.where(lane >= DA_HEAD_DIM, qs, zero))
    r = lax.broadcasted_iota(jnp.int32, (tq, tq), 0)
    c = lax.broadcasted_iota(jnp.int32, (tq, tq), 1)
    diag_mask = (c // CHUNK) <= (r // CHUNK)

    def softmax_av(qm, lo):
        sd = jnp.where(diag_mask, _nt_dot(qm, k_ref[lo:lo + tq, :]), -jnp.inf)
        m = jnp.max(sd, axis=-1, keepdims=True)
        if lo > 0:
            sl = _nt_dot(qm, k_ref[0:lo, :])
            m = jnp.maximum(m, jnp.max(sl, axis=-1, keepdims=True))
        acc = jnp.dot(jnp.exp(sd - m).astype(BF16), vext_ref[lo:lo + tq, :], preferred_element_type=F32)
        if lo > 0:
            acc = acc + jnp.dot(jnp.exp(sl - m).astype(BF16), vext_ref[0:lo, :],
                                preferred_element_type=F32)
        return acc[:, :DA_V_DIM] / acc[:, DA_V_DIM:]

    for step in range(n_steps):
        @pl.when(i == step)
        def _(step=step):
            for sub in range(tiles_per_step):
                rows = slice(sub * tq, (sub + 1) * tq)
                lo = (step * tiles_per_step + sub) * tq
                o = (softmax_av(q_maps[0][rows], lo) - lam * softmax_av(q_maps[1][rows], lo))
                ms = jnp.mean(o * o, axis=-1, keepdims=True)
                o_ref[rows, :] = (o * lax.rsqrt(ms + EPS) * g_ref[...] * (1.0 - lam_init)).astype(o_ref.dtype)


def _da_attention(lam_vecs, proj3, sub_g, batch, seq, col_q, col_k, col_v, lam_init):
    d = DA_V_DIM
    tq = min(TQ_DA, seq)
    tps = min(DA_TILES_PER_STEP, seq // tq)
    rows = tq * tps
    return pl.pallas_call(
        functools.partial(_da_kernel, tq=tq, tiles_per_step=tps, n_steps=seq // rows, lam_init=lam_init),
        out_shape=jax.ShapeDtypeStruct((batch, seq, DA_HEADS * d), BF16),
        grid=(batch, DA_HEADS, seq // rows),
        in_specs=[
            pl.BlockSpec(lam_vecs.shape, lambda b, h, i: (0, 0)),
            pl.BlockSpec((None, rows, d), lambda b, h, i: (b, i, col_q + h)),
            pl.BlockSpec((None, seq, d), lambda b, h, i: (b, 0, col_k + h)),
            pl.BlockSpec((None, seq, d), lambda b, h, i: (b, 0, col_v + h)),
            pl.BlockSpec((1, d), lambda b, h, i: (0, 0)),
        ],
        out_specs=pl.BlockSpec((None, rows, d), lambda b, h, i: (b, i, h)),
        scratch_shapes=[pltpu.VMEM((seq, 2 * d), BF16)],
        compiler_params=_params(("parallel", "parallel", "arbitrary")),
        name="da_attn",
    )(lam_vecs, proj3, proj3, proj3, sub_g)


def _gate_branch_kernel(x_ref, gmix_ref, osb_ref, oda_ref, wgs_ref, wgd_ref, wbs_ref, wbd_ref, m_ref, h_scr):
    @pl.when(pl.program_id(1) == 0)
    def _():
        x = x_ref[...]
        ms = jnp.mean(x * x, axis=-1, keepdims=True)
        h_scr[...] = (x * lax.rsqrt(ms + EPS) * gmix_ref[...]).astype(BF16)

    h = h_scr[...]
    gate_sb = jax.nn.sigmoid(jnp.dot(h, wgs_ref[...], preferred_element_type=F32))
    gate_da = jax.nn.sigmoid(jnp.dot(h, wgd_ref[...], preferred_element_type=F32))
    br_sb = jnp.dot(osb_ref[...], wbs_ref[...], preferred_element_type=F32)
    br_da = jnp.dot(oda_ref[...], wbd_ref[...], preferred_element_type=F32)
    m_ref[...] = (gate_sb * br_sb + gate_da * br_da).astype(m_ref.dtype)


def _gate_branch(x2d, gmix, o_sb, o_da, w_in_bf16, gate_col0, wbs, wbd):
    n, d = x2d.shape
    tm, tn = min(TM_GATE, n), TN_GATE
    wsb, wda = o_sb.shape[1], o_da.shape[1]
    sb0, da0 = gate_col0 // tn, (gate_col0 + d) // tn
    return pl.pallas_call(
        _gate_branch_kernel,
        out_shape=jax.ShapeDtypeStruct((n, d), BF16),
        grid=(n // tm, d // tn),
        in_specs=[
            pl.BlockSpec((tm, d), lambda i, j: (i, 0)),
            pl.BlockSpec((1, d), lambda i, j: (0, 0)),
            pl.BlockSpec((tm, wsb), lambda i, j: (i, 0)),
            pl.BlockSpec((tm, wda), lambda i, j: (i, 0)),
            pl.BlockSpec((d, tn), lambda i, j: (0, sb0 + j)),
            pl.BlockSpec((d, tn), lambda i, j: (0, da0 + j)),
            pl.BlockSpec((wsb, tn), lambda i, j: (0, j)),
            pl.BlockSpec((wda, tn), lambda i, j: (0, j)),
        ],
        out_specs=pl.BlockSpec((tm, tn), lambda i, j: (i, j)),
        scratch_shapes=[pltpu.VMEM((tm, d), BF16)],
        compiler_params=_params(("parallel", "arbitrary")),
        name="gate_branch",
    )(x2d, gmix, o_sb, o_da, w_in_bf16, w_in_bf16, wbs, wbd)


def _out_proj_kernel(m_ref, x_ref, wo_ref, gffn_ref, wr_hi_ref, wr_lo_ref, x1_ref, h2_ref, lt_ref):
    x1 = x_ref[...] + jnp.dot(m_ref[...], wo_ref[...], preferred_element_type=F32)
    x1_ref[...] = x1
    ms = jnp.mean(x1 * x1, axis=-1, keepdims=True)
    h2 = x1 * lax.rsqrt(ms + EPS) * gffn_ref[...]
    h2_ref[...] = h2
    h_hi = h2.astype(BF16)
    h_lo = (h2 - h_hi.astype(F32)).astype(BF16)
    lt_ref[...] = (_nt_dot(wr_hi_ref[...], h_hi) + _nt_dot(wr_lo_ref[...], h_hi)
                   + _nt_dot(wr_hi_ref[...], h_lo))


def _out_proj(merged, x2d, wo, gffn, wr_t):
    n, d = x2d.shape
    tm = min(TM_OUT, n)
    once = pl.Buffered(1)
    wr_hi = wr_t.astype(BF16)
    wr_lo = (wr_t - wr_hi.astype(F32)).astype(BF16)
    return pl.pallas_call(
        _out_proj_kernel,
        out_shape=(jax.ShapeDtypeStruct((n, d), F32),
                   jax.ShapeDtypeStruct((n, d), F32),
                   jax.ShapeDtypeStruct((ROUTER_ROWS, n), F32)),
        grid=(n // tm,),
        in_specs=[
            pl.BlockSpec((tm, d), lambda i: (i, 0)),
            pl.BlockSpec((tm, d), lambda i: (i, 0)),
            pl.BlockSpec((d, d), lambda i: (0, 0), pipeline_mode=once),
            pl.BlockSpec((1, d), lambda i: (0, 0), pipeline_mode=once),
            pl.BlockSpec((ROUTER_ROWS, d), lambda i: (0, 0), pipeline_mode=once),
            pl.BlockSpec((ROUTER_ROWS, d), lambda i: (0, 0), pipeline_mode=once),
        ],
        out_specs=(pl.BlockSpec((tm, d), lambda i: (i, 0)),
                   pl.BlockSpec((tm, d), lambda i: (i, 0)),
                   pl.BlockSpec((ROUTER_ROWS, tm), lambda i: (0, i))),
        compiler_params=_params(("parallel",)),
        name="out_proj",
    )(merged, x2d, wo, gffn, wr_hi, wr_lo)


def _route_kernel(lt_ref, ids_ref, rank_ref, gate_ref, cnt_ref, carry_ref):
    step = pl.program_id(0)

    @pl.when(step == 0)
    def _():
        carry_ref[...] = jnp.zeros_like(carry_ref)

    t = lt_ref.shape[1]
    el = lt_ref[0:N_EXPERTS, :]
    gl = lt_ref[N_EXPERTS:N_EXPERTS + 8, :]
    grow = lax.broadcasted_iota(jnp.int32, gl.shape, 0)
    gl = jnp.where(grow < N_GROUPS, gl, -jnp.inf)
    gmax = jnp.max(gl, axis=0, keepdims=True)
    grp = jnp.min(jnp.where(gl == gmax, grow, N_GROUPS), axis=0, keepdims=True)
    p_grp = 1.0 / jnp.sum(jnp.exp(gl - gmax), axis=0, keepdims=True)

    eidx = lax.broadcasted_iota(jnp.int32, el.shape, 0)
    in_grp = (eidx // EXPERTS_PER_GROUP) == grp
    e1 = jnp.where(in_grp, el, -jnp.inf)
    top1 = jnp.max(e1, axis=0, keepdims=True)
    idx1 = jnp.min(jnp.where(e1 == top1, eidx, N_EXPERTS), axis=0, keepdims=True)
    e2 = jnp.where(eidx == idx1, -jnp.inf, e1)
    top2 = jnp.max(e2, axis=0, keepdims=True)
    idx2 = jnp.min(jnp.where(e2 == top2, eidx, N_EXPERTS), axis=0, keepdims=True)
    dlt = jnp.exp(top2 - top1)
    den = 1.0 + dlt
    gate_ref[0:1, :] = p_grp / den
    gate_ref[1:2, :] = p_grp * dlt / den
    ids_ref[0:1, :] = idx1
    ids_ref[1:2, :] = idx2

    oh1 = jnp.where(eidx == idx1, 1.0, 0.0)
    oh2 = jnp.where(eidx == idx2, 1.0, 0.0)
    rr = lax.broadcasted_iota(jnp.int32, (t, t), 0)
    cc = lax.broadcasted_iota(jnp.int32, (t, t), 1)
    before = jnp.where(rr < cc, 1.0, 0.0).astype(BF16)
    pre1 = jnp.dot(oh1.astype(BF16), before, preferred_element_type=F32)
    pre2 = jnp.dot(oh2.astype(BF16), before, preferred_element_type=F32)
    cnt1 = jnp.sum(oh1, axis=1, keepdims=True)
    cnt2 = jnp.sum(oh2, axis=1, keepdims=True)
    carry = carry_ref[:, 0:1]
    rank1 = jnp.sum(oh1 * (carry + pre1), axis=0, keepdims=True)
    rank2 = jnp.sum(oh2 * (carry + cnt1 + pre2), axis=0, keepdims=True)
    rank_ref[0:1, :] = rank1.astype(jnp.int32)
    rank_ref[1:2, :] = rank2.astype(jnp.int32)
    new_carry = carry_ref[...] + cnt1 + cnt2
    carry_ref[...] = new_carry
    cnt_ref[...] = new_carry


def _route(logits_t):
    n = logits_t.shape[1]
    t = min(T_ROUTE, n)
    return pl.pallas_call(
        _route_kernel,
        out_shape=(jax.ShapeDtypeStruct((2, n), jnp.int32),
                   jax.ShapeDtypeStruct((2, n), jnp.int32),
                   jax.ShapeDtypeStruct((2, n), F32),
                   jax.ShapeDtypeStruct((N_EXPERTS, LANES), F32)),
        grid=(n // t,),
        in_specs=[pl.BlockSpec((ROUTER_ROWS, t), lambda i: (0, i))],
        out_specs=(pl.BlockSpec((2, t), lambda i: (0, i)),
                   pl.BlockSpec((2, t), lambda i: (0, i)),
                   pl.BlockSpec((2, t), lambda i: (0, i)),
                   pl.BlockSpec((N_EXPERTS, LANES), lambda i: (0, 0))),
        scratch_shapes=[pltpu.VMEM((N_EXPERTS, LANES), F32)],
        compiler_params=_params(("arbitrary",)),
        name="route",
    )(logits_t)


def _dispatch_kernel(zs_ref, nu_ref, dest_ref, h2_ref, xb_hbm, zeros_ref, sem, tail_sem, *, tm, te, n_tiles):
    step = pl.program_id(0)

    def tail_copy(t):
        return pltpu.make_async_copy(zeros_ref, xb_hbm.at[pl.ds(pl.multiple_of(t * te, te), te)], tail_sem)

    @pl.when(step == 0)
    def _():
        zeros_ref[...] = jnp.zeros_like(zeros_ref)

        def zero_copy(e):
            start = pl.multiple_of(zs_ref[e], SUBLANES)
            return pltpu.make_async_copy(zeros_ref, xb_hbm.at[pl.ds(start, te)], sem)

        for e in range(N_EXPERTS):
            zero_copy(e).start()
        for e in range(N_EXPERTS):
            zero_copy(e).wait()
        lax.fori_loop(nu_ref[0], n_tiles, lambda t, c: (tail_copy(t).start(), c)[1], 0)

    def issue(blk, _):
        r0 = pl.multiple_of(blk * SUBLANES, SUBLANES)
        for u in range(SUBLANES):
            src = h2_ref.at[pl.ds(r0 + u, 1)]
            for k in range(2):
                pltpu.make_async_copy(src, xb_hbm.at[pl.ds(dest_ref[0, 0, k * tm + r0 + u], 1)],
                                      sem).start(priority=k)
        return 0

    lax.fori_loop(0, tm // SUBLANES, issue, 0)
    for _ in range(2):
        pltpu.make_async_copy(h2_ref, xb_hbm.at[pl.ds(0, tm)], sem).wait()

    @pl.when(step == pl.num_programs(0) - 1)
    def _():
        lax.fori_loop(nu_ref[0], n_tiles, lambda t, c: (tail_copy(t).wait(), c)[1], 0)


def _dispatch(zero_starts, n_used, dest_tiles, h2, n_tiles, tm, te):
    n, d = h2.shape
    return pl.pallas_call(
        functools.partial(_dispatch_kernel, tm=tm, te=te, n_tiles=n_tiles),
        out_shape=jax.ShapeDtypeStruct((n_tiles * te, d), h2.dtype),
        grid_spec=pltpu.PrefetchScalarGridSpec(
            num_scalar_prefetch=2,
            grid=(n // tm,),
            in_specs=[
                pl.BlockSpec((1, 1, 2 * tm), lambda i, zs, nu: (i, 0, 0), memory_space=pltpu.SMEM),
                pl.BlockSpec((tm, d), lambda i, zs, nu: (i, 0)),
            ],
            out_specs=pl.BlockSpec(memory_space=pl.ANY),
            scratch_shapes=[pltpu.VMEM((te, d), h2.dtype), pltpu.SemaphoreType.DMA,
                            pltpu.SemaphoreType.DMA],
        ),
        compiler_params=_params(("arbitrary",)),
        name="dispatch",
    )(zero_starts, n_used, dest_tiles, h2)


def _expert_kernel(te_ref, nu_ref, slot_ref, next_ref, xb_hbm, wg_hbm, wu_hbm, wd_hbm, yb_ref,
                   wg_f, wu_f, wd_f, wg_s, wu_s, wd_s, sems, xb_buf, xb_sems, *, te):
    t = pl.program_id(0)
    used = t < nu_ref[0]
    new_run = jnp.logical_or(t == 0, te_ref[t] != te_ref[jnp.maximum(t - 1, 0)])

    def weight_copies(expert, slot):
        return [pltpu.make_async_copy(src.at[expert], dst.at[slot], sems.at[slot, i])
                for i, (src, dst) in enumerate(((wg_hbm, wg_f), (wu_hbm, wu_f), (wd_hbm, wd_f)))]

    @pl.when(t == 0)
    def _():
        for c in weight_copies(te_ref[0], 0):
            c.start()

    @pl.when(jnp.logical_and(used, new_run))
    def _():
        slot = slot_ref[t]
        for c in weight_copies(te_ref[t], slot):
            c.wait()

        @pl.when(next_ref[t] >= 0)
        def _():
            for c in weight_copies(next_ref[t], 1 - slot):
                c.start()

        wg_s[...] = wg_f[slot].astype(BF16)
        wu_s[...] = wu_f[slot].astype(BF16)
        wd_s[...] = wd_f[slot].astype(BF16)

    def row_copy(tile):
        rows = pl.ds(pl.multiple_of(tile * te, te), te)
        slot = tile % XB_SLOTS
        return pltpu.make_async_copy(xb_hbm.at[rows], xb_buf.at[slot], xb_sems.at[slot])

    @pl.when(t == 0)
    def _():
        for ahead in range(XB_SLOTS - 1):
            @pl.when(ahead < nu_ref[0])
            def _(ahead=ahead):
                row_copy(ahead).start()

    @pl.when(used)
    def _():
        @pl.when(t + (XB_SLOTS - 1) < nu_ref[0])
        def _():
            row_copy(t + (XB_SLOTS - 1)).start()

        row_copy(t).wait()
        x = xb_buf[t % XB_SLOTS].astype(BF16)
        g = jnp.dot(x, wg_s[...], preferred_element_type=F32)
        u = jnp.dot(x, wu_s[...], preferred_element_type=F32)
        hid = (g * jax.nn.sigmoid(g)) * u
        yb_ref[...] = jnp.dot(hid.astype(BF16), wd_s[...], preferred_element_type=F32)

    @pl.when(jnp.logical_not(used))
    def _():
        yb_ref[...] = jnp.zeros_like(yb_ref)


def _experts(tile_expert, n_used, run_slot, next_expert, xb, wg, wu, wd, n_tiles, te):
    d = xb.shape[1]
    de = wg.shape[2]

    hbm = pl.BlockSpec(memory_space=pl.ANY)
    return pl.pallas_call(
        functools.partial(_expert_kernel, te=te),
        out_shape=jax.ShapeDtypeStruct((n_tiles * te, d), F32),
        grid_spec=pltpu.PrefetchScalarGridSpec(
            num_scalar_prefetch=4,
            grid=(n_tiles,),
            in_specs=[hbm, hbm, hbm, hbm],
            out_specs=pl.BlockSpec((te, d), lambda t, *_: (t, 0)),
            scratch_shapes=[pltpu.VMEM((2, d, de), F32), pltpu.VMEM((2, d, de), F32), pltpu.VMEM((2, de, d), F32),
                            pltpu.VMEM((d, de), BF16), pltpu.VMEM((d, de), BF16), pltpu.VMEM((de, d), BF16),
                            pltpu.SemaphoreType.DMA((2, 3)),
                            pltpu.VMEM((XB_SLOTS, te, d), xb.dtype), pltpu.SemaphoreType.DMA((XB_SLOTS,))],
        ),
        compiler_params=_params(("arbitrary",)),
        name="experts",
    )(tile_expert, n_used, run_slot, next_expert, xb, wg, wu, wd)


def _combine_kernel(dest_ref, next_dest_ref, x1_ref, gate_ref, g_ref, yb_hbm, o_ref, buf, sems, *, tm,
                    final_norm):
    i = pl.program_id(0)
    slot = i % 2

    def start_rows(idx_ref, slot):
        def issue(blk, _):
            r0 = pl.multiple_of(blk * SUBLANES, SUBLANES)
            for u in range(SUBLANES):
                for k in range(2):
                    row = k * tm + r0 + u
                    pltpu.make_async_copy(yb_hbm.at[pl.ds(idx_ref[0, 0, row], 1)],
                                          buf.at[slot, pl.ds(row, 1)], sems.at[slot]).start(priority=k)
            return 0
        lax.fori_loop(0, tm // SUBLANES, issue, 0)

    @pl.when(i == 0)
    def _():
        start_rows(dest_ref, 0)

    @pl.when(i + 1 < pl.num_programs(0))
    def _():
        start_rows(next_dest_ref, 1 - slot)

    pltpu.make_async_copy(yb_hbm.at[pl.ds(0, 2 * tm)], buf.at[slot], sems.at[slot]).wait()
    gates = gate_ref[...]
    g0, g1 = gates[:, 0:1], gates[:, 1:2]
    x = x1_ref[...] + (buf[slot, 0:tm, :] * g0 + buf[slot, tm:2 * tm, :] * g1)
    if final_norm:
        ms = jnp.mean(x * x, axis=-1, keepdims=True)
        x = x * lax.rsqrt(ms + EPS) * g_ref[...]
    o_ref[...] = x


def _combine(dest_tiles, x1, gates, g_final, yb, tm, final_norm):
    n, d = x1.shape
    n_steps = n // tm
    return pl.pallas_call(
        functools.partial(_combine_kernel, tm=tm, final_norm=final_norm),
        out_shape=jax.ShapeDtypeStruct((n, d), F32),
        grid=(n_steps,),
        in_specs=[
            pl.BlockSpec((1, 1, 2 * tm), lambda i: (i, 0, 0), memory_space=pltpu.SMEM),
            pl.BlockSpec((1, 1, 2 * tm), lambda i: (jnp.minimum(i + 1, n_steps - 1), 0, 0),
                         memory_space=pltpu.SMEM),
            pl.BlockSpec((tm, d), lambda i: (i, 0)),
            pl.BlockSpec((tm, 2), lambda i: (i, 0)),
            pl.BlockSpec((1, d), lambda i: (0, 0)),
            pl.BlockSpec(memory_space=pl.ANY),
        ],
        out_specs=pl.BlockSpec((tm, d), lambda i: (i, 0)),
        scratch_shapes=[pltpu.VMEM((2, 2 * tm, d), F32), pltpu.SemaphoreType.DMA((2,))],
        compiler_params=_params(("arbitrary",)),
        name="combine",
    )(dest_tiles, dest_tiles, x1, gates, g_final, yb)


def _rope_tables(seq):
    half = DA_HEAD_DIM // 2
    inv_freq = ROPE_THETA ** (-(jnp.arange(half, dtype=F32) * 2.0 / DA_HEAD_DIM))
    ang = jnp.arange(seq, dtype=F32)[:, None] * inv_freq[None, :]
    cos, sin = jnp.cos(ang), jnp.sin(ang)
    cos_tab = jnp.tile(cos, (1, LANES // half))
    sin_tab = jnp.tile(jnp.concatenate([-sin, sin], axis=1), (1, LANES // DA_HEAD_DIM))
    return cos_tab, sin_tab


def _tile_indices(dest, tm):
    n = dest.shape[1]
    return dest.reshape(2, n // tm, tm).transpose(1, 0, 2).reshape(n // tm, 1, 2 * tm)


def _layer(x2d, batch, seq, layer, mix_g, w_in, lam_vecs, sub_g, w_bsb, w_bda, w_out, ffn_g,
           w_grp, w_exp, w_gate, w_up, w_down):
    n, d = x2d.shape
    sbw = SB_HEADS * SB_HEAD_DIM
    daw = DA_HEADS * DA_V_DIM
    qkv_cols = 3 * sbw + 3 * daw
    lam_init = 0.8 - 0.6 * math.exp(-0.3 * layer)

    w_in_bf16 = w_in.astype(BF16)
    cos_tab, sin_tab = _rope_tables(seq)
    rope_lo = 3 * sbw // TN_IN
    rope_hi = (3 * sbw + 2 * daw) // TN_IN
    proj = _in_proj(x2d, mix_g[None, :], w_in_bf16, qkv_cols, cos_tab, sin_tab, seq, rope_lo, rope_hi)
    proj3 = proj.reshape(batch, seq, qkv_cols)
    nsb = sbw // SB_HEAD_DIM
    nda = daw // DA_V_DIM
    o_sb = _sb_attention(proj3, batch, seq, 0, nsb, 2 * nsb)
    o_da = _da_attention(lam_vecs, proj3, sub_g[None, :], batch, seq,
                         3 * nsb, 3 * nsb + nda, 3 * nsb + 2 * nda, lam_init)

    wr_t = jnp.zeros((ROUTER_ROWS, d), F32)
    wr_t = wr_t.at[:N_EXPERTS].set(w_exp.T).at[N_EXPERTS:N_EXPERTS + N_GROUPS].set(w_grp.T)
    merged = _gate_branch(x2d, mix_g[None, :], o_sb.reshape(n, sbw), o_da.reshape(n, daw),
                          w_in_bf16, qkv_cols, w_bsb.astype(BF16), w_bda.astype(BF16))
    x1, h2, logits_t = _out_proj(merged, x2d, w_out.astype(BF16), ffn_g[None, :], wr_t)

    ids, ranks, gates, counts = _route(logits_t)
    te = TM_EXPERT
    counts = counts[:, 0].astype(jnp.int32)
    padded = (counts + te - 1) // te * te
    pad_ends = jnp.cumsum(padded)
    pad_starts = pad_ends - padded
    expert_ids = jnp.arange(N_EXPERTS, dtype=jnp.int32)
    dest = ranks + jnp.sum(jnp.where(ids[..., None] == expert_ids, pad_starts, 0), axis=-1)
    n_tiles = (2 * n) // te + N_EXPERTS
    n_used = (pad_ends[-1] // te).astype(jnp.int32)[None]
    tile_row0 = jnp.arange(n_tiles, dtype=jnp.int32) * te
    tile_expert = jnp.minimum(jnp.sum((pad_ends[None, :] <= tile_row0[:, None]).astype(jnp.int32), axis=1),
                              N_EXPERTS - 1)
    zero_starts = ((pad_starts + counts) // SUBLANES * SUBLANES).astype(jnp.int32)

    tmd = min(TM_DISPATCH, n)
    xb = _dispatch(zero_starts, n_used, _tile_indices(dest, tmd), h2, n_tiles + 1, tmd, te)
    tile_ids = jnp.arange(n_tiles, dtype=jnp.int32)
    new_run = jnp.logical_or(tile_ids == 0, tile_expert != jnp.roll(tile_expert, 1))
    run_slot = ((jnp.cumsum(new_run.astype(jnp.int32)) - 1) % 2).astype(jnp.int32)
    later = jnp.logical_and(tile_ids[None, :] < n_used, tile_expert[None, :] > tile_expert[:, None])
    next_expert = jnp.min(jnp.where(later, tile_expert[None, :], N_EXPERTS), axis=1)
    next_expert = jnp.where(next_expert == N_EXPERTS, -1, next_expert).astype(jnp.int32)
    yb = _experts(tile_expert, n_used, run_slot, next_expert, xb, w_gate, w_up, w_down, n_tiles, te)
    return x1, gates, dest, yb


def kernel(x, mix_norm_g, w_in, lambda_q1, lambda_k1, lambda_q2, lambda_k2, diff_subnorm_g, w_branch_sb, w_branch_diff, w_out, ffn_norm_g, w_group_router, w_expert_router, w_gate, w_up, w_down, final_norm_g):
    batch, seq, d = x.shape
    depth = w_in.shape[0]
    n = batch * seq
    x2d = x.reshape(n, d)
    tmc = min(TM_COMBINE, n)
    for l in range(depth):
        lam_vecs = jnp.stack([lambda_q1[l], lambda_k1[l], lambda_q2[l], lambda_k2[l]]).astype(F32)
        x1, gates, dest, yb = _layer(
            x2d, batch, seq, l, mix_norm_g[l], w_in[l], lam_vecs, diff_subnorm_g[l], w_branch_sb[l],
            w_branch_diff[l], w_out[l], ffn_norm_g[l], w_group_router[l], w_expert_router[l],
            w_gate[l], w_up[l], w_down[l])
        x2d = _combine(_tile_indices(dest, tmc), x1, gates.T, final_norm_g[None, :], yb, tmc,
                       final_norm=(l == depth - 1))
    return x2d.reshape(batch, seq, d)
```

```python
import functools
import math

import jax
import jax.numpy as jnp
from jax import lax
from jax.experimental import pallas as pl
from jax.experimental.pallas import tpu as pltpu

F32 = jnp.float32
BF16 = jnp.bfloat16

CHUNK = 64
SB_HEADS = 8
SB_HEAD_DIM = 128
DA_HEADS = 8
DA_HEAD_DIM = 64
DA_V_DIM = 2 * DA_HEAD_DIM
N_GROUPS = 4
EXPERTS_PER_GROUP = 8
N_EXPERTS = N_GROUPS * EXPERTS_PER_GROUP
ROPE_THETA = 10000.0
EPS = 1e-6

LANES = 128
SUBLANES = 8
ROUTER_ROWS = 40
VMEM_LIMIT_BYTES = 56 * 1024 * 1024

SB_LOG_WEIGHT_FLOOR = -104.0
SB_MASKED_LOGIT = -1e30
LOG2_E = 1.4426950408889634

TM_IN = 1024
TN_IN = 1024
TQ_SB = 256
TK_SB = 128
SB_TILES_PER_STEP = 4
TQ_DA = 256
DA_TILES_PER_STEP = 4
TM_GATE = 1024
TN_GATE = 256
TM_OUT = 512
T_ROUTE = 512
TM_DISPATCH = 1024
TM_EXPERT = 256
XB_SLOTS = 3
TM_COMBINE = 256


def _params(sem, vmem=VMEM_LIMIT_BYTES):
    return pltpu.CompilerParams(dimension_semantics=sem, vmem_limit_bytes=vmem)


def _nt_dot(a, b, **kw):
    return lax.dot_general(a, b, (((1,), (1,)), ((), ())), preferred_element_type=F32, **kw)


def _in_proj_kernel(x_ref, g_ref, w_ref, cos_ref, sin_ref, o_ref, h_ref, *, rope_lo, rope_hi):
    j = pl.program_id(1)

    @pl.when(j == 0)
    def _():
        x = x_ref[...]
        ms = jnp.mean(x * x, axis=-1, keepdims=True)
        h = (x * lax.rsqrt(ms + EPS) * g_ref[...]).astype(BF16)
        h_ref[...] = h
        o_ref[...] = jnp.dot(h, w_ref[...], preferred_element_type=F32).astype(BF16)

    is_rope = jnp.logical_and(j >= rope_lo, j < rope_hi)

    @pl.when(is_rope)
    def _():
        y = jnp.dot(h_ref[...], w_ref[...], preferred_element_type=F32)
        cos = cos_ref[...]
        sin = sin_ref[...]
        lane = lax.broadcasted_iota(jnp.int32, cos.shape, 1)
        first_half = (lane % DA_HEAD_DIM) < (DA_HEAD_DIM // 2)
        for hh in range(y.shape[1] // LANES):
            yh = y[:, hh * LANES:(hh + 1) * LANES]
            partner = jnp.where(first_half,
                                pltpu.roll(yh, LANES - DA_HEAD_DIM // 2, 1),
                                pltpu.roll(yh, DA_HEAD_DIM // 2, 1))
            o_ref[:, hh * LANES:(hh + 1) * LANES] = (yh * cos + partner * sin).astype(BF16)

    @pl.when(jnp.logical_and(j > 0, jnp.logical_not(is_rope)))
    def _():
        o_ref[...] = jnp.dot(h_ref[...], w_ref[...], preferred_element_type=F32).astype(BF16)


def _in_proj(x2d, g, w_bf16, cols, cos_tab, sin_tab, seq, rope_lo, rope_hi):
    n, d = x2d.shape
    tm, tn = min(TM_IN, seq), TN_IN
    pos_tiles = seq // tm
    assert 0 < rope_lo <= rope_hi, "the first column tile carries the RMSNorm and must not be a rotary tile"
    return pl.pallas_call(
        functools.partial(_in_proj_kernel, rope_lo=rope_lo, rope_hi=rope_hi),
        out_shape=jax.ShapeDtypeStruct((n, cols), BF16),
        grid=(n // tm, cols // tn),
        in_specs=[
            pl.BlockSpec((tm, d), lambda i, j: (i, 0)),
            pl.BlockSpec((1, d), lambda i, j: (0, 0)),
            pl.BlockSpec((d, tn), lambda i, j: (0, j)),
            pl.BlockSpec((tm, LANES), lambda i, j: (i % pos_tiles, 0)),
            pl.BlockSpec((tm, LANES), lambda i, j: (i % pos_tiles, 0)),
        ],
        out_specs=pl.BlockSpec((tm, tn), lambda i, j: (i, j)),
        scratch_shapes=[pltpu.VMEM((tm, d), BF16)],
        compiler_params=_params(("parallel", "arbitrary")),
        name="in_proj",
    )(x2d, g, w_bf16, cos_tab, sin_tab)


def _sb_kernel(q_ref, k_ref, v_ref, o_ref, *, tq, tk, tiles_per_step, scale):
    i = pl.program_id(2)
    row = lax.broadcasted_iota(jnp.int32, (2 * tk, 2 * tk), 0) % tk
    col = lax.broadcasted_iota(jnp.int32, (2 * tk, 2 * tk), 1)
    u2 = jnp.where(jnp.logical_or(col >= tk, row > col), 1.0, 0.0).astype(BF16)

    def span(q, start, width, diag_cols, running):
        k = k_ref[pl.ds(start, width), :]
        v = v_ref[pl.ds(start, width), :]
        z = _nt_dot(q, k) * scale
        if diag_cols is not None:
            r = lax.broadcasted_iota(jnp.int32, z.shape, 0)
            c = lax.broadcasted_iota(jnp.int32, z.shape, 1)
            z = jnp.where((c - diag_cols) < r, z, SB_MASKED_LOGIT)
        sp = jnp.maximum(z, 0.0) + jnp.log(1.0 + jnp.exp2(jnp.abs(z) * (-LOG2_E)))
        hi = sp.astype(BF16)
        lo = (sp - hi.astype(F32)).astype(BF16)
        nb = width // tk
        suffix = [None] * nb
        for blk in range(nb - 1, -1, -1):
            sl = slice(blk * tk, (blk + 1) * tk)
            cs = jnp.dot(jnp.concatenate([hi[:, sl], lo[:, sl]], axis=1), u2, preferred_element_type=F32)
            suffix[blk] = cs[:, :tk] + running
            running = running + cs[:, tk:]
        a = jnp.exp((z - sp) - jnp.concatenate(suffix, axis=1))
        return jnp.dot(a.astype(BF16), v, preferred_element_type=F32), running

    zero_run = jnp.zeros((tq, tk), F32)

    def far_keys(q, first, running, acc):
        def alive_after(running):
            return jnp.min(running) < -SB_LOG_WEIGHT_FLOOR

        def cond(c):
            j, alive, _, _ = c
            return jnp.logical_and(j >= 0, alive)

        def body(c):
            j, _, running, acc = c
            out, running = span(q, pl.multiple_of(j * tq, tq), tq, None, running)
            return j - 1, alive_after(running), running, acc + out

        return lax.while_loop(cond, body, (first, alive_after(running), running, acc))[3]

    def tiles(first_tile, head_has_no_past):
        near = []
        for sub in range(tiles_per_step):
            q = q_ref[sub * tq:(sub + 1) * tq, :]
            if sub == 0 and head_has_no_past:
                near.append(span(q, 0, tq, 0, zero_run))
            else:
                start = (first_tile + sub - 1) * tq
                if not isinstance(start, int):
                    start = pl.multiple_of(start, tq)
                near.append(span(q, start, 2 * tq, tq, zero_run))
        for sub in range(tiles_per_step):
            acc, running = near[sub]
            if not (head_has_no_past and sub < 2):
                acc = far_keys(q_ref[sub * tq:(sub + 1) * tq, :], first_tile + sub - 2, running, acc)
            o_ref[sub * tq:(sub + 1) * tq, :] = acc.astype(o_ref.dtype)

    @pl.when(i == 0)
    def _():
        tiles(0, True)

    @pl.when(i > 0)
    def _():
        tiles(i * tiles_per_step, False)


def _sb_attention(proj3, batch, seq, col_q, col_k, col_v):
    d = SB_HEAD_DIM
    tq, tk = min(TQ_SB, seq), min(TK_SB, seq)
    tps = min(SB_TILES_PER_STEP, seq // tq)
    rows = tq * tps
    return pl.pallas_call(
        functools.partial(_sb_kernel, tq=tq, tk=tk, tiles_per_step=tps, scale=1.0 / math.sqrt(d)),
        out_shape=jax.ShapeDtypeStruct((batch, seq, SB_HEADS * d), BF16),
        grid=(batch, SB_HEADS, seq // rows),
        in_specs=[
            pl.BlockSpec((None, rows, d), lambda b, h, i: (b, i, col_q + h)),
            pl.BlockSpec((None, seq, d), lambda b, h, i: (b, 0, col_k + h)),
            pl.BlockSpec((None, seq, d), lambda b, h, i: (b, 0, col_v + h)),
        ],
        out_specs=pl.BlockSpec((None, rows, d), lambda b, h, i: (b, i, h)),
        compiler_params=_params(("parallel", "parallel", "arbitrary")),
        name="sb_attn",
    )(proj3, proj3, proj3)


def _da_kernel(lam_ref, q_ref, k_ref, v_ref, g_ref, o_ref, vext_ref, *, tq, tiles_per_step, n_steps,
               lam_init):
    i = pl.program_id(2)

    @pl.when(i == 0)
    def _():
        vext_ref[:, :DA_V_DIM] = v_ref[...]
        vext_ref[:, DA_V_DIM:] = jnp.ones((vext_ref.shape[0], DA_V_DIM), BF16)

    lv = lam_ref[...]
    lam = (jnp.exp(jnp.sum(lv[0:1] * lv[1:2], axis=-1, keepdims=True))
           - jnp.exp(jnp.sum(lv[2:3] * lv[3:4], axis=-1, keepdims=True)) + lam_init)

    lane = lax.broadcasted_iota(jnp.int32, (tq, q_ref.shape[1]), 1)
    r = lax.broadcasted_iota(jnp.int32, (tq, tq), 0)
    c = lax.broadcasted_iota(jnp.int32, (tq, tq), 1)
    diag_mask = (c // CHUNK) <= (r // CHUNK)

    def query_maps(rows):
        qs = q_ref[rows, :] * jnp.asarray(1.0 / math.sqrt(DA_HEAD_DIM), BF16)
        zero = jnp.zeros_like(qs)
        return jnp.where(lane < DA_HEAD_DIM, qs, zero), jnp.where(lane >= DA_HEAD_DIM, qs, zero)

    def softmax_av(qm, lo):
        sd = jnp.where(diag_mask, _nt_dot(qm, k_ref[lo:lo + tq, :]), -jnp.inf)
        m = jnp.max(sd, axis=-1, keepdims=True)
        if lo > 0:
            sl = _nt_dot(qm, k_ref[0:lo, :])
            m = jnp.maximum(m, jnp.max(sl, axis=-1, keepdims=True))
        acc = jnp.dot(jnp.exp(sd - m).astype(BF16), vext_ref[lo:lo + tq, :], preferred_element_type=F32)
        if lo > 0:
            acc = acc + jnp.dot(jnp.exp(sl - m).astype(BF16), vext_ref[0:lo, :],
                                preferred_element_type=F32)
        return acc[:, :DA_V_DIM] / acc[:, DA_V_DIM:]

    for step in range(n_steps):
        @pl.when(i == step)
        def _(step=step):
            for sub in range(tiles_per_step):
                rows = slice(sub * tq, (sub + 1) * tq)
                lo = (step * tiles_per_step + sub) * tq
                q1, q2 = query_maps(rows)
                o = softmax_av(q1, lo) - lam * softmax_av(q2, lo)
                ms = jnp.mean(o * o, axis=-1, keepdims=True)
                o_ref[rows, :] = (o * lax.rsqrt(ms + EPS) * g_ref[...] * (1.0 - lam_init)).astype(o_ref.dtype)


def _da_attention(lam_vecs, proj3, sub_g, batch, seq, col_q, col_k, col_v, lam_init):
    d = DA_V_DIM
    tq = min(TQ_DA, seq)
    tps = min(DA_TILES_PER_STEP, seq // tq)
    rows = tq * tps
    return pl.pallas_call(
        functools.partial(_da_kernel, tq=tq, tiles_per_step=tps, n_steps=seq // rows, lam_init=lam_init),
        out_shape=jax.ShapeDtypeStruct((batch, seq, DA_HEADS * d), BF16),
        grid=(batch, DA_HEADS, seq // rows),
        in_specs=[
            pl.BlockSpec(lam_vecs.shape, lambda b, h, i: (0, 0)),
            pl.BlockSpec((None, rows, d), lambda b, h, i: (b, i, col_q + h)),
            pl.BlockSpec((None, seq, d), lambda b, h, i: (b, 0, col_k + h)),
            pl.BlockSpec((None, seq, d), lambda b, h, i: (b, 0, col_v + h)),
            pl.BlockSpec((1, d), lambda b, h, i: (0, 0)),
        ],
        out_specs=pl.BlockSpec((None, rows, d), lambda b, h, i: (b, i, h)),
        scratch_shapes=[pltpu.VMEM((seq, 2 * d), BF16)],
        compiler_params=_params(("parallel", "parallel", "arbitrary")),
        name="da_attn",
    )(lam_vecs, proj3, proj3, proj3, sub_g)


def _gate_branch_kernel(x_ref, gmix_ref, osb_ref, oda_ref, wgs_ref, wgd_ref, wbs_ref, wbd_ref, m_ref, h_scr):
    def column_tile(h):
        gate_sb = jax.nn.sigmoid(jnp.dot(h, wgs_ref[...], preferred_element_type=F32))
        gate_da = jax.nn.sigmoid(jnp.dot(h, wgd_ref[...], preferred_element_type=F32))
        br_sb = jnp.dot(osb_ref[...], wbs_ref[...], preferred_element_type=F32)
        br_da = jnp.dot(oda_ref[...], wbd_ref[...], preferred_element_type=F32)
        m_ref[...] = (gate_sb * br_sb + gate_da * br_da).astype(m_ref.dtype)

    @pl.when(pl.program_id(1) == 0)
    def _():
        x = x_ref[...]
        ms = jnp.mean(x * x, axis=-1, keepdims=True)
        h = (x * lax.rsqrt(ms + EPS) * gmix_ref[...]).astype(BF16)
        h_scr[...] = h
        column_tile(h)

    @pl.when(pl.program_id(1) > 0)
    def _():
        column_tile(h_scr[...])


def _gate_branch(x2d, gmix, o_sb, o_da, w_in_bf16, gate_col0, wbs, wbd):
    n, d = x2d.shape
    tm, tn = min(TM_GATE, n), TN_GATE
    wsb, wda = o_sb.shape[1], o_da.shape[1]
    sb0, da0 = gate_col0 // tn, (gate_col0 + d) // tn
    return pl.pallas_call(
        _gate_branch_kernel,
        out_shape=jax.ShapeDtypeStruct((n, d), BF16),
        grid=(n // tm, d // tn),
        in_specs=[
            pl.BlockSpec((tm, d), lambda i, j: (i, 0)),
            pl.BlockSpec((1, d), lambda i, j: (0, 0)),
            pl.BlockSpec((tm, wsb), lambda i, j: (i, 0)),
            pl.BlockSpec((tm, wda), lambda i, j: (i, 0)),
            pl.BlockSpec((d, tn), lambda i, j: (0, sb0 + j)),
            pl.BlockSpec((d, tn), lambda i, j: (0, da0 + j)),
            pl.BlockSpec((wsb, tn), lambda i, j: (0, j)),
            pl.BlockSpec((wda, tn), lambda i, j: (0, j)),
        ],
        out_specs=pl.BlockSpec((tm, tn), lambda i, j: (i, j)),
        scratch_shapes=[pltpu.VMEM((tm, d), BF16)],
        compiler_params=_params(("parallel", "arbitrary")),
        name="gate_branch",
    )(x2d, gmix, o_sb, o_da, w_in_bf16, w_in_bf16, wbs, wbd)


def _out_proj_kernel(m_ref, x_ref, wo_ref, gffn_ref, wr_hi_ref, wr_lo_ref, x1_ref, h2_ref, lt_ref):
    x1 = x_ref[...] + jnp.dot(m_ref[...], wo_ref[...], preferred_element_type=F32)
    x1_ref[...] = x1
    ms = jnp.mean(x1 * x1, axis=-1, keepdims=True)
    h2 = x1 * lax.rsqrt(ms + EPS) * gffn_ref[...]
    h2_ref[...] = h2
    h_hi = h2.astype(BF16)
    h_lo = (h2 - h_hi.astype(F32)).astype(BF16)
    lt_ref[...] = (_nt_dot(wr_hi_ref[...], h_hi) + _nt_dot(wr_lo_ref[...], h_hi)
                   + _nt_dot(wr_hi_ref[...], h_lo))


def _out_proj(merged, x2d, wo, gffn, wr_t):
    n, d = x2d.shape
    tm = min(TM_OUT, n)
    once = pl.Buffered(1)
    wr_hi = wr_t.astype(BF16)
    wr_lo = (wr_t - wr_hi.astype(F32)).astype(BF16)
    return pl.pallas_call(
        _out_proj_kernel,
        out_shape=(jax.ShapeDtypeStruct((n, d), F32),
                   jax.ShapeDtypeStruct((n, d), F32),
                   jax.ShapeDtypeStruct((ROUTER_ROWS, n), F32)),
        grid=(n // tm,),
        in_specs=[
            pl.BlockSpec((tm, d), lambda i: (i, 0)),
            pl.BlockSpec((tm, d), lambda i: (i, 0)),
            pl.BlockSpec((d, d), lambda i: (0, 0), pipeline_mode=once),
            pl.BlockSpec((1, d), lambda i: (0, 0), pipeline_mode=once),
            pl.BlockSpec((ROUTER_ROWS, d), lambda i: (0, 0), pipeline_mode=once),
            pl.BlockSpec((ROUTER_ROWS, d), lambda i: (0, 0), pipeline_mode=once),
        ],
        out_specs=(pl.BlockSpec((tm, d), lambda i: (i, 0)),
                   pl.BlockSpec((tm, d), lambda i: (i, 0)),
                   pl.BlockSpec((ROUTER_ROWS, tm), lambda i: (0, i))),
        compiler_params=_params(("parallel",)),
        name="out_proj",
    )(merged, x2d, wo, gffn, wr_hi, wr_lo)


def _route_kernel(lt_ref, ids_ref, rank_ref, gate_ref, cnt_ref, carry_ref):
    step = pl.program_id(0)

    @pl.when(step == 0)
    def _():
        carry_ref[...] = jnp.zeros_like(carry_ref)

    t = lt_ref.shape[1]
    el = lt_ref[0:N_EXPERTS, :]
    gl = lt_ref[N_EXPERTS:N_EXPERTS + 8, :]
    grow = lax.broadcasted_iota(jnp.int32, gl.shape, 0)
    gl = jnp.where(grow < N_GROUPS, gl, -jnp.inf)
    gmax = jnp.max(gl, axis=0, keepdims=True)
    grp = jnp.min(jnp.where(gl == gmax, grow, N_GROUPS), axis=0, keepdims=True)
    p_grp = 1.0 / jnp.sum(jnp.exp(gl - gmax), axis=0, keepdims=True)

    eidx = lax.broadcasted_iota(jnp.int32, el.shape, 0)
    in_grp = (eidx // EXPERTS_PER_GROUP) == grp
    e1 = jnp.where(in_grp, el, -jnp.inf)
    top1 = jnp.max(e1, axis=0, keepdims=True)
    idx1 = jnp.min(jnp.where(e1 == top1, eidx, N_EXPERTS), axis=0, keepdims=True)
    e2 = jnp.where(eidx == idx1, -jnp.inf, e1)
    top2 = jnp.max(e2, axis=0, keepdims=True)
    idx2 = jnp.min(jnp.where(e2 == top2, eidx, N_EXPERTS), axis=0, keepdims=True)
    dlt = jnp.exp(top2 - top1)
    den = 1.0 + dlt
    gate_ref[0:1, :] = p_grp / den
    gate_ref[1:2, :] = p_grp * dlt / den
    ids_ref[0:1, :] = idx1
    ids_ref[1:2, :] = idx2

    oh1 = jnp.where(eidx == idx1, 1.0, 0.0)
    oh2 = jnp.where(eidx == idx2, 1.0, 0.0)
    rr = lax.broadcasted_iota(jnp.int32, (t, t), 0)
    cc = lax.broadcasted_iota(jnp.int32, (t, t), 1)
    before = jnp.where(rr < cc, 1.0, 0.0).astype(BF16)
    pre1 = jnp.dot(oh1.astype(BF16), before, preferred_element_type=F32)
    pre2 = jnp.dot(oh2.astype(BF16), before, preferred_element_type=F32)
    cnt1 = jnp.sum(oh1, axis=1, keepdims=True)
    cnt2 = jnp.sum(oh2, axis=1, keepdims=True)
    carry = carry_ref[:, 0:1]
    rank1 = jnp.sum(oh1 * (carry + pre1), axis=0, keepdims=True)
    rank2 = jnp.sum(oh2 * (carry + cnt1 + pre2), axis=0, keepdims=True)
    rank_ref[0:1, :] = rank1.astype(jnp.int32)
    rank_ref[1:2, :] = rank2.astype(jnp.int32)
    new_carry = carry_ref[...] + cnt1 + cnt2
    carry_ref[...] = new_carry
    cnt_ref[...] = new_carry


def _route(logits_t):
    n = logits_t.shape[1]
    t = min(T_ROUTE, n)
    return pl.pallas_call(
        _route_kernel,
        out_shape=(jax.ShapeDtypeStruct((2, n), jnp.int32),
                   jax.ShapeDtypeStruct((2, n), jnp.int32),
                   jax.ShapeDtypeStruct((2, n), F32),
                   jax.ShapeDtypeStruct((N_EXPERTS, LANES), F32)),
        grid=(n // t,),
        in_specs=[pl.BlockSpec((ROUTER_ROWS, t), lambda i: (0, i))],
        out_specs=(pl.BlockSpec((2, t), lambda i: (0, i)),
                   pl.BlockSpec((2, t), lambda i: (0, i)),
                   pl.BlockSpec((2, t), lambda i: (0, i)),
                   pl.BlockSpec((N_EXPERTS, LANES), lambda i: (0, 0))),
        scratch_shapes=[pltpu.VMEM((N_EXPERTS, LANES), F32)],
        compiler_params=_params(("arbitrary",)),
        name="route",
    )(logits_t)


def _dispatch_kernel(zs_ref, zc_ref, nu_ref, dest_ref, h2_ref, xb_hbm, zeros_ref, sem, tail_sem, *, tm, te,
                     n_tiles):
    step = pl.program_id(0)

    def tail_copy(t):
        return pltpu.make_async_copy(zeros_ref, xb_hbm.at[pl.ds(pl.multiple_of(t * te, te), te)], tail_sem)

    @pl.when(step == 0)
    def _():
        zeros_ref[...] = jnp.zeros_like(zeros_ref)

        def pad_copy(row):
            return pltpu.make_async_copy(zeros_ref.at[pl.ds(0, SUBLANES)],
                                         xb_hbm.at[pl.ds(pl.multiple_of(row, SUBLANES), SUBLANES)], sem)

        total = 0
        for e in range(N_EXPERTS):
            lax.fori_loop(0, zc_ref[e], lambda c, _, e=e: (pad_copy(zs_ref[e] + c * SUBLANES).start(), 0)[1], 0)
            total = total + zc_ref[e]
        lax.fori_loop(0, total, lambda c, _: (pad_copy(0).wait(), 0)[1], 0)
        lax.fori_loop(nu_ref[0], n_tiles, lambda t, c: (tail_copy(t).start(), c)[1], 0)

    def issue(blk, _):
        r0 = pl.multiple_of(blk * SUBLANES, SUBLANES)
        for u in range(SUBLANES):
            src = h2_ref.at[pl.ds(r0 + u, 1)]
            for k in range(2):
                pltpu.make_async_copy(src, xb_hbm.at[pl.ds(dest_ref[0, 0, k * tm + r0 + u], 1)],
                                      sem).start(priority=k)
        return 0

    lax.fori_loop(0, tm // SUBLANES, issue, 0)
    for _ in range(2):
        pltpu.make_async_copy(h2_ref, xb_hbm.at[pl.ds(0, tm)], sem).wait()

    @pl.when(step == pl.num_programs(0) - 1)
    def _():
        lax.fori_loop(nu_ref[0], n_tiles, lambda t, c: (tail_copy(t).wait(), c)[1], 0)


def _dispatch(zero_starts, zero_chunks, n_used, dest_tiles, h2, n_tiles, tm, te):
    n, d = h2.shape
    return pl.pallas_call(
        functools.partial(_dispatch_kernel, tm=tm, te=te, n_tiles=n_tiles),
        out_shape=jax.ShapeDtypeStruct((n_tiles * te, d), h2.dtype),
        grid_spec=pltpu.PrefetchScalarGridSpec(
            num_scalar_prefetch=3,
            grid=(n // tm,),
            in_specs=[
                pl.BlockSpec((1, 1, 2 * tm), lambda i, *_: (i, 0, 0), memory_space=pltpu.SMEM),
                pl.BlockSpec((tm, d), lambda i, *_: (i, 0)),
            ],
            out_specs=pl.BlockSpec(memory_space=pl.ANY),
            scratch_shapes=[pltpu.VMEM((te, d), h2.dtype), pltpu.SemaphoreType.DMA,
                            pltpu.SemaphoreType.DMA],
        ),
        compiler_params=_params(("arbitrary",)),
        name="dispatch",
    )(zero_starts, zero_chunks, n_used, dest_tiles, h2)


def _expert_kernel(te_ref, nu_ref, slot_ref, next_ref, xb_hbm, wg_hbm, wu_hbm, wd_hbm, yb_ref,
                   wg_f, wu_f, wd_f, wg_s, wu_s, wd_s, sems, xb_buf, xb_sems, *, te):
    t = pl.program_id(0)
    used = t < nu_ref[0]
    new_run = jnp.logical_or(t == 0, te_ref[t] != te_ref[jnp.maximum(t - 1, 0)])

    def weight_copies(expert, slot):
        return [pltpu.make_async_copy(src.at[expert], dst.at[slot], sems.at[slot, i])
                for i, (src, dst) in enumerate(((wg_hbm, wg_f), (wu_hbm, wu_f), (wd_hbm, wd_f)))]

    @pl.when(t == 0)
    def _():
        for c in weight_copies(te_ref[0], 0):
            c.start()

    @pl.when(jnp.logical_and(used, new_run))
    def _():
        slot = slot_ref[t]
        for c in weight_copies(te_ref[t], slot):
            c.wait()

        @pl.when(next_ref[t] >= 0)
        def _():
            for c in weight_copies(next_ref[t], 1 - slot):
                c.start()

        wg_s[...] = wg_f[slot].astype(BF16)
        wu_s[...] = wu_f[slot].astype(BF16)
        wd_s[...] = wd_f[slot].astype(BF16)

    def row_copy(tile):
        rows = pl.ds(pl.multiple_of(tile * te, te), te)
        slot = tile % XB_SLOTS
        return pltpu.make_async_copy(xb_hbm.at[rows], xb_buf.at[slot], xb_sems.at[slot])

    @pl.when(t == 0)
    def _():
        for ahead in range(XB_SLOTS - 1):
            @pl.when(ahead < nu_ref[0])
            def _(ahead=ahead):
                row_copy(ahead).start()

    @pl.when(used)
    def _():
        @pl.when(t + (XB_SLOTS - 1) < nu_ref[0])
        def _():
            row_copy(t + (XB_SLOTS - 1)).start()

        row_copy(t).wait()
        x = xb_buf[t % XB_SLOTS].astype(BF16)
        g = jnp.dot(x, wg_s[...], preferred_element_type=F32)
        u = jnp.dot(x, wu_s[...], preferred_element_type=F32)
        hid = (g * jax.nn.sigmoid(g)) * u
        yb_ref[...] = jnp.dot(hid.astype(BF16), wd_s[...], preferred_element_type=F32)

    @pl.when(jnp.logical_not(used))
    def _():
        yb_ref[...] = jnp.zeros_like(yb_ref)


def _experts(tile_expert, n_used, run_slot, next_expert, xb, wg, wu, wd, n_tiles, te):
    d = xb.shape[1]
    de = wg.shape[2]

    hbm = pl.BlockSpec(memory_space=pl.ANY)
    return pl.pallas_call(
        functools.partial(_expert_kernel, te=te),
        out_shape=jax.ShapeDtypeStruct((n_tiles * te, d), F32),
        grid_spec=pltpu.PrefetchScalarGridSpec(
            num_scalar_prefetch=4,
            grid=(n_tiles,),
            in_specs=[hbm, hbm, hbm, hbm],
            out_specs=pl.BlockSpec((te, d), lambda t, *_: (t, 0)),
            scratch_shapes=[pltpu.VMEM((2, d, de), F32), pltpu.VMEM((2, d, de), F32), pltpu.VMEM((2, de, d), F32),
                            pltpu.VMEM((d, de), BF16), pltpu.VMEM((d, de), BF16), pltpu.VMEM((de, d), BF16),
                            pltpu.SemaphoreType.DMA((2, 3)),
                            pltpu.VMEM((XB_SLOTS, te, d), xb.dtype), pltpu.SemaphoreType.DMA((XB_SLOTS,))],
        ),
        compiler_params=_params(("arbitrary",)),
        name="experts",
    )(tile_expert, n_used, run_slot, next_expert, xb, wg, wu, wd)


def _combine_kernel(dest_ref, next_dest_ref, x1_ref, gate_ref, g_ref, yb_hbm, o_ref, buf, sems, *, tm,
                    final_norm):
    i = pl.program_id(0)
    slot = i % 2

    def row_copy(idx_ref, slot, row, k):
        return pltpu.make_async_copy(yb_hbm.at[pl.ds(idx_ref[0, 0, row], 1)], buf.at[slot, pl.ds(row, 1)],
                                     sems.at[slot])

    def tile_wait(slot):
        pltpu.make_async_copy(yb_hbm.at[pl.ds(0, 2 * tm)], buf.at[slot], sems.at[slot]).wait()

    @pl.when(i == 0)
    def _():
        def issue(r, _):
            for k in range(2):
                row_copy(dest_ref, 0, k * tm + r, k).start(priority=k)
            return 0
        lax.fori_loop(0, tm, issue, 0, unroll=8)

    tile_wait(slot)
    for r in range(tm):
        for k in range(2):
            row_copy(next_dest_ref, 1 - slot, k * tm + r, k).start(priority=k)
    gates = gate_ref[...]
    g0, g1 = gates[:, 0:1], gates[:, 1:2]
    x = x1_ref[...] + (buf[slot, 0:tm, :] * g0 + buf[slot, tm:2 * tm, :] * g1)
    if final_norm:
        ms = jnp.mean(x * x, axis=-1, keepdims=True)
        x = x * lax.rsqrt(ms + EPS) * g_ref[...]
    o_ref[...] = x

    @pl.when(i == pl.num_programs(0) - 1)
    def _():
        tile_wait(1 - slot)


def _combine(dest_tiles, x1, gates, g_final, yb, tm, final_norm):
    n, d = x1.shape
    n_steps = n // tm
    return pl.pallas_call(
        functools.partial(_combine_kernel, tm=tm, final_norm=final_norm),
        out_shape=jax.ShapeDtypeStruct((n, d), F32),
        grid=(n_steps,),
        in_specs=[
            pl.BlockSpec((1, 1, 2 * tm), lambda i: (i, 0, 0), memory_space=pltpu.SMEM),
            pl.BlockSpec((1, 1, 2 * tm), lambda i: (jnp.minimum(i + 1, n_steps - 1), 0, 0),
                         memory_space=pltpu.SMEM),
            pl.BlockSpec((tm, d), lambda i: (i, 0)),
            pl.BlockSpec((tm, 2), lambda i: (i, 0)),
            pl.BlockSpec((1, d), lambda i: (0, 0)),
            pl.BlockSpec(memory_space=pl.ANY),
        ],
        out_specs=pl.BlockSpec((tm, d), lambda i: (i, 0)),
        scratch_shapes=[pltpu.VMEM((2, 2 * tm, d), F32), pltpu.SemaphoreType.DMA((2,))],
        compiler_params=_params(("arbitrary",)),
        name="combine",
    )(dest_tiles, dest_tiles, x1, gates, g_final, yb)


def _rope_tables(seq):
    half = DA_HEAD_DIM // 2
    inv_freq = ROPE_THETA ** (-(jnp.arange(half, dtype=F32) * 2.0 / DA_HEAD_DIM))
    ang = jnp.arange(seq, dtype=F32)[:, None] * inv_freq[None, :]
    cos, sin = jnp.cos(ang), jnp.sin(ang)
    cos_tab = jnp.tile(cos, (1, LANES // half))
    sin_tab = jnp.tile(jnp.concatenate([-sin, sin], axis=1), (1, LANES // DA_HEAD_DIM))
    return cos_tab, sin_tab


def _tile_indices(dest, tm):
    n = dest.shape[1]
    return dest.reshape(2, n // tm, tm).transpose(1, 0, 2).reshape(n // tm, 1, 2 * tm)


def _layer(x2d, batch, seq, layer, mix_g, w_in, lam_vecs, sub_g, w_bsb, w_bda, w_out, ffn_g,
           w_grp, w_exp, w_gate, w_up, w_down):
    n, d = x2d.shape
    sbw = SB_HEADS * SB_HEAD_DIM
    daw = DA_HEADS * DA_V_DIM
    qkv_cols = 3 * sbw + 3 * daw
    lam_init = 0.8 - 0.6 * math.exp(-0.3 * layer)

    w_in_bf16 = w_in.astype(BF16)
    cos_tab, sin_tab = _rope_tables(seq)
    rope_lo = 3 * sbw // TN_IN
    rope_hi = (3 * sbw + 2 * daw) // TN_IN
    proj = _in_proj(x2d, mix_g[None, :], w_in_bf16, qkv_cols, cos_tab, sin_tab, seq, rope_lo, rope_hi)
    proj3 = proj.reshape(batch, seq, qkv_cols)
    nsb = sbw // SB_HEAD_DIM
    nda = daw // DA_V_DIM
    o_sb = _sb_attention(proj3, batch, seq, 0, nsb, 2 * nsb)
    o_da = _da_attention(lam_vecs, proj3, sub_g[None, :], batch, seq,
                         3 * nsb, 3 * nsb + nda, 3 * nsb + 2 * nda, lam_init)

    wr_t = jnp.zeros((ROUTER_ROWS, d), F32)
    wr_t = wr_t.at[:N_EXPERTS].set(w_exp.T).at[N_EXPERTS:N_EXPERTS + N_GROUPS].set(w_grp.T)
    merged = _gate_branch(x2d, mix_g[None, :], o_sb.reshape(n, sbw), o_da.reshape(n, daw),
                          w_in_bf16, qkv_cols, w_bsb.astype(BF16), w_bda.astype(BF16))
    x1, h2, logits_t = _out_proj(merged, x2d, w_out.astype(BF16), ffn_g[None, :], wr_t)

    ids, ranks, gates, counts = _route(logits_t)
    te = TM_EXPERT
    counts = counts[:, 0].astype(jnp.int32)
    padded = (counts + te - 1) // te * te
    pad_ends = jnp.cumsum(padded)
    pad_starts = pad_ends - padded
    expert_ids = jnp.arange(N_EXPERTS, dtype=jnp.int32)
    dest = ranks + jnp.sum(jnp.where(ids[..., None] == expert_ids, pad_starts, 0), axis=-1)
    n_tiles = (2 * n) // te + N_EXPERTS
    n_used = (pad_ends[-1] // te).astype(jnp.int32)[None]
    tile_row0 = jnp.arange(n_tiles, dtype=jnp.int32) * te
    tile_expert = jnp.minimum(jnp.sum((pad_ends[None, :] <= tile_row0[:, None]).astype(jnp.int32), axis=1),
                              N_EXPERTS - 1)
    zero_starts = ((pad_starts + counts) // SUBLANES * SUBLANES).astype(jnp.int32)
    zero_chunks = ((pad_ends - zero_starts) // SUBLANES).astype(jnp.int32)

    tmd = min(TM_DISPATCH, n)
    xb = _dispatch(zero_starts, zero_chunks, n_used, _tile_indices(dest, tmd), h2, n_tiles, tmd, te)
    tile_ids = jnp.arange(n_tiles, dtype=jnp.int32)
    new_run = jnp.logical_or(tile_ids == 0, tile_expert != jnp.roll(tile_expert, 1))
    run_slot = ((jnp.cumsum(new_run.astype(jnp.int32)) - 1) % 2).astype(jnp.int32)
    later = jnp.logical_and(tile_ids[None, :] < n_used, tile_expert[None, :] > tile_expert[:, None])
    next_expert = jnp.min(jnp.where(later, tile_expert[None, :], N_EXPERTS), axis=1)
    next_expert = jnp.where(next_expert == N_EXPERTS, -1, next_expert).astype(jnp.int32)
    yb = _experts(tile_expert, n_used, run_slot, next_expert, xb, w_gate, w_up, w_down, n_tiles, te)
    return x1, gates, dest, yb


def kernel(x, mix_norm_g, w_in, lambda_q1, lambda_k1, lambda_q2, lambda_k2, diff_subnorm_g, w_branch_sb, w_branch_diff, w_out, ffn_norm_g, w_group_router, w_expert_router, w_gate, w_up, w_down, final_norm_g):
    batch, seq, d = x.shape
    depth = w_in.shape[0]
    n = batch * seq
    x2d = x.reshape(n, d)
    tmc = min(TM_COMBINE, n)
    for l in range(depth):
        lam_vecs = jnp.stack([lambda_q1[l], lambda_k1[l], lambda_q2[l], lambda_k2[l]]).astype(F32)
        x1, gates, dest, yb = _layer(
            x2d, batch, seq, l, mix_norm_g[l], w_in[l], lam_vecs, diff_subnorm_g[l], w_branch_sb[l],
            w_branch_diff[l], w_out[l], ffn_norm_g[l], w_group_router[l], w_expert_router[l],
            w_gate[l], w_up[l], w_down[l])
        x2d = _combine(_tile_indices(dest, tmc), x1, gates.T, final_norm_g[None, :], yb, tmc,
                       final_norm=(l == depth - 1))
    return x2d.reshape(batch, seq, d)
```

```python
import functools
import math

import jax
import jax.numpy as jnp
from jax import lax
from jax.experimental import pallas as pl
from jax.experimental.pallas import tpu as pltpu

F32 = jnp.float32
BF16 = jnp.bfloat16

CHUNK = 64
SB_HEADS = 8
SB_HEAD_DIM = 128
DA_HEADS = 8
DA_HEAD_DIM = 64
DA_V_DIM = 2 * DA_HEAD_DIM
N_GROUPS = 4
EXPERTS_PER_GROUP = 8
N_EXPERTS = N_GROUPS * EXPERTS_PER_GROUP
ROPE_THETA = 10000.0
EPS = 1e-6

LANES = 128
SUBLANES = 8
ROUTER_ROWS = 40
VMEM_LIMIT_BYTES = 56 * 1024 * 1024

SB_LOG_WEIGHT_FLOOR = -104.0
SB_MASKED_LOGIT = -1e30
LOG2_E = 1.4426950408889634

TM_IN = 1024
TN_IN = 1024
TQ_SB = 256
TK_SB = 128
SB_TILES_PER_STEP = 4
SB_NEAR_PAST = 256
TQ_DA = 256
DA_TILES_PER_STEP = 4
TM_GATE = 1024
TN_GATE = 256
TM_OUT = 512
T_ROUTE = 512
TM_DISPATCH = 1024
TM_EXPERT = 256
XB_SLOTS = 3
TM_COMBINE = 256


def _params(sem, vmem=VMEM_LIMIT_BYTES):
    return pltpu.CompilerParams(dimension_semantics=sem, vmem_limit_bytes=vmem)


def _nt_dot(a, b, **kw):
    return lax.dot_general(a, b, (((1,), (1,)), ((), ())), preferred_element_type=F32, **kw)


def _in_proj_kernel(x_ref, g_ref, w_ref, cos_ref, sin_ref, o_ref, h_ref, *, rope_lo, rope_hi):
    j = pl.program_id(1)

    @pl.when(j == 0)
    def _():
        x = x_ref[...]
        ms = jnp.mean(x * x, axis=-1, keepdims=True)
        h = (x * lax.rsqrt(ms + EPS) * g_ref[...]).astype(BF16)
        h_ref[...] = h
        o_ref[...] = jnp.dot(h, w_ref[...], preferred_element_type=F32).astype(BF16)

    is_rope = jnp.logical_and(j >= rope_lo, j < rope_hi)

    @pl.when(is_rope)
    def _():
        y = jnp.dot(h_ref[...], w_ref[...], preferred_element_type=F32)
        cos = cos_ref[...]
        sin = sin_ref[...]
        lane = lax.broadcasted_iota(jnp.int32, cos.shape, 1)
        first_half = (lane % DA_HEAD_DIM) < (DA_HEAD_DIM // 2)
        for hh in range(y.shape[1] // LANES):
            yh = y[:, hh * LANES:(hh + 1) * LANES]
            partner = jnp.where(first_half,
                                pltpu.roll(yh, LANES - DA_HEAD_DIM // 2, 1),
                                pltpu.roll(yh, DA_HEAD_DIM // 2, 1))
            o_ref[:, hh * LANES:(hh + 1) * LANES] = (yh * cos + partner * sin).astype(BF16)

    @pl.when(jnp.logical_and(j > 0, jnp.logical_not(is_rope)))
    def _():
        o_ref[...] = jnp.dot(h_ref[...], w_ref[...], preferred_element_type=F32).astype(BF16)


def _in_proj(x2d, g, w_bf16, cols, cos_tab, sin_tab, seq, rope_lo, rope_hi):
    n, d = x2d.shape
    tm, tn = min(TM_IN, seq), TN_IN
    pos_tiles = seq // tm
    assert 0 < rope_lo <= rope_hi, "the first column tile carries the RMSNorm and must not be a rotary tile"
    return pl.pallas_call(
        functools.partial(_in_proj_kernel, rope_lo=rope_lo, rope_hi=rope_hi),
        out_shape=jax.ShapeDtypeStruct((n, cols), BF16),
        grid=(n // tm, cols // tn),
        in_specs=[
            pl.BlockSpec((tm, d), lambda i, j: (i, 0)),
            pl.BlockSpec((1, d), lambda i, j: (0, 0)),
            pl.BlockSpec((d, tn), lambda i, j: (0, j)),
            pl.BlockSpec((tm, LANES), lambda i, j: (i % pos_tiles, 0)),
            pl.BlockSpec((tm, LANES), lambda i, j: (i % pos_tiles, 0)),
        ],
        out_specs=pl.BlockSpec((tm, tn), lambda i, j: (i, j)),
        scratch_shapes=[pltpu.VMEM((tm, d), BF16)],
        compiler_params=_params(("parallel", "arbitrary")),
        name="in_proj",
    )(x2d, g, w_bf16, cos_tab, sin_tab)


def _sb_kernel(q_ref, k_ref, v_ref, o_ref, *, tq, tk, tiles_per_step, near_past, scale):
    i = pl.program_id(2)
    row = lax.broadcasted_iota(jnp.int32, (2 * tk, 2 * tk), 0) % tk
    col = lax.broadcasted_iota(jnp.int32, (2 * tk, 2 * tk), 1)
    u2 = jnp.where(jnp.logical_or(col >= tk, row > col), 1.0, 0.0).astype(BF16)

    def span(q, start, width, diag_cols, running):
        k = k_ref[pl.ds(start, width), :]
        v = v_ref[pl.ds(start, width), :]
        z = _nt_dot(q, k) * scale
        if diag_cols is not None:
            r = lax.broadcasted_iota(jnp.int32, z.shape, 0)
            c = lax.broadcasted_iota(jnp.int32, z.shape, 1)
            z = jnp.where((c - diag_cols) < r, z, SB_MASKED_LOGIT)
        sp = jnp.maximum(z, 0.0) + jnp.log(1.0 + jnp.exp2(jnp.abs(z) * (-LOG2_E)))
        hi = sp.astype(BF16)
        lo = (sp - hi.astype(F32)).astype(BF16)
        nb = width // tk
        suffix = [None] * nb
        for blk in range(nb - 1, -1, -1):
            sl = slice(blk * tk, (blk + 1) * tk)
            cs = jnp.dot(jnp.concatenate([hi[:, sl], lo[:, sl]], axis=1), u2, preferred_element_type=F32)
            suffix[blk] = cs[:, :tk] + running
            running = running + cs[:, tk:]
        a = jnp.exp((z - sp) - jnp.concatenate(suffix, axis=1))
        return jnp.dot(a.astype(BF16), v, preferred_element_type=F32), running

    zero_run = jnp.zeros((tq, tk), F32)

    def far_keys(q, first, running, acc):
        def alive_after(running):
            return jnp.min(running) < -SB_LOG_WEIGHT_FLOOR

        def cond(c):
            j, alive, _, _ = c
            return jnp.logical_and(j >= 0, alive)

        def body(c):
            j, _, running, acc = c
            out, running = span(q, pl.multiple_of(j * tk, tk), tk, None, running)
            return j - 1, alive_after(running), running, acc + out

        return lax.while_loop(cond, body, (first, alive_after(running), running, acc))[3]

    def tiles(first_row, at_start):
        near, starts = [], []
        for sub in range(tiles_per_step):
            q = q_ref[sub * tq:(sub + 1) * tq, :]
            t0 = first_row + sub * tq
            past = min(near_past, t0) if at_start else near_past
            start = t0 - past
            if not at_start:
                start = pl.multiple_of(start, tk)
            near.append(span(q, start, tq + past, past, zero_run))
            starts.append(start)
        for sub in range(tiles_per_step):
            acc, running = near[sub]
            if not (at_start and starts[sub] == 0):
                acc = far_keys(q_ref[sub * tq:(sub + 1) * tq, :], starts[sub] // tk - 1, running, acc)
            o_ref[sub * tq:(sub + 1) * tq, :] = acc.astype(o_ref.dtype)

    @pl.when(i == 0)
    def _():
        tiles(0, True)

    @pl.when(i > 0)
    def _():
        tiles(i * (tiles_per_step * tq), False)


def _sb_attention(proj3, batch, seq, col_q, col_k, col_v):
    d = SB_HEAD_DIM
    tq, tk = min(TQ_SB, seq), min(TK_SB, seq)
    tps = min(SB_TILES_PER_STEP, seq // tq)
    rows = tq * tps
    assert rows == seq or rows >= SB_NEAR_PAST, "later steps assume a full near span of past keys"
    return pl.pallas_call(
        functools.partial(_sb_kernel, tq=tq, tk=tk, tiles_per_step=tps, near_past=SB_NEAR_PAST,
                          scale=1.0 / math.sqrt(d)),
        out_shape=jax.ShapeDtypeStruct((batch, seq, SB_HEADS * d), BF16),
        grid=(batch, SB_HEADS, seq // rows),
        in_specs=[
            pl.BlockSpec((None, rows, d), lambda b, h, i: (b, i, col_q + h)),
            pl.BlockSpec((None, seq, d), lambda b, h, i: (b, 0, col_k + h)),
            pl.BlockSpec((None, seq, d), lambda b, h, i: (b, 0, col_v + h)),
        ],
        out_specs=pl.BlockSpec((None, rows, d), lambda b, h, i: (b, i, h)),
        compiler_params=_params(("parallel", "parallel", "arbitrary")),
        name="sb_attn",
    )(proj3, proj3, proj3)


def _da_kernel(lam_ref, q_ref, k_ref, v_ref, g_ref, o_ref, vext_ref, *, tq, tiles_per_step, n_steps,
               lam_init):
    i = pl.program_id(2)

    @pl.when(i == 0)
    def _():
        vext_ref[:, :DA_V_DIM] = v_ref[...]
        vext_ref[:, DA_V_DIM:] = jnp.ones((vext_ref.shape[0], DA_V_DIM), BF16)

    lv = lam_ref[...]
    lam = (jnp.exp(jnp.sum(lv[0:1] * lv[1:2], axis=-1, keepdims=True))
           - jnp.exp(jnp.sum(lv[2:3] * lv[3:4], axis=-1, keepdims=True)) + lam_init)

    lane = lax.broadcasted_iota(jnp.int32, (tq, q_ref.shape[1]), 1)
    r = lax.broadcasted_iota(jnp.int32, (tq, tq), 0)
    c = lax.broadcasted_iota(jnp.int32, (tq, tq), 1)
    diag_mask = (c // CHUNK) <= (r // CHUNK)

    def query_maps(rows):
        qs = q_ref[rows, :] * jnp.asarray(1.0 / math.sqrt(DA_HEAD_DIM), BF16)
        zero = jnp.zeros_like(qs)
        return jnp.where(lane < DA_HEAD_DIM, qs, zero), jnp.where(lane >= DA_HEAD_DIM, qs, zero)

    def softmax_av(qm, lo):
        sd = jnp.where(diag_mask, _nt_dot(qm, k_ref[lo:lo + tq, :]), -jnp.inf)
        m = jnp.max(sd, axis=-1, keepdims=True)
        if lo > 0:
            sl = _nt_dot(qm, k_ref[0:lo, :])
            m = jnp.maximum(m, jnp.max(sl, axis=-1, keepdims=True))
        acc = jnp.dot(jnp.exp(sd - m).astype(BF16), vext_ref[lo:lo + tq, :], preferred_element_type=F32)
        if lo > 0:
            acc = acc + jnp.dot(jnp.exp(sl - m).astype(BF16), vext_ref[0:lo, :],
                                preferred_element_type=F32)
        return acc[:, :DA_V_DIM] / acc[:, DA_V_DIM:]

    for step in range(n_steps):
        @pl.when(i == step)
        def _(step=step):
            for sub in range(tiles_per_step):
                rows = slice(sub * tq, (sub + 1) * tq)
                lo = (step * tiles_per_step + sub) * tq
                q1, q2 = query_maps(rows)
                o = softmax_av(q1, lo) - lam * softmax_av(q2, lo)
                ms = jnp.mean(o * o, axis=-1, keepdims=True)
                o_ref[rows, :] = (o * lax.rsqrt(ms + EPS) * g_ref[...] * (1.0 - lam_init)).astype(o_ref.dtype)


def _da_attention(lam_vecs, proj3, sub_g, batch, seq, col_q, col_k, col_v, lam_init):
    d = DA_V_DIM
    tq = min(TQ_DA, seq)
    tps = min(DA_TILES_PER_STEP, seq // tq)
    rows = tq * tps
    return pl.pallas_call(
        functools.partial(_da_kernel, tq=tq, tiles_per_step=tps, n_steps=seq // rows, lam_init=lam_init),
        out_shape=jax.ShapeDtypeStruct((batch, seq, DA_HEADS * d), BF16),
        grid=(batch, DA_HEADS, seq // rows),
        in_specs=[
            pl.BlockSpec(lam_vecs.shape, lambda b, h, i: (0, 0)),
            pl.BlockSpec((None, rows, d), lambda b, h, i: (b, i, col_q + h)),
            pl.BlockSpec((None, seq, d), lambda b, h, i: (b, 0, col_k + h)),
            pl.BlockSpec((None, seq, d), lambda b, h, i: (b, 0, col_v + h)),
            pl.BlockSpec((1, d), lambda b, h, i: (0, 0)),
        ],
        out_specs=pl.BlockSpec((None, rows, d), lambda b, h, i: (b, i, h)),
        scratch_shapes=[pltpu.VMEM((seq, 2 * d), BF16)],
        compiler_params=_params(("parallel", "parallel", "arbitrary")),
        name="da_attn",
    )(lam_vecs, proj3, proj3, proj3, sub_g)


def _gate_branch_kernel(x_ref, gmix_ref, osb_ref, oda_ref, wgs_ref, wgd_ref, wbs_ref, wbd_ref, *rest, n_cast):
    cast_in, m_ref, cast_out, h_scr = rest[:n_cast], rest[n_cast], rest[n_cast + 1:2 * n_cast + 1], rest[-1]

    def column_tile(h):
        gate_sb = jax.nn.sigmoid(jnp.dot(h, wgs_ref[...], preferred_element_type=F32))
        gate_da = jax.nn.sigmoid(jnp.dot(h, wgd_ref[...], preferred_element_type=F32))
        br_sb = jnp.dot(osb_ref[...], wbs_ref[...], preferred_element_type=F32)
        br_da = jnp.dot(oda_ref[...], wbd_ref[...], preferred_element_type=F32)
        m_ref[...] = (gate_sb * br_sb + gate_da * br_da).astype(m_ref.dtype)
        for src, dst in zip(cast_in, cast_out):
            dst[...] = src[...].astype(dst.dtype)

    @pl.when(pl.program_id(1) == 0)
    def _():
        x = x_ref[...]
        ms = jnp.mean(x * x, axis=-1, keepdims=True)
        h = (x * lax.rsqrt(ms + EPS) * gmix_ref[...]).astype(BF16)
        h_scr[...] = h
        column_tile(h)

    @pl.when(pl.program_id(1) > 0)
    def _():
        column_tile(h_scr[...])


def _gate_branch(x2d, gmix, o_sb, o_da, w_in_bf16, gate_col0, wbs, wbd, to_cast):
    n, d = x2d.shape
    tm, tn = min(TM_GATE, n), TN_GATE
    wsb, wda = o_sb.shape[1], o_da.shape[1]
    sb0, da0 = gate_col0 // tn, (gate_col0 + d) // tn
    nj = d // tn
    n_steps = (n // tm) * nj
    cast_specs = []
    for a in to_cast:
        assert a.shape[0] % (n_steps * 16) == 0, "cast operand rows must split into whole bf16 tiles per step"
        cast_specs.append(pl.BlockSpec((a.shape[0] // n_steps, a.shape[1]), lambda i, j: (i * nj + j, 0)))
    return pl.pallas_call(
        functools.partial(_gate_branch_kernel, n_cast=len(to_cast)),
        out_shape=(jax.ShapeDtypeStruct((n, d), BF16),
                   *[jax.ShapeDtypeStruct(a.shape, BF16) for a in to_cast]),
        grid=(n // tm, nj),
        in_specs=[
            pl.BlockSpec((tm, d), lambda i, j: (i, 0)),
            pl.BlockSpec((1, d), lambda i, j: (0, 0)),
            pl.BlockSpec((tm, wsb), lambda i, j: (i, 0)),
            pl.BlockSpec((tm, wda), lambda i, j: (i, 0)),
            pl.BlockSpec((d, tn), lambda i, j: (0, sb0 + j)),
            pl.BlockSpec((d, tn), lambda i, j: (0, da0 + j)),
            pl.BlockSpec((wsb, tn), lambda i, j: (0, j)),
            pl.BlockSpec((wda, tn), lambda i, j: (0, j)),
            *cast_specs,
        ],
        out_specs=(pl.BlockSpec((tm, tn), lambda i, j: (i, j)), *cast_specs),
        scratch_shapes=[pltpu.VMEM((tm, d), BF16)],
        compiler_params=_params(("parallel", "arbitrary")),
        name="gate_branch",
    )(x2d, gmix, o_sb, o_da, w_in_bf16, w_in_bf16, wbs, wbd, *to_cast)


def _out_proj_kernel(m_ref, x_ref, wo_ref, gffn_ref, wr_hi_ref, wr_lo_ref, x1_ref, h2_ref, lt_ref):
    x1 = x_ref[...] + jnp.dot(m_ref[...], wo_ref[...], preferred_element_type=F32)
    x1_ref[...] = x1
    ms = jnp.mean(x1 * x1, axis=-1, keepdims=True)
    h2 = x1 * lax.rsqrt(ms + EPS) * gffn_ref[...]
    h2_ref[...] = h2
    h_hi = h2.astype(BF16)
    h_lo = (h2 - h_hi.astype(F32)).astype(BF16)
    lt_ref[...] = (_nt_dot(wr_hi_ref[...], h_hi) + _nt_dot(wr_lo_ref[...], h_hi)
                   + _nt_dot(wr_hi_ref[...], h_lo))


def _out_proj(merged, x2d, wo, gffn, wr_t):
    n, d = x2d.shape
    tm = min(TM_OUT, n)
    once = pl.Buffered(1)
    wr_hi = wr_t.astype(BF16)
    wr_lo = (wr_t - wr_hi.astype(F32)).astype(BF16)
    return pl.pallas_call(
        _out_proj_kernel,
        out_shape=(jax.ShapeDtypeStruct((n, d), F32),
                   jax.ShapeDtypeStruct((n, d), F32),
                   jax.ShapeDtypeStruct((ROUTER_ROWS, n), F32)),
        grid=(n // tm,),
        in_specs=[
            pl.BlockSpec((tm, d), lambda i: (i, 0)),
            pl.BlockSpec((tm, d), lambda i: (i, 0)),
            pl.BlockSpec((d, d), lambda i: (0, 0), pipeline_mode=once),
            pl.BlockSpec((1, d), lambda i: (0, 0), pipeline_mode=once),
            pl.BlockSpec((ROUTER_ROWS, d), lambda i: (0, 0), pipeline_mode=once),
            pl.BlockSpec((ROUTER_ROWS, d), lambda i: (0, 0), pipeline_mode=once),
        ],
        out_specs=(pl.BlockSpec((tm, d), lambda i: (i, 0)),
                   pl.BlockSpec((tm, d), lambda i: (i, 0)),
                   pl.BlockSpec((ROUTER_ROWS, tm), lambda i: (0, i))),
        compiler_params=_params(("parallel",)),
        name="out_proj",
    )(merged, x2d, wo, gffn, wr_hi, wr_lo)


def _route_kernel(lt_ref, ids_ref, rank_ref, gate_ref, cnt_ref, carry_ref):
    step = pl.program_id(0)

    @pl.when(step == 0)
    def _():
        carry_ref[...] = jnp.zeros_like(carry_ref)

    t = lt_ref.shape[1]
    el = lt_ref[0:N_EXPERTS, :]
    gl = lt_ref[N_EXPERTS:N_EXPERTS + 8, :]
    grow = lax.broadcasted_iota(jnp.int32, gl.shape, 0)
    gl = jnp.where(grow < N_GROUPS, gl, -jnp.inf)
    gmax = jnp.max(gl, axis=0, keepdims=True)
    grp = jnp.min(jnp.where(gl == gmax, grow, N_GROUPS), axis=0, keepdims=True)
    p_grp = 1.0 / jnp.sum(jnp.exp(gl - gmax), axis=0, keepdims=True)

    eidx = lax.broadcasted_iota(jnp.int32, el.shape, 0)
    in_grp = (eidx // EXPERTS_PER_GROUP) == grp
    e1 = jnp.where(in_grp, el, -jnp.inf)
    top1 = jnp.max(e1, axis=0, keepdims=True)
    idx1 = jnp.min(jnp.where(e1 == top1, eidx, N_EXPERTS), axis=0, keepdims=True)
    e2 = jnp.where(eidx == idx1, -jnp.inf, e1)
    top2 = jnp.max(e2, axis=0, keepdims=True)
    idx2 = jnp.min(jnp.where(e2 == top2, eidx, N_EXPERTS), axis=0, keepdims=True)
    dlt = jnp.exp(top2 - top1)
    den = 1.0 + dlt
    gate_ref[0:1, :] = p_grp / den
    gate_ref[1:2, :] = p_grp * dlt / den
    ids_ref[0:1, :] = idx1
    ids_ref[1:2, :] = idx2

    oh1 = jnp.where(eidx == idx1, 1.0, 0.0)
    oh2 = jnp.where(eidx == idx2, 1.0, 0.0)
    rr = lax.broadcasted_iota(jnp.int32, (t, t), 0)
    cc = lax.broadcasted_iota(jnp.int32, (t, t), 1)
    before = jnp.where(rr < cc, 1.0, 0.0).astype(BF16)
    pre1 = jnp.dot(oh1.astype(BF16), before, preferred_element_type=F32)
    pre2 = jnp.dot(oh2.astype(BF16), before, preferred_element_type=F32)
    cnt1 = jnp.sum(oh1, axis=1, keepdims=True)
    cnt2 = jnp.sum(oh2, axis=1, keepdims=True)
    carry = carry_ref[:, 0:1]
    rank1 = jnp.sum(oh1 * (carry + pre1), axis=0, keepdims=True)
    rank2 = jnp.sum(oh2 * (carry + cnt1 + pre2), axis=0, keepdims=True)
    rank_ref[0:1, :] = rank1.astype(jnp.int32)
    rank_ref[1:2, :] = rank2.astype(jnp.int32)
    new_carry = carry_ref[...] + cnt1 + cnt2
    carry_ref[...] = new_carry
    cnt_ref[...] = new_carry


def _route(logits_t):
    n = logits_t.shape[1]
    t = min(T_ROUTE, n)
    return pl.pallas_call(
        _route_kernel,
        out_shape=(jax.ShapeDtypeStruct((2, n), jnp.int32),
                   jax.ShapeDtypeStruct((2, n), jnp.int32),
                   jax.ShapeDtypeStruct((2, n), F32),
                   jax.ShapeDtypeStruct((N_EXPERTS, LANES), F32)),
        grid=(n // t,),
        in_specs=[pl.BlockSpec((ROUTER_ROWS, t), lambda i: (0, i))],
        out_specs=(pl.BlockSpec((2, t), lambda i: (0, i)),
                   pl.BlockSpec((2, t), lambda i: (0, i)),
                   pl.BlockSpec((2, t), lambda i: (0, i)),
                   pl.BlockSpec((N_EXPERTS, LANES), lambda i: (0, 0))),
        scratch_shapes=[pltpu.VMEM((N_EXPERTS, LANES), F32)],
        compiler_params=_params(("arbitrary",)),
        name="route",
    )(logits_t)


def _dispatch_kernel(zs_ref, zc_ref, nu_ref, dest_ref, h2_ref, xb_hbm, zeros_ref, sem, tail_sem, *, tm, te,
                     n_tiles):
    step = pl.program_id(0)

    def tail_copy(t):
        return pltpu.make_async_copy(zeros_ref, xb_hbm.at[pl.ds(pl.multiple_of(t * te, te), te)], tail_sem)

    @pl.when(step == 0)
    def _():
        zeros_ref[...] = jnp.zeros_like(zeros_ref)

        def pad_copy(row):
            return pltpu.make_async_copy(zeros_ref.at[pl.ds(0, SUBLANES)],
                                         xb_hbm.at[pl.ds(pl.multiple_of(row, SUBLANES), SUBLANES)], sem)

        total = 0
        for e in range(N_EXPERTS):
            lax.fori_loop(0, zc_ref[e], lambda c, _, e=e: (pad_copy(zs_ref[e] + c * SUBLANES).start(), 0)[1], 0)
            total = total + zc_ref[e]
        lax.fori_loop(0, total, lambda c, _: (pad_copy(0).wait(), 0)[1], 0)
        lax.fori_loop(nu_ref[0], n_tiles, lambda t, c: (tail_copy(t).start(), c)[1], 0)

    def issue(blk, _):
        r0 = pl.multiple_of(blk * SUBLANES, SUBLANES)
        for u in range(SUBLANES):
            src = h2_ref.at[pl.ds(r0 + u, 1)]
            for k in range(2):
                pltpu.make_async_copy(src, xb_hbm.at[pl.ds(dest_ref[0, 0, k * tm + r0 + u], 1)],
                                      sem).start(priority=k)
        return 0

    lax.fori_loop(0, tm // SUBLANES, issue, 0)
    for _ in range(2):
        pltpu.make_async_copy(h2_ref, xb_hbm.at[pl.ds(0, tm)], sem).wait()

    @pl.when(step == pl.num_programs(0) - 1)
    def _():
        lax.fori_loop(nu_ref[0], n_tiles, lambda t, c: (tail_copy(t).wait(), c)[1], 0)


def _dispatch(zero_starts, zero_chunks, n_used, dest_tiles, h2, n_tiles, tm, te):
    n, d = h2.shape
    return pl.pallas_call(
        functools.partial(_dispatch_kernel, tm=tm, te=te, n_tiles=n_tiles),
        out_shape=jax.ShapeDtypeStruct((n_tiles * te, d), h2.dtype),
        grid_spec=pltpu.PrefetchScalarGridSpec(
            num_scalar_prefetch=3,
            grid=(n // tm,),
            in_specs=[
                pl.BlockSpec((1, 1, 2 * tm), lambda i, *_: (i, 0, 0), memory_space=pltpu.SMEM),
                pl.BlockSpec((tm, d), lambda i, *_: (i, 0)),
            ],
            out_specs=pl.BlockSpec(memory_space=pl.ANY),
            scratch_shapes=[pltpu.VMEM((te, d), h2.dtype), pltpu.SemaphoreType.DMA,
                            pltpu.SemaphoreType.DMA],
        ),
        compiler_params=_params(("arbitrary",)),
        name="dispatch",
    )(zero_starts, zero_chunks, n_used, dest_tiles, h2)


def _expert_kernel(te_ref, nu_ref, slot_ref, next_ref, xb_hbm, wg_hbm, wu_hbm, wd_hbm, yb_ref,
                   wg_s, wu_s, wd_s, sems, xb_buf, xb_sems, *, te):
    t = pl.program_id(0)
    used = t < nu_ref[0]
    new_run = jnp.logical_or(t == 0, te_ref[t] != te_ref[jnp.maximum(t - 1, 0)])

    def weight_copies(expert, slot):
        return [pltpu.make_async_copy(src.at[expert], dst.at[slot], sems.at[slot, i])
                for i, (src, dst) in enumerate(((wg_hbm, wg_s), (wu_hbm, wu_s), (wd_hbm, wd_s)))]

    @pl.when(t == 0)
    def _():
        for c in weight_copies(te_ref[0], 0):
            c.start()

    @pl.when(jnp.logical_and(used, new_run))
    def _():
        slot = slot_ref[t]
        for c in weight_copies(te_ref[t], slot):
            c.wait()

        @pl.when(next_ref[t] >= 0)
        def _():
            for c in weight_copies(next_ref[t], 1 - slot):
                c.start()

    def row_copy(tile):
        rows = pl.ds(pl.multiple_of(tile * te, te), te)
        slot = tile % XB_SLOTS
        return pltpu.make_async_copy(xb_hbm.at[rows], xb_buf.at[slot], xb_sems.at[slot])

    @pl.when(t == 0)
    def _():
        for ahead in range(XB_SLOTS - 1):
            @pl.when(ahead < nu_ref[0])
            def _(ahead=ahead):
                row_copy(ahead).start()

    @pl.when(used)
    def _():
        @pl.when(t + (XB_SLOTS - 1) < nu_ref[0])
        def _():
            row_copy(t + (XB_SLOTS - 1)).start()

        row_copy(t).wait()
        x = xb_buf[t % XB_SLOTS].astype(BF16)
        slot = slot_ref[t]
        g = jnp.dot(x, wg_s[slot], preferred_element_type=F32)
        u = jnp.dot(x, wu_s[slot], preferred_element_type=F32)
        hid = (g * jax.nn.sigmoid(g)) * u
        yb_ref[...] = jnp.dot(hid.astype(BF16), wd_s[slot], preferred_element_type=F32)

    @pl.when(jnp.logical_not(used))
    def _():
        yb_ref[...] = jnp.zeros_like(yb_ref)


def _experts(tile_expert, n_used, run_slot, next_expert, xb, wg, wu, wd, n_tiles, te):
    d = xb.shape[1]
    de = wg.shape[2]

    hbm = pl.BlockSpec(memory_space=pl.ANY)
    return pl.pallas_call(
        functools.partial(_expert_kernel, te=te),
        out_shape=jax.ShapeDtypeStruct((n_tiles * te, d), F32),
        grid_spec=pltpu.PrefetchScalarGridSpec(
            num_scalar_prefetch=4,
            grid=(n_tiles,),
            in_specs=[hbm, hbm, hbm, hbm],
            out_specs=pl.BlockSpec((te, d), lambda t, *_: (t, 0)),
            scratch_shapes=[pltpu.VMEM((2, d, de), BF16), pltpu.VMEM((2, d, de), BF16), pltpu.VMEM((2, de, d), BF16),
                            pltpu.SemaphoreType.DMA((2, 3)),
                            pltpu.VMEM((XB_SLOTS, te, d), xb.dtype), pltpu.SemaphoreType.DMA((XB_SLOTS,))],
        ),
        compiler_params=_params(("arbitrary",)),
        name="experts",
    )(tile_expert, n_used, run_slot, next_expert, xb, wg, wu, wd)


def _combine_kernel(dest_ref, next_dest_ref, x1_ref, gate_ref, g_ref, yb_hbm, o_ref, buf, sems, *, tm,
                    final_norm):
    i = pl.program_id(0)
    slot = i % 2

    def row_copy(idx_ref, slot, row, k):
        return pltpu.make_async_copy(yb_hbm.at[pl.ds(idx_ref[0, 0, row], 1)], buf.at[slot, pl.ds(row, 1)],
                                     sems.at[slot])

    def tile_wait(slot):
        pltpu.make_async_copy(yb_hbm.at[pl.ds(0, 2 * tm)], buf.at[slot], sems.at[slot]).wait()

    @pl.when(i == 0)
    def _():
        def issue(r, _):
            for k in range(2):
                row_copy(dest_ref, 0, k * tm + r, k).start(priority=k)
            return 0
        lax.fori_loop(0, tm, issue, 0, unroll=8)

    tile_wait(slot)
    for r in range(tm):
        for k in range(2):
            row_copy(next_dest_ref, 1 - slot, k * tm + r, k).start(priority=k)
    gates = gate_ref[...]
    g0, g1 = gates[:, 0:1], gates[:, 1:2]
    x = x1_ref[...] + (buf[slot, 0:tm, :] * g0 + buf[slot, tm:2 * tm, :] * g1)
    if final_norm:
        ms = jnp.mean(x * x, axis=-1, keepdims=True)
        x = x * lax.rsqrt(ms + EPS) * g_ref[...]
    o_ref[...] = x

    @pl.when(i == pl.num_programs(0) - 1)
    def _():
        tile_wait(1 - slot)


def _combine(dest_tiles, x1, gates, g_final, yb, tm, final_norm):
    n, d = x1.shape
    n_steps = n // tm
    return pl.pallas_call(
        functools.partial(_combine_kernel, tm=tm, final_norm=final_norm),
        out_shape=jax.ShapeDtypeStruct((n, d), F32),
        grid=(n_steps,),
        in_specs=[
            pl.BlockSpec((1, 1, 2 * tm), lambda i: (i, 0, 0), memory_space=pltpu.SMEM),
            pl.BlockSpec((1, 1, 2 * tm), lambda i: (jnp.minimum(i + 1, n_steps - 1), 0, 0),
                         memory_space=pltpu.SMEM),
            pl.BlockSpec((tm, d), lambda i: (i, 0)),
            pl.BlockSpec((tm, 2), lambda i: (i, 0)),
            pl.BlockSpec((1, d), lambda i: (0, 0)),
            pl.BlockSpec(memory_space=pl.ANY),
        ],
        out_specs=pl.BlockSpec((tm, d), lambda i: (i, 0)),
        scratch_shapes=[pltpu.VMEM((2, 2 * tm, d), F32), pltpu.SemaphoreType.DMA((2,))],
        compiler_params=_params(("arbitrary",)),
        name="combine",
    )(dest_tiles, dest_tiles, x1, gates, g_final, yb)


def _rope_tables(seq):
    half = DA_HEAD_DIM // 2
    inv_freq = ROPE_THETA ** (-(jnp.arange(half, dtype=F32) * 2.0 / DA_HEAD_DIM))
    ang = jnp.arange(seq, dtype=F32)[:, None] * inv_freq[None, :]
    cos, sin = jnp.cos(ang), jnp.sin(ang)
    cos_tab = jnp.tile(cos, (1, LANES // half))
    sin_tab = jnp.tile(jnp.concatenate([-sin, sin], axis=1), (1, LANES // DA_HEAD_DIM))
    return cos_tab, sin_tab


def _tile_indices(dest, tm):
    n = dest.shape[1]
    return dest.reshape(2, n // tm, tm).transpose(1, 0, 2).reshape(n // tm, 1, 2 * tm)


def _layer(x2d, batch, seq, layer, mix_g, w_in, lam_vecs, sub_g, w_bsb, w_bda, w_out, ffn_g,
           w_grp, w_exp, w_gate, w_up, w_down):
    n, d = x2d.shape
    sbw = SB_HEADS * SB_HEAD_DIM
    daw = DA_HEADS * DA_V_DIM
    qkv_cols = 3 * sbw + 3 * daw
    lam_init = 0.8 - 0.6 * math.exp(-0.3 * layer)

    w_in_bf16 = w_in.astype(BF16)
    cos_tab, sin_tab = _rope_tables(seq)
    rope_lo = 3 * sbw // TN_IN
    rope_hi = (3 * sbw + 2 * daw) // TN_IN
    proj = _in_proj(x2d, mix_g[None, :], w_in_bf16, qkv_cols, cos_tab, sin_tab, seq, rope_lo, rope_hi)
    proj3 = proj.reshape(batch, seq, qkv_cols)
    nsb = sbw // SB_HEAD_DIM
    nda = daw // DA_V_DIM
    o_sb = _sb_attention(proj3, batch, seq, 0, nsb, 2 * nsb)
    o_da = _da_attention(lam_vecs, proj3, sub_g[None, :], batch, seq,
                         3 * nsb, 3 * nsb + nda, 3 * nsb + 2 * nda, lam_init)

    wr_t = jnp.zeros((ROUTER_ROWS, d), F32)
    wr_t = wr_t.at[:N_EXPERTS].set(w_exp.T).at[N_EXPERTS:N_EXPERTS + N_GROUPS].set(w_grp.T)
    n_exp, _, d_exp = w_gate.shape
    merged, wg_bf16, wu_bf16, wd_bf16 = _gate_branch(
        x2d, mix_g[None, :], o_sb.reshape(n, sbw), o_da.reshape(n, daw), w_in_bf16, qkv_cols,
        w_bsb.astype(BF16), w_bda.astype(BF16),
        (w_gate.reshape(n_exp * d, d_exp), w_up.reshape(n_exp * d, d_exp), w_down.reshape(n_exp * d_exp, d)))
    x1, h2, logits_t = _out_proj(merged, x2d, w_out.astype(BF16), ffn_g[None, :], wr_t)

    ids, ranks, gates, counts = _route(logits_t)
    te = TM_EXPERT
    counts = counts[:, 0].astype(jnp.int32)
    padded = (counts + te - 1) // te * te
    pad_ends = jnp.cumsum(padded)
    pad_starts = pad_ends - padded
    expert_ids = jnp.arange(N_EXPERTS, dtype=jnp.int32)
    dest = ranks + jnp.sum(jnp.where(ids[..., None] == expert_ids, pad_starts, 0), axis=-1)
    n_tiles = (2 * n) // te + N_EXPERTS
    n_used = (pad_ends[-1] // te).astype(jnp.int32)[None]
    tile_row0 = jnp.arange(n_tiles, dtype=jnp.int32) * te
    tile_expert = jnp.minimum(jnp.sum((pad_ends[None, :] <= tile_row0[:, None]).astype(jnp.int32), axis=1),
                              N_EXPERTS - 1)
    zero_starts = ((pad_starts + counts) // SUBLANES * SUBLANES).astype(jnp.int32)
    zero_chunks = ((pad_ends - zero_starts) // SUBLANES).astype(jnp.int32)

    tmd = min(TM_DISPATCH, n)
    xb = _dispatch(zero_starts, zero_chunks, n_used, _tile_indices(dest, tmd), h2, n_tiles, tmd, te)
    tile_ids = jnp.arange(n_tiles, dtype=jnp.int32)
    new_run = jnp.logical_or(tile_ids == 0, tile_expert != jnp.roll(tile_expert, 1))
    run_slot = ((jnp.cumsum(new_run.astype(jnp.int32)) - 1) % 2).astype(jnp.int32)
    later = jnp.logical_and(tile_ids[None, :] < n_used, tile_expert[None, :] > tile_expert[:, None])
    next_expert = jnp.min(jnp.where(later, tile_expert[None, :], N_EXPERTS), axis=1)
    next_expert = jnp.where(next_expert == N_EXPERTS, -1, next_expert).astype(jnp.int32)
    yb = _experts(tile_expert, n_used, run_slot, next_expert, xb, wg_bf16.reshape(w_gate.shape),
                  wu_bf16.reshape(w_up.shape), wd_bf16.reshape(w_down.shape), n_tiles, te)
    return x1, gates, dest, yb


def kernel(x, mix_norm_g, w_in, lambda_q1, lambda_k1, lambda_q2, lambda_k2, diff_subnorm_g, w_branch_sb, w_branch_diff, w_out, ffn_norm_g, w_group_router, w_expert_router, w_gate, w_up, w_down, final_norm_g):
    batch, seq, d = x.shape
    depth = w_in.shape[0]
    n = batch * seq
    x2d = x.reshape(n, d)
    tmc = min(TM_COMBINE, n)
    for l in range(depth):
        lam_vecs = jnp.stack([lambda_q1[l], lambda_k1[l], lambda_q2[l], lambda_k2[l]]).astype(F32)
        x1, gates, dest, yb = _layer(
            x2d, batch, seq, l, mix_norm_g[l], w_in[l], lam_vecs, diff_subnorm_g[l], w_branch_sb[l],
            w_branch_diff[l], w_out[l], ffn_norm_g[l], w_group_router[l], w_expert_router[l],
            w_gate[l], w_up[l], w_down[l])
        x2d = _combine(_tile_indices(dest, tmc), x1, gates.T, final_norm_g[None, :], yb, tmc,
                       final_norm=(l == depth - 1))
    return x2d.reshape(batch, seq, d)
```

```python
import functools
import math

import jax
import jax.numpy as jnp
from jax import lax
from jax.experimental import pallas as pl
from jax.experimental.pallas import tpu as pltpu

F32 = jnp.float32
BF16 = jnp.bfloat16

CHUNK = 64
SB_HEADS = 8
SB_HEAD_DIM = 128
DA_HEADS = 8
DA_HEAD_DIM = 64
DA_V_DIM = 2 * DA_HEAD_DIM
N_GROUPS = 4
EXPERTS_PER_GROUP = 8
N_EXPERTS = N_GROUPS * EXPERTS_PER_GROUP
ROPE_THETA = 10000.0
EPS = 1e-6

LANES = 128
SUBLANES = 8
ROUTER_ROWS = 40
VMEM_LIMIT_BYTES = 56 * 1024 * 1024

SB_LOG_WEIGHT_FLOOR = -104.0
SB_MASKED_LOGIT = -1e30
LOG2_E = 1.4426950408889634

TM_IN = 1024
TN_IN = 1024
TQ_SB = 256
TK_SB = 128
SB_TILES_PER_STEP = 4
SB_NEAR_PAST = 256
TQ_DA = 256
DA_TILES_PER_STEP = 4
TM_GATE = 1024
TN_GATE = 256
TM_OUT = 512
T_ROUTE = 512
TM_DISPATCH = 1024
TM_EXPERT = 256
XB_SLOTS = 3
TM_COMBINE = 256


def _params(sem, vmem=VMEM_LIMIT_BYTES):
    return pltpu.CompilerParams(dimension_semantics=sem, vmem_limit_bytes=vmem)


def _nt_dot(a, b, **kw):
    return lax.dot_general(a, b, (((1,), (1,)), ((), ())), preferred_element_type=F32, **kw)


def _in_proj_kernel(x_ref, g_ref, w_ref, cos_ref, sin_ref, o_ref, h_ref, *, rope_lo, rope_hi):
    j = pl.program_id(1)

    @pl.when(j == 0)
    def _():
        x = x_ref[...]
        ms = jnp.mean(x * x, axis=-1, keepdims=True)
        h = (x * lax.rsqrt(ms + EPS) * g_ref[...]).astype(BF16)
        h_ref[...] = h
        o_ref[...] = jnp.dot(h, w_ref[...], preferred_element_type=F32).astype(BF16)

    is_rope = jnp.logical_and(j >= rope_lo, j < rope_hi)

    @pl.when(is_rope)
    def _():
        y = jnp.dot(h_ref[...], w_ref[...], preferred_element_type=F32)
        cos = cos_ref[...]
        sin = sin_ref[...]
        lane = lax.broadcasted_iota(jnp.int32, cos.shape, 1)
        first_half = (lane % DA_HEAD_DIM) < (DA_HEAD_DIM // 2)
        for hh in range(y.shape[1] // LANES):
            yh = y[:, hh * LANES:(hh + 1) * LANES]
            partner = jnp.where(first_half,
                                pltpu.roll(yh, LANES - DA_HEAD_DIM // 2, 1),
                                pltpu.roll(yh, DA_HEAD_DIM // 2, 1))
            o_ref[:, hh * LANES:(hh + 1) * LANES] = (yh * cos + partner * sin).astype(BF16)

    @pl.when(jnp.logical_and(j > 0, jnp.logical_not(is_rope)))
    def _():
        o_ref[...] = jnp.dot(h_ref[...], w_ref[...], preferred_element_type=F32).astype(BF16)


def _in_proj(x2d, g, w_bf16, cols, cos_tab, sin_tab, seq, rope_lo, rope_hi):
    n, d = x2d.shape
    tm, tn = min(TM_IN, seq), TN_IN
    pos_tiles = seq // tm
    assert 0 < rope_lo <= rope_hi, "the first column tile carries the RMSNorm and must not be a rotary tile"
    return pl.pallas_call(
        functools.partial(_in_proj_kernel, rope_lo=rope_lo, rope_hi=rope_hi),
        out_shape=jax.ShapeDtypeStruct((n, cols), BF16),
        grid=(n // tm, cols // tn),
        in_specs=[
            pl.BlockSpec((tm, d), lambda i, j: (i, 0)),
            pl.BlockSpec((1, d), lambda i, j: (0, 0)),
            pl.BlockSpec((d, tn), lambda i, j: (0, j)),
            pl.BlockSpec((tm, LANES), lambda i, j: (i % pos_tiles, 0)),
            pl.BlockSpec((tm, LANES), lambda i, j: (i % pos_tiles, 0)),
        ],
        out_specs=pl.BlockSpec((tm, tn), lambda i, j: (i, j)),
        scratch_shapes=[pltpu.VMEM((tm, d), BF16)],
        compiler_params=_params(("parallel", "arbitrary")),
        name="in_proj",
    )(x2d, g, w_bf16, cos_tab, sin_tab)


def _sb_kernel(q_ref, k_ref, v_ref, o_ref, *, tq, tk, tiles_per_step, near_past, scale):
    i = pl.program_id(2)
    row = lax.broadcasted_iota(jnp.int32, (2 * tk, 2 * tk), 0) % tk
    col = lax.broadcasted_iota(jnp.int32, (2 * tk, 2 * tk), 1)
    u2 = jnp.where(jnp.logical_or(col >= tk, row > col), 1.0, 0.0).astype(BF16)

    def span(q, start, width, diag_cols, running):
        k = k_ref[pl.ds(start, width), :]
        v = v_ref[pl.ds(start, width), :]
        z = _nt_dot(q, k) * scale
        if diag_cols is not None:
            r = lax.broadcasted_iota(jnp.int32, z.shape, 0)
            c = lax.broadcasted_iota(jnp.int32, z.shape, 1)
            z = jnp.where((c - diag_cols) < r, z, SB_MASKED_LOGIT)
        sp = jnp.maximum(z, 0.0) + jnp.log(1.0 + jnp.exp2(jnp.abs(z) * (-LOG2_E)))
        hi = sp.astype(BF16)
        lo = (sp - hi.astype(F32)).astype(BF16)
        nb = width // tk
        suffix = [None] * nb
        for blk in range(nb - 1, -1, -1):
            sl = slice(blk * tk, (blk + 1) * tk)
            cs = jnp.dot(jnp.concatenate([hi[:, sl], lo[:, sl]], axis=1), u2, preferred_element_type=F32)
            suffix[blk] = cs[:, :tk] + running
            running = running + cs[:, tk:]
        a = jnp.exp((z - sp) - jnp.concatenate(suffix, axis=1))
        return jnp.dot(a.astype(BF16), v, preferred_element_type=F32), running

    zero_run = jnp.zeros((tq, tk), F32)

    def far_keys(q, first, running, acc):
        def alive_after(running):
            return jnp.min(running) < -SB_LOG_WEIGHT_FLOOR

        def cond(c):
            j, alive, _, _ = c
            return jnp.logical_and(j >= 0, alive)

        def body(c):
            j, _, running, acc = c
            out, running = span(q, pl.multiple_of(j * tk, tk), tk, None, running)
            return j - 1, alive_after(running), running, acc + out

        return lax.while_loop(cond, body, (first, alive_after(running), running, acc))[3]

    def tiles(first_row, at_start):
        near, starts = [], []
        for sub in range(tiles_per_step):
            q = q_ref[sub * tq:(sub + 1) * tq, :]
            t0 = first_row + sub * tq
            past = min(near_past, t0) if at_start else near_past
            start = t0 - past
            if not at_start:
                start = pl.multiple_of(start, tk)
            near.append(span(q, start, tq + past, past, zero_run))
            starts.append(start)
        for sub in range(tiles_per_step):
            acc, running = near[sub]
            if not (at_start and starts[sub] == 0):
                acc = far_keys(q_ref[sub * tq:(sub + 1) * tq, :], starts[sub] // tk - 1, running, acc)
            o_ref[sub * tq:(sub + 1) * tq, :] = acc.astype(o_ref.dtype)

    @pl.when(i == 0)
    def _():
        tiles(0, True)

    @pl.when(i > 0)
    def _():
        tiles(i * (tiles_per_step * tq), False)


def _sb_attention(proj3, batch, seq, col_q, col_k, col_v):
    d = SB_HEAD_DIM
    tq, tk = min(TQ_SB, seq), min(TK_SB, seq)
    tps = min(SB_TILES_PER_STEP, seq // tq)
    rows = tq * tps
    assert rows == seq or rows >= SB_NEAR_PAST, "later steps assume a full near span of past keys"
    return pl.pallas_call(
        functools.partial(_sb_kernel, tq=tq, tk=tk, tiles_per_step=tps, near_past=SB_NEAR_PAST,
                          scale=1.0 / math.sqrt(d)),
        out_shape=jax.ShapeDtypeStruct((batch, seq, SB_HEADS * d), BF16),
        grid=(batch, SB_HEADS, seq // rows),
        in_specs=[
            pl.BlockSpec((None, rows, d), lambda b, h, i: (b, i, col_q + h)),
            pl.BlockSpec((None, seq, d), lambda b, h, i: (b, 0, col_k + h)),
            pl.BlockSpec((None, seq, d), lambda b, h, i: (b, 0, col_v + h)),
        ],
        out_specs=pl.BlockSpec((None, rows, d), lambda b, h, i: (b, i, h)),
        compiler_params=_params(("parallel", "parallel", "arbitrary")),
        name="sb_attn",
    )(proj3, proj3, proj3)


def _da_kernel(lam_ref, q_ref, k_ref, v_ref, g_ref, *rest, tq, tiles_per_step, n_steps, lam_init, n_cast):
    cast_in, o_ref, cast_out, vext_ref = rest[:n_cast], rest[n_cast], rest[n_cast + 1:2 * n_cast + 1], rest[-1]
    i = pl.program_id(2)

    @pl.when(i == 0)
    def _():
        vext_ref[:, :DA_V_DIM] = v_ref[...]
        vext_ref[:, DA_V_DIM:] = jnp.ones((vext_ref.shape[0], DA_V_DIM), BF16)

    lv = lam_ref[...]
    lam = (jnp.exp(jnp.sum(lv[0:1] * lv[1:2], axis=-1, keepdims=True))
           - jnp.exp(jnp.sum(lv[2:3] * lv[3:4], axis=-1, keepdims=True)) + lam_init)

    lane = lax.broadcasted_iota(jnp.int32, (tq, q_ref.shape[1]), 1)
    r = lax.broadcasted_iota(jnp.int32, (tq, tq), 0)
    c = lax.broadcasted_iota(jnp.int32, (tq, tq), 1)
    diag_mask = (c // CHUNK) <= (r // CHUNK)

    def query_maps(rows):
        qs = q_ref[rows, :] * jnp.asarray(1.0 / math.sqrt(DA_HEAD_DIM), BF16)
        zero = jnp.zeros_like(qs)
        return jnp.where(lane < DA_HEAD_DIM, qs, zero), jnp.where(lane >= DA_HEAD_DIM, qs, zero)

    def softmax_av(qm, lo):
        sd = jnp.where(diag_mask, _nt_dot(qm, k_ref[lo:lo + tq, :]), -jnp.inf)
        m = jnp.max(sd, axis=-1, keepdims=True)
        if lo > 0:
            sl = _nt_dot(qm, k_ref[0:lo, :])
            m = jnp.maximum(m, jnp.max(sl, axis=-1, keepdims=True))
        acc = jnp.dot(jnp.exp(sd - m).astype(BF16), vext_ref[lo:lo + tq, :], preferred_element_type=F32)
        if lo > 0:
            acc = acc + jnp.dot(jnp.exp(sl - m).astype(BF16), vext_ref[0:lo, :],
                                preferred_element_type=F32)
        return acc[:, :DA_V_DIM] / acc[:, DA_V_DIM:]

    for step in range(n_steps):
        @pl.when(i == step)
        def _(step=step):
            for sub in range(tiles_per_step):
                rows = slice(sub * tq, (sub + 1) * tq)
                lo = (step * tiles_per_step + sub) * tq
                q1, q2 = query_maps(rows)
                o = softmax_av(q1, lo) - lam * softmax_av(q2, lo)
                ms = jnp.mean(o * o, axis=-1, keepdims=True)
                o_ref[rows, :] = (o * lax.rsqrt(ms + EPS) * g_ref[...] * (1.0 - lam_init)).astype(o_ref.dtype)
            for src, dst in zip(cast_in, cast_out):
                dst[...] = src[...].astype(dst.dtype)


def _da_attention(lam_vecs, proj3, sub_g, batch, seq, col_q, col_k, col_v, lam_init, to_cast):
    d = DA_V_DIM
    tq = min(TQ_DA, seq)
    tps = min(DA_TILES_PER_STEP, seq // tq)
    rows = tq * tps
    n_i = seq // rows
    n_steps = batch * DA_HEADS * n_i
    cast_specs = []
    for a in to_cast:
        assert a.shape[0] % (n_steps * 16) == 0, "cast operand rows must split into whole bf16 tiles per step"
        cast_specs.append(pl.BlockSpec((a.shape[0] // n_steps, a.shape[1]),
                                       lambda b, h, i: ((b * DA_HEADS + h) * n_i + i, 0)))
    return pl.pallas_call(
        functools.partial(_da_kernel, tq=tq, tiles_per_step=tps, n_steps=n_i, lam_init=lam_init,
                          n_cast=len(to_cast)),
        out_shape=(jax.ShapeDtypeStruct((batch, seq, DA_HEADS * d), BF16),
                   *[jax.ShapeDtypeStruct(a.shape, BF16) for a in to_cast]),
        grid=(batch, DA_HEADS, n_i),
        in_specs=[
            pl.BlockSpec(lam_vecs.shape, lambda b, h, i: (0, 0)),
            pl.BlockSpec((None, rows, d), lambda b, h, i: (b, i, col_q + h)),
            pl.BlockSpec((None, seq, d), lambda b, h, i: (b, 0, col_k + h)),
            pl.BlockSpec((None, seq, d), lambda b, h, i: (b, 0, col_v + h)),
            pl.BlockSpec((1, d), lambda b, h, i: (0, 0)),
            *cast_specs,
        ],
        out_specs=(pl.BlockSpec((None, rows, d), lambda b, h, i: (b, i, h)), *cast_specs),
        scratch_shapes=[pltpu.VMEM((seq, 2 * d), BF16)],
        compiler_params=_params(("parallel", "parallel", "arbitrary")),
        name="da_attn",
    )(lam_vecs, proj3, proj3, proj3, sub_g, *to_cast)


def _gate_branch_kernel(x_ref, gmix_ref, osb_ref, oda_ref, wgs_ref, wgd_ref, wbs_ref, wbd_ref, *rest, n_cast):
    cast_in, m_ref, cast_out, h_scr = rest[:n_cast], rest[n_cast], rest[n_cast + 1:2 * n_cast + 1], rest[-1]

    def column_tile(h):
        gate_sb = jax.nn.sigmoid(jnp.dot(h, wgs_ref[...], preferred_element_type=F32))
        gate_da = jax.nn.sigmoid(jnp.dot(h, wgd_ref[...], preferred_element_type=F32))
        br_sb = jnp.dot(osb_ref[...], wbs_ref[...], preferred_element_type=F32)
        br_da = jnp.dot(oda_ref[...], wbd_ref[...], preferred_element_type=F32)
        m_ref[...] = (gate_sb * br_sb + gate_da * br_da).astype(m_ref.dtype)
        for src, dst in zip(cast_in, cast_out):
            dst[...] = src[...].astype(dst.dtype)

    @pl.when(pl.program_id(1) == 0)
    def _():
        x = x_ref[...]
        ms = jnp.mean(x * x, axis=-1, keepdims=True)
        h = (x * lax.rsqrt(ms + EPS) * gmix_ref[...]).astype(BF16)
        h_scr[...] = h
        column_tile(h)

    @pl.when(pl.program_id(1) > 0)
    def _():
        column_tile(h_scr[...])


def _gate_branch(x2d, gmix, o_sb, o_da, w_in_bf16, gate_col0, wbs, wbd, to_cast):
    n, d = x2d.shape
    tm, tn = min(TM_GATE, n), TN_GATE
    wsb, wda = o_sb.shape[1], o_da.shape[1]
    sb0, da0 = gate_col0 // tn, (gate_col0 + d) // tn
    nj = d // tn
    n_steps = (n // tm) * nj
    cast_specs = []
    for a in to_cast:
        assert a.shape[0] % (n_steps * 16) == 0, "cast operand rows must split into whole bf16 tiles per step"
        cast_specs.append(pl.BlockSpec((a.shape[0] // n_steps, a.shape[1]), lambda i, j: (i * nj + j, 0)))
    return pl.pallas_call(
        functools.partial(_gate_branch_kernel, n_cast=len(to_cast)),
        out_shape=(jax.ShapeDtypeStruct((n, d), BF16),
                   *[jax.ShapeDtypeStruct(a.shape, BF16) for a in to_cast]),
        grid=(n // tm, nj),
        in_specs=[
            pl.BlockSpec((tm, d), lambda i, j: (i, 0)),
            pl.BlockSpec((1, d), lambda i, j: (0, 0)),
            pl.BlockSpec((tm, wsb), lambda i, j: (i, 0)),
            pl.BlockSpec((tm, wda), lambda i, j: (i, 0)),
            pl.BlockSpec((d, tn), lambda i, j: (0, sb0 + j)),
            pl.BlockSpec((d, tn), lambda i, j: (0, da0 + j)),
            pl.BlockSpec((wsb, tn), lambda i, j: (0, j)),
            pl.BlockSpec((wda, tn), lambda i, j: (0, j)),
            *cast_specs,
        ],
        out_specs=(pl.BlockSpec((tm, tn), lambda i, j: (i, j)), *cast_specs),
        scratch_shapes=[pltpu.VMEM((tm, d), BF16)],
        compiler_params=_params(("parallel", "arbitrary")),
        name="gate_branch",
    )(x2d, gmix, o_sb, o_da, w_in_bf16, w_in_bf16, wbs, wbd, *to_cast)


def _out_proj_kernel(m_ref, x_ref, wo_ref, gffn_ref, wr_hi_ref, wr_lo_ref, x1_ref, h2_ref, lt_ref):
    x1 = x_ref[...] + jnp.dot(m_ref[...], wo_ref[...], preferred_element_type=F32)
    x1_ref[...] = x1
    ms = jnp.mean(x1 * x1, axis=-1, keepdims=True)
    h2 = x1 * lax.rsqrt(ms + EPS) * gffn_ref[...]
    h2_ref[...] = h2
    h_hi = h2.astype(BF16)
    h_lo = (h2 - h_hi.astype(F32)).astype(BF16)
    lt_ref[...] = (_nt_dot(wr_hi_ref[...], h_hi) + _nt_dot(wr_lo_ref[...], h_hi)
                   + _nt_dot(wr_hi_ref[...], h_lo))


def _out_proj(merged, x2d, wo, gffn, wr_t):
    n, d = x2d.shape
    tm = min(TM_OUT, n)
    once = pl.Buffered(1)
    wr_hi = wr_t.astype(BF16)
    wr_lo = (wr_t - wr_hi.astype(F32)).astype(BF16)
    return pl.pallas_call(
        _out_proj_kernel,
        out_shape=(jax.ShapeDtypeStruct((n, d), F32),
                   jax.ShapeDtypeStruct((n, d), F32),
                   jax.ShapeDtypeStruct((ROUTER_ROWS, n), F32)),
        grid=(n // tm,),
        in_specs=[
            pl.BlockSpec((tm, d), lambda i: (i, 0)),
            pl.BlockSpec((tm, d), lambda i: (i, 0)),
            pl.BlockSpec((d, d), lambda i: (0, 0), pipeline_mode=once),
            pl.BlockSpec((1, d), lambda i: (0, 0), pipeline_mode=once),
            pl.BlockSpec((ROUTER_ROWS, d), lambda i: (0, 0), pipeline_mode=once),
            pl.BlockSpec((ROUTER_ROWS, d), lambda i: (0, 0), pipeline_mode=once),
        ],
        out_specs=(pl.BlockSpec((tm, d), lambda i: (i, 0)),
                   pl.BlockSpec((tm, d), lambda i: (i, 0)),
                   pl.BlockSpec((ROUTER_ROWS, tm), lambda i: (0, i))),
        compiler_params=_params(("parallel",)),
        name="out_proj",
    )(merged, x2d, wo, gffn, wr_hi, wr_lo)


def _route_kernel(lt_ref, ids_ref, rank_ref, gate_ref, cnt_ref, carry_ref):
    step = pl.program_id(0)

    @pl.when(step == 0)
    def _():
        carry_ref[...] = jnp.zeros_like(carry_ref)

    t = lt_ref.shape[1]
    el = lt_ref[0:N_EXPERTS, :]
    gl = lt_ref[N_EXPERTS:N_EXPERTS + 8, :]
    grow = lax.broadcasted_iota(jnp.int32, gl.shape, 0)
    gl = jnp.where(grow < N_GROUPS, gl, -jnp.inf)
    gmax = jnp.max(gl, axis=0, keepdims=True)
    grp = jnp.min(jnp.where(gl == gmax, grow, N_GROUPS), axis=0, keepdims=True)
    p_grp = 1.0 / jnp.sum(jnp.exp(gl - gmax), axis=0, keepdims=True)

    eidx = lax.broadcasted_iota(jnp.int32, el.shape, 0)
    in_grp = (eidx // EXPERTS_PER_GROUP) == grp
    e1 = jnp.where(in_grp, el, -jnp.inf)
    top1 = jnp.max(e1, axis=0, keepdims=True)
    idx1 = jnp.min(jnp.where(e1 == top1, eidx, N_EXPERTS), axis=0, keepdims=True)
    e2 = jnp.where(eidx == idx1, -jnp.inf, e1)
    top2 = jnp.max(e2, axis=0, keepdims=True)
    idx2 = jnp.min(jnp.where(e2 == top2, eidx, N_EXPERTS), axis=0, keepdims=True)
    dlt = jnp.exp(top2 - top1)
    den = 1.0 + dlt
    gate_ref[0:1, :] = p_grp / den
    gate_ref[1:2, :] = p_grp * dlt / den
    ids_ref[0:1, :] = idx1
    ids_ref[1:2, :] = idx2

    oh1 = jnp.where(eidx == idx1, 1.0, 0.0)
    oh2 = jnp.where(eidx == idx2, 1.0, 0.0)
    rr = lax.broadcasted_iota(jnp.int32, (t, t), 0)
    cc = lax.broadcasted_iota(jnp.int32, (t, t), 1)
    before = jnp.where(rr < cc, 1.0, 0.0).astype(BF16)
    pre1 = jnp.dot(oh1.astype(BF16), before, preferred_element_type=F32)
    pre2 = jnp.dot(oh2.astype(BF16), before, preferred_element_type=F32)
    cnt1 = jnp.sum(oh1, axis=1, keepdims=True)
    cnt2 = jnp.sum(oh2, axis=1, keepdims=True)
    carry = carry_ref[:, 0:1]
    rank1 = jnp.sum(oh1 * (carry + pre1), axis=0, keepdims=True)
    rank2 = jnp.sum(oh2 * (carry + cnt1 + pre2), axis=0, keepdims=True)
    rank_ref[0:1, :] = rank1.astype(jnp.int32)
    rank_ref[1:2, :] = rank2.astype(jnp.int32)
    new_carry = carry_ref[...] + cnt1 + cnt2
    carry_ref[...] = new_carry
    cnt_ref[...] = new_carry


def _route(logits_t):
    n = logits_t.shape[1]
    t = min(T_ROUTE, n)
    return pl.pallas_call(
        _route_kernel,
        out_shape=(jax.ShapeDtypeStruct((2, n), jnp.int32),
                   jax.ShapeDtypeStruct((2, n), jnp.int32),
                   jax.ShapeDtypeStruct((2, n), F32),
                   jax.ShapeDtypeStruct((N_EXPERTS, LANES), F32)),
        grid=(n // t,),
        in_specs=[pl.BlockSpec((ROUTER_ROWS, t), lambda i: (0, i))],
        out_specs=(pl.BlockSpec((2, t), lambda i: (0, i)),
                   pl.BlockSpec((2, t), lambda i: (0, i)),
                   pl.BlockSpec((2, t), lambda i: (0, i)),
                   pl.BlockSpec((N_EXPERTS, LANES), lambda i: (0, 0))),
        scratch_shapes=[pltpu.VMEM((N_EXPERTS, LANES), F32)],
        compiler_params=_params(("arbitrary",)),
        name="route",
    )(logits_t)


def _dispatch_kernel(zs_ref, zc_ref, nu_ref, dest_ref, h2_ref, xb_hbm, zeros_ref, sem, tail_sem, *, tm, te,
                     n_tiles):
    step = pl.program_id(0)

    def tail_copy(t):
        return pltpu.make_async_copy(zeros_ref, xb_hbm.at[pl.ds(pl.multiple_of(t * te, te), te)], tail_sem)

    @pl.when(step == 0)
    def _():
        zeros_ref[...] = jnp.zeros_like(zeros_ref)

        def pad_copy(row):
            return pltpu.make_async_copy(zeros_ref.at[pl.ds(0, SUBLANES)],
                                         xb_hbm.at[pl.ds(pl.multiple_of(row, SUBLANES), SUBLANES)], sem)

        total = 0
        for e in range(N_EXPERTS):
            lax.fori_loop(0, zc_ref[e], lambda c, _, e=e: (pad_copy(zs_ref[e] + c * SUBLANES).start(), 0)[1], 0)
            total = total + zc_ref[e]
        lax.fori_loop(0, total, lambda c, _: (pad_copy(0).wait(), 0)[1], 0)
        lax.fori_loop(nu_ref[0], n_tiles, lambda t, c: (tail_copy(t).start(), c)[1], 0)

    def issue(blk, _):
        r0 = pl.multiple_of(blk * SUBLANES, SUBLANES)
        for u in range(SUBLANES):
            src = h2_ref.at[pl.ds(r0 + u, 1)]
            for k in range(2):
                pltpu.make_async_copy(src, xb_hbm.at[pl.ds(dest_ref[0, 0, k * tm + r0 + u], 1)],
                                      sem).start(priority=k)
        return 0

    lax.fori_loop(0, tm // SUBLANES, issue, 0)
    for _ in range(2):
        pltpu.make_async_copy(h2_ref, xb_hbm.at[pl.ds(0, tm)], sem).wait()

    @pl.when(step == pl.num_programs(0) - 1)
    def _():
        lax.fori_loop(nu_ref[0], n_tiles, lambda t, c: (tail_copy(t).wait(), c)[1], 0)


def _dispatch(zero_starts, zero_chunks, n_used, dest_tiles, h2, n_tiles, tm, te):
    n, d = h2.shape
    return pl.pallas_call(
        functools.partial(_dispatch_kernel, tm=tm, te=te, n_tiles=n_tiles),
        out_shape=jax.ShapeDtypeStruct((n_tiles * te, d), h2.dtype),
        grid_spec=pltpu.PrefetchScalarGridSpec(
            num_scalar_prefetch=3,
            grid=(n // tm,),
            in_specs=[
                pl.BlockSpec((1, 1, 2 * tm), lambda i, *_: (i, 0, 0), memory_space=pltpu.SMEM),
                pl.BlockSpec((tm, d), lambda i, *_: (i, 0)),
            ],
            out_specs=pl.BlockSpec(memory_space=pl.ANY),
            scratch_shapes=[pltpu.VMEM((te, d), h2.dtype), pltpu.SemaphoreType.DMA,
                            pltpu.SemaphoreType.DMA],
        ),
        compiler_params=_params(("arbitrary",)),
        name="dispatch",
    )(zero_starts, zero_chunks, n_used, dest_tiles, h2)


def _expert_kernel(te_ref, nu_ref, slot_ref, next_ref, xb_hbm, wg_hbm, wu_hbm, wd_hbm, yb_ref,
                   wg_s, wu_s, wd_s, sems, xb_buf, xb_sems, *, te):
    t = pl.program_id(0)
    used = t < nu_ref[0]
    new_run = jnp.logical_or(t == 0, te_ref[t] != te_ref[jnp.maximum(t - 1, 0)])

    def weight_copies(expert, slot):
        return [pltpu.make_async_copy(src.at[expert], dst.at[slot], sems.at[slot, i])
                for i, (src, dst) in enumerate(((wg_hbm, wg_s), (wu_hbm, wu_s), (wd_hbm, wd_s)))]

    @pl.when(t == 0)
    def _():
        for c in weight_copies(te_ref[0], 0):
            c.start()

    @pl.when(jnp.logical_and(used, new_run))
    def _():
        slot = slot_ref[t]
        for c in weight_copies(te_ref[t], slot):
            c.wait()

        @pl.when(next_ref[t] >= 0)
        def _():
            for c in weight_copies(next_ref[t], 1 - slot):
                c.start()

    def row_copy(tile):
        rows = pl.ds(pl.multiple_of(tile * te, te), te)
        slot = tile % XB_SLOTS
        return pltpu.make_async_copy(xb_hbm.at[rows], xb_buf.at[slot], xb_sems.at[slot])

    @pl.when(t == 0)
    def _():
        for ahead in range(XB_SLOTS - 1):
            @pl.when(ahead < nu_ref[0])
            def _(ahead=ahead):
                row_copy(ahead).start()

    @pl.when(used)
    def _():
        @pl.when(t + (XB_SLOTS - 1) < nu_ref[0])
        def _():
            row_copy(t + (XB_SLOTS - 1)).start()

        row_copy(t).wait()
        x = xb_buf[t % XB_SLOTS].astype(BF16)
        slot = slot_ref[t]
        g = jnp.dot(x, wg_s[slot], preferred_element_type=F32)
        u = jnp.dot(x, wu_s[slot], preferred_element_type=F32)
        hid = (g * jax.nn.sigmoid(g)) * u
        yb_ref[...] = jnp.dot(hid.astype(BF16), wd_s[slot], preferred_element_type=F32)

    @pl.when(jnp.logical_not(used))
    def _():
        yb_ref[...] = jnp.zeros_like(yb_ref)


def _experts(tile_expert, n_used, run_slot, next_expert, xb, wg, wu, wd, n_tiles, te):
    d = xb.shape[1]
    de = wg.shape[2]

    hbm = pl.BlockSpec(memory_space=pl.ANY)
    return pl.pallas_call(
        functools.partial(_expert_kernel, te=te),
        out_shape=jax.ShapeDtypeStruct((n_tiles * te, d), F32),
        grid_spec=pltpu.PrefetchScalarGridSpec(
            num_scalar_prefetch=4,
            grid=(n_tiles,),
            in_specs=[hbm, hbm, hbm, hbm],
            out_specs=pl.BlockSpec((te, d), lambda t, *_: (t, 0)),
            scratch_shapes=[pltpu.VMEM((2, d, de), BF16), pltpu.VMEM((2, d, de), BF16), pltpu.VMEM((2, de, d), BF16),
                            pltpu.SemaphoreType.DMA((2, 3)),
                            pltpu.VMEM((XB_SLOTS, te, d), xb.dtype), pltpu.SemaphoreType.DMA((XB_SLOTS,))],
        ),
        compiler_params=_params(("arbitrary",)),
        name="experts",
    )(tile_expert, n_used, run_slot, next_expert, xb, wg, wu, wd)


def _combine_kernel(dest_ref, next_dest_ref, x1_ref, gate_ref, g_ref, yb_hbm, o_ref, buf, sems, *, tm,
                    final_norm):
    i = pl.program_id(0)
    slot = i % 2

    def row_copy(idx_ref, slot, row, k):
        return pltpu.make_async_copy(yb_hbm.at[pl.ds(idx_ref[0, 0, row], 1)], buf.at[slot, pl.ds(row, 1)],
                                     sems.at[slot])

    def tile_wait(slot):
        pltpu.make_async_copy(yb_hbm.at[pl.ds(0, 2 * tm)], buf.at[slot], sems.at[slot]).wait()

    @pl.when(i == 0)
    def _():
        def issue(r, _):
            for k in range(2):
                row_copy(dest_ref, 0, k * tm + r, k).start(priority=k)
            return 0
        lax.fori_loop(0, tm, issue, 0, unroll=8)

    tile_wait(slot)
    for r in range(tm):
        for k in range(2):
            row_copy(next_dest_ref, 1 - slot, k * tm + r, k).start(priority=k)
    gates = gate_ref[...]
    g0, g1 = gates[:, 0:1], gates[:, 1:2]
    x = x1_ref[...] + (buf[slot, 0:tm, :] * g0 + buf[slot, tm:2 * tm, :] * g1)
    if final_norm:
        ms = jnp.mean(x * x, axis=-1, keepdims=True)
        x = x * lax.rsqrt(ms + EPS) * g_ref[...]
    o_ref[...] = x

    @pl.when(i == pl.num_programs(0) - 1)
    def _():
        tile_wait(1 - slot)


def _combine(dest_tiles, x1, gates, g_final, yb, tm, final_norm):
    n, d = x1.shape
    n_steps = n // tm
    return pl.pallas_call(
        functools.partial(_combine_kernel, tm=tm, final_norm=final_norm),
        out_shape=jax.ShapeDtypeStruct((n, d), F32),
        grid=(n_steps,),
        in_specs=[
            pl.BlockSpec((1, 1, 2 * tm), lambda i: (i, 0, 0), memory_space=pltpu.SMEM),
            pl.BlockSpec((1, 1, 2 * tm), lambda i: (jnp.minimum(i + 1, n_steps - 1), 0, 0),
                         memory_space=pltpu.SMEM),
            pl.BlockSpec((tm, d), lambda i: (i, 0)),
            pl.BlockSpec((tm, 2), lambda i: (i, 0)),
            pl.BlockSpec((1, d), lambda i: (0, 0)),
            pl.BlockSpec(memory_space=pl.ANY),
        ],
        out_specs=pl.BlockSpec((tm, d), lambda i: (i, 0)),
        scratch_shapes=[pltpu.VMEM((2, 2 * tm, d), F32), pltpu.SemaphoreType.DMA((2,))],
        compiler_params=_params(("arbitrary",)),
        name="combine",
    )(dest_tiles, dest_tiles, x1, gates, g_final, yb)


def _rope_tables(seq):
    half = DA_HEAD_DIM // 2
    inv_freq = ROPE_THETA ** (-(jnp.arange(half, dtype=F32) * 2.0 / DA_HEAD_DIM))
    ang = jnp.arange(seq, dtype=F32)[:, None] * inv_freq[None, :]
    cos, sin = jnp.cos(ang), jnp.sin(ang)
    cos_tab = jnp.tile(cos, (1, LANES // half))
    sin_tab = jnp.tile(jnp.concatenate([-sin, sin], axis=1), (1, LANES // DA_HEAD_DIM))
    return cos_tab, sin_tab


def _tile_indices(dest, tm):
    n = dest.shape[1]
    return dest.reshape(2, n // tm, tm).transpose(1, 0, 2).reshape(n // tm, 1, 2 * tm)


def _layer(x2d, batch, seq, layer, mix_g, w_in, lam_vecs, sub_g, w_bsb, w_bda, w_out, ffn_g,
           w_grp, w_exp, w_gate, w_up, w_down):
    n, d = x2d.shape
    sbw = SB_HEADS * SB_HEAD_DIM
    daw = DA_HEADS * DA_V_DIM
    qkv_cols = 3 * sbw + 3 * daw
    lam_init = 0.8 - 0.6 * math.exp(-0.3 * layer)

    w_in_bf16 = w_in.astype(BF16)
    cos_tab, sin_tab = _rope_tables(seq)
    rope_lo = 3 * sbw // TN_IN
    rope_hi = (3 * sbw + 2 * daw) // TN_IN
    proj = _in_proj(x2d, mix_g[None, :], w_in_bf16, qkv_cols, cos_tab, sin_tab, seq, rope_lo, rope_hi)
    proj3 = proj.reshape(batch, seq, qkv_cols)
    nsb = sbw // SB_HEAD_DIM
    nda = daw // DA_V_DIM
    o_sb = _sb_attention(proj3, batch, seq, 0, nsb, 2 * nsb)
    n_exp, _, d_exp = w_gate.shape
    o_da, wg_bf16, wu_bf16, wd_bf16 = _da_attention(
        lam_vecs, proj3, sub_g[None, :], batch, seq, 3 * nsb, 3 * nsb + nda, 3 * nsb + 2 * nda, lam_init,
        (w_gate.reshape(n_exp * d, d_exp), w_up.reshape(n_exp * d, d_exp), w_down.reshape(n_exp * d_exp, d)))

    wr_t = jnp.zeros((ROUTER_ROWS, d), F32)
    wr_t = wr_t.at[:N_EXPERTS].set(w_exp.T).at[N_EXPERTS:N_EXPERTS + N_GROUPS].set(w_grp.T)
    merged, = _gate_branch(x2d, mix_g[None, :], o_sb.reshape(n, sbw), o_da.reshape(n, daw), w_in_bf16, qkv_cols,
                           w_bsb.astype(BF16), w_bda.astype(BF16), ())
    x1, h2, logits_t = _out_proj(merged, x2d, w_out.astype(BF16), ffn_g[None, :], wr_t)

    ids, ranks, gates, counts = _route(logits_t)
    te = TM_EXPERT
    counts = counts[:, 0].astype(jnp.int32)
    padded = (counts + te - 1) // te * te
    pad_ends = jnp.cumsum(padded)
    pad_starts = pad_ends - padded
    expert_ids = jnp.arange(N_EXPERTS, dtype=jnp.int32)
    dest = ranks + jnp.sum(jnp.where(ids[..., None] == expert_ids, pad_starts, 0), axis=-1)
    n_tiles = (2 * n) // te + N_EXPERTS
    n_used = (pad_ends[-1] // te).astype(jnp.int32)[None]
    tile_row0 = jnp.arange(n_tiles, dtype=jnp.int32) * te
    tile_expert = jnp.minimum(jnp.sum((pad_ends[None, :] <= tile_row0[:, None]).astype(jnp.int32), axis=1),
                              N_EXPERTS - 1)
    zero_starts = ((pad_starts + counts) // SUBLANES * SUBLANES).astype(jnp.int32)
    zero_chunks = ((pad_ends - zero_starts) // SUBLANES).astype(jnp.int32)

    tmd = min(TM_DISPATCH, n)
    xb = _dispatch(zero_starts, zero_chunks, n_used, _tile_indices(dest, tmd), h2, n_tiles, tmd, te)
    tile_ids = jnp.arange(n_tiles, dtype=jnp.int32)
    new_run = jnp.logical_or(tile_ids == 0, tile_expert != jnp.roll(tile_expert, 1))
    run_slot = ((jnp.cumsum(new_run.astype(jnp.int32)) - 1) % 2).astype(jnp.int32)
    later = jnp.logical_and(tile_ids[None, :] < n_used, tile_expert[None, :] > tile_expert[:, None])
    next_expert = jnp.min(jnp.where(later, tile_expert[None, :], N_EXPERTS), axis=1)
    next_expert = jnp.where(next_expert == N_EXPERTS, -1, next_expert).astype(jnp.int32)
    yb = _experts(tile_expert, n_used, run_slot, next_expert, xb, wg_bf16.reshape(w_gate.shape),
                  wu_bf16.reshape(w_up.shape), wd_bf16.reshape(w_down.shape), n_tiles, te)
    return x1, gates, dest, yb


def kernel(x, mix_norm_g, w_in, lambda_q1, lambda_k1, lambda_q2, lambda_k2, diff_subnorm_g, w_branch_sb, w_branch_diff, w_out, ffn_norm_g, w_group_router, w_expert_router, w_gate, w_up, w_down, final_norm_g):
    batch, seq, d = x.shape
    depth = w_in.shape[0]
    n = batch * seq
    x2d = x.reshape(n, d)
    tmc = min(TM_COMBINE, n)
    for l in range(depth):
        lam_vecs = jnp.stack([lambda_q1[l], lambda_k1[l], lambda_q2[l], lambda_k2[l]]).astype(F32)
        x1, gates, dest, yb = _layer(
            x2d, batch, seq, l, mix_norm_g[l], w_in[l], lam_vecs, diff_subnorm_g[l], w_branch_sb[l],
            w_branch_diff[l], w_out[l], ffn_norm_g[l], w_group_router[l], w_expert_router[l],
            w_gate[l], w_up[l], w_down[l])
        x2d = _combine(_tile_indices(dest, tmc), x1, gates.T, final_norm_g[None, :], yb, tmc,
                       final_norm=(l == depth - 1))
    return x2d.reshape(batch, seq, d)
```

```python
import functools
import math

import jax
import jax.numpy as jnp
from jax import lax
from jax.experimental import pallas as pl
from jax.experimental.pallas import tpu as pltpu

F32 = jnp.float32
BF16 = jnp.bfloat16

CHUNK = 64
SB_HEADS = 8
SB_HEAD_DIM = 128
DA_HEADS = 8
DA_HEAD_DIM = 64
DA_V_DIM = 2 * DA_HEAD_DIM
N_GROUPS = 4
EXPERTS_PER_GROUP = 8
N_EXPERTS = N_GROUPS * EXPERTS_PER_GROUP
ROPE_THETA = 10000.0
EPS = 1e-6

LANES = 128
SUBLANES = 8
ROUTER_ROWS = 40
VMEM_LIMIT_BYTES = 56 * 1024 * 1024

SB_LOG_WEIGHT_FLOOR = -104.0
SB_MASKED_LOGIT = -1e30
LOG2_E = 1.4426950408889634

TM_IN = 1024
TN_IN = 1024
TQ_ATTN = 256
ATTN_TILES_PER_STEP = 4
TK_SB = 128
SB_NEAR_PAST = 256
TM_GATE = 1024
TN_GATE = 256
TM_OUT = 512
T_ROUTE = 512
TM_DISPATCH = 1024
TM_EXPERT = 256
XB_SLOTS = 3
TM_COMBINE = 256


def _params(sem, vmem=VMEM_LIMIT_BYTES):
    return pltpu.CompilerParams(dimension_semantics=sem, vmem_limit_bytes=vmem)


def _nt_dot(a, b, **kw):
    return lax.dot_general(a, b, (((1,), (1,)), ((), ())), preferred_element_type=F32, **kw)


def _in_proj_kernel(x_ref, g_ref, w_ref, cos_ref, sin_ref, o_ref, h_ref, *, rope_lo, rope_hi):
    j = pl.program_id(1)

    @pl.when(j == 0)
    def _():
        x = x_ref[...]
        ms = jnp.mean(x * x, axis=-1, keepdims=True)
        h = (x * lax.rsqrt(ms + EPS) * g_ref[...]).astype(BF16)
        h_ref[...] = h
        o_ref[...] = jnp.dot(h, w_ref[...], preferred_element_type=F32).astype(BF16)

    is_rope = jnp.logical_and(j >= rope_lo, j < rope_hi)

    @pl.when(is_rope)
    def _():
        y = jnp.dot(h_ref[...], w_ref[...], preferred_element_type=F32)
        cos = cos_ref[...]
        sin = sin_ref[...]
        lane = lax.broadcasted_iota(jnp.int32, cos.shape, 1)
        first_half = (lane % DA_HEAD_DIM) < (DA_HEAD_DIM // 2)
        for hh in range(y.shape[1] // LANES):
            yh = y[:, hh * LANES:(hh + 1) * LANES]
            partner = jnp.where(first_half,
                                pltpu.roll(yh, LANES - DA_HEAD_DIM // 2, 1),
                                pltpu.roll(yh, DA_HEAD_DIM // 2, 1))
            o_ref[:, hh * LANES:(hh + 1) * LANES] = (yh * cos + partner * sin).astype(BF16)

    @pl.when(jnp.logical_and(j > 0, jnp.logical_not(is_rope)))
    def _():
        o_ref[...] = jnp.dot(h_ref[...], w_ref[...], preferred_element_type=F32).astype(BF16)


def _in_proj(x2d, g, w_bf16, cols, cos_tab, sin_tab, seq, rope_lo, rope_hi):
    n, d = x2d.shape
    tm, tn = min(TM_IN, seq), TN_IN
    pos_tiles = seq // tm
    assert 0 < rope_lo <= rope_hi, "the first column tile carries the RMSNorm and must not be a rotary tile"
    return pl.pallas_call(
        functools.partial(_in_proj_kernel, rope_lo=rope_lo, rope_hi=rope_hi),
        out_shape=jax.ShapeDtypeStruct((n, cols), BF16),
        grid=(n // tm, cols // tn),
        in_specs=[
            pl.BlockSpec((tm, d), lambda i, j: (i, 0)),
            pl.BlockSpec((1, d), lambda i, j: (0, 0)),
            pl.BlockSpec((d, tn), lambda i, j: (0, j)),
            pl.BlockSpec((tm, LANES), lambda i, j: (i % pos_tiles, 0)),
            pl.BlockSpec((tm, LANES), lambda i, j: (i % pos_tiles, 0)),
        ],
        out_specs=pl.BlockSpec((tm, tn), lambda i, j: (i, j)),
        scratch_shapes=[pltpu.VMEM((tm, d), BF16)],
        compiler_params=_params(("parallel", "arbitrary")),
        name="in_proj",
    )(x2d, g, w_bf16, cos_tab, sin_tab)


def _sb_program(q_ref, k_ref, v_ref, o_ref, *, tq, tk, tiles_per_step, near_past, scale):
    row = lax.broadcasted_iota(jnp.int32, (2 * tk, 2 * tk), 0) % tk
    col = lax.broadcasted_iota(jnp.int32, (2 * tk, 2 * tk), 1)
    u2 = jnp.where(jnp.logical_or(col >= tk, row > col), 1.0, 0.0).astype(BF16)

    def span(q, start, width, diag_cols, running):
        k = k_ref[pl.ds(start, width), :]
        v = v_ref[pl.ds(start, width), :]
        z = _nt_dot(q, k) * scale
        if diag_cols is not None:
            r = lax.broadcasted_iota(jnp.int32, z.shape, 0)
            c = lax.broadcasted_iota(jnp.int32, z.shape, 1)
            z = jnp.where((c - diag_cols) < r, z, SB_MASKED_LOGIT)
        sp = jnp.maximum(z, 0.0) + jnp.log(1.0 + jnp.exp2(jnp.abs(z) * (-LOG2_E)))
        hi = sp.astype(BF16)
        lo = (sp - hi.astype(F32)).astype(BF16)
        nb = width // tk
        suffix = [None] * nb
        for blk in range(nb - 1, -1, -1):
            sl = slice(blk * tk, (blk + 1) * tk)
            cs = jnp.dot(jnp.concatenate([hi[:, sl], lo[:, sl]], axis=1), u2, preferred_element_type=F32)
            suffix[blk] = cs[:, :tk] + running
            running = running + cs[:, tk:]
        a = jnp.exp((z - sp) - jnp.concatenate(suffix, axis=1))
        return jnp.dot(a.astype(BF16), v, preferred_element_type=F32), running

    zero_run = jnp.zeros((tq, tk), F32)

    def far_keys(q, first, running, acc):
        def alive_after(running):
            return jnp.min(running) < -SB_LOG_WEIGHT_FLOOR

        def cond(c):
            j, alive, _, _ = c
            return jnp.logical_and(j >= 0, alive)

        def body(c):
            j, _, running, acc = c
            out, running = span(q, pl.multiple_of(j * tk, tk), tk, None, running)
            return j - 1, alive_after(running), running, acc + out

        return lax.while_loop(cond, body, (first, alive_after(running), running, acc))[3]

    def near(first_row, sub):
        q = q_ref[sub * tq:(sub + 1) * tq, :]
        t0 = first_row + sub * tq
        past = min(near_past, t0)
        acc, running = span(q, t0 - past, tq + past, past, zero_run)
        return acc, running, t0 - past

    def finish(near_out):
        for sub, (acc, running, start) in enumerate(near_out):
            if start > 0:
                acc = far_keys(q_ref[sub * tq:(sub + 1) * tq, :], start // tk - 1, running, acc)
            o_ref[sub * tq:(sub + 1) * tq, :] = acc.astype(o_ref.dtype)

    return near, finish


def _da_program(lam_ref, q_ref, k_ref, vext_ref, g_ref, o_ref, *, tq, tiles_per_step, lam_init):
    lv = lam_ref[...]
    lam = (jnp.exp(jnp.sum(lv[0:1] * lv[1:2], axis=-1, keepdims=True))
           - jnp.exp(jnp.sum(lv[2:3] * lv[3:4], axis=-1, keepdims=True)) + lam_init)
    lane = lax.broadcasted_iota(jnp.int32, (tq, q_ref.shape[1]), 1)
    r = lax.broadcasted_iota(jnp.int32, (tq, tq), 0)
    c = lax.broadcasted_iota(jnp.int32, (tq, tq), 1)
    diag_mask = (c // CHUNK) <= (r // CHUNK)

    def query_maps(rows):
        qs = q_ref[rows, :] * jnp.asarray(1.0 / math.sqrt(DA_HEAD_DIM), BF16)
        zero = jnp.zeros_like(qs)
        return jnp.where(lane < DA_HEAD_DIM, qs, zero), jnp.where(lane >= DA_HEAD_DIM, qs, zero)

    def softmax_av(qm, lo):
        s = _nt_dot(qm, k_ref[0:lo + tq, :])
        sd = jnp.where(diag_mask, s[:, lo:], -jnp.inf)
        s = sd if lo == 0 else jnp.concatenate([s[:, :lo], sd], axis=1)
        m = jnp.max(s, axis=-1, keepdims=True)
        acc = jnp.dot(jnp.exp(s - m).astype(BF16), vext_ref[0:lo + tq, :], preferred_element_type=F32)
        return acc[:, :DA_V_DIM] / acc[:, DA_V_DIM:]

    def run(first_row, sub):
        rows = slice(sub * tq, (sub + 1) * tq)
        q1, q2 = query_maps(rows)
        lo = first_row + sub * tq
        o = softmax_av(q1, lo) - lam * softmax_av(q2, lo)
        ms = jnp.mean(o * o, axis=-1, keepdims=True)
        o_ref[rows, :] = (o * lax.rsqrt(ms + EPS) * g_ref[...] * (1.0 - lam_init)).astype(o_ref.dtype)

    return run


def _mixers_kernel(lam_ref, qs_ref, ks_ref, vs_ref, qd_ref, kd_ref, vd_ref, g_ref, *rest, tq, tk, tiles_per_step,
                   n_steps, near_past, sb_scale, lam_init, n_cast):
    cast_in = rest[:n_cast]
    osb_ref, oda_ref = rest[n_cast], rest[n_cast + 1]
    cast_out = rest[n_cast + 2:2 * n_cast + 2]
    vext_ref = rest[-1]
    i = pl.program_id(2)

    @pl.when(i == 0)
    def _():
        vext_ref[:, :DA_V_DIM] = vd_ref[...]
        vext_ref[:, DA_V_DIM:] = jnp.ones((vext_ref.shape[0], DA_V_DIM), BF16)

    sb_near, sb_finish = _sb_program(qs_ref, ks_ref, vs_ref, osb_ref, tq=tq, tk=tk, tiles_per_step=tiles_per_step,
                                     near_past=near_past, scale=sb_scale)
    da_run = _da_program(lam_ref, qd_ref, kd_ref, vext_ref, g_ref, oda_ref, tq=tq, tiles_per_step=tiles_per_step,
                         lam_init=lam_init)
    rows = tq * tiles_per_step
    for step in range(n_steps):
        @pl.when(i == step)
        def _(step=step):
            near = []
            for sub in range(tiles_per_step):
                near.append(sb_near(step * rows, sub))
                da_run(step * rows, sub)
            for src, dst in zip(cast_in, cast_out):
                dst[...] = src[...].astype(dst.dtype)
            sb_finish(near)


def _mixers(lam_vecs, proj3, sub_g, batch, seq, sb_cols, da_cols, lam_init, to_cast):
    d = SB_HEAD_DIM
    assert DA_V_DIM == d and SB_HEADS == DA_HEADS
    tq, tk = min(TQ_ATTN, seq), min(TK_SB, seq)
    tps = min(ATTN_TILES_PER_STEP, seq // tq)
    rows = tq * tps
    n_i = seq // rows
    n_steps = batch * SB_HEADS * n_i
    cast_specs = []
    for a in to_cast:
        assert a.shape[0] % (n_steps * 16) == 0, "cast operand rows must split into whole bf16 tiles per step"
        cast_specs.append(pl.BlockSpec((a.shape[0] // n_steps, a.shape[1]),
                                       lambda b, h, i: ((b * SB_HEADS + h) * n_i + i, 0)))

    def q_spec(col):
        return pl.BlockSpec((None, rows, d), lambda b, h, i: (b, i, col + h))

    def kv_spec(col):
        return pl.BlockSpec((None, seq, d), lambda b, h, i: (b, 0, col + h))

    out_spec = pl.BlockSpec((None, rows, d), lambda b, h, i: (b, i, h))
    return pl.pallas_call(
        functools.partial(_mixers_kernel, tq=tq, tk=tk, tiles_per_step=tps, n_steps=n_i, near_past=SB_NEAR_PAST,
                          sb_scale=1.0 / math.sqrt(d), lam_init=lam_init, n_cast=len(to_cast)),
        out_shape=(jax.ShapeDtypeStruct((batch, seq, SB_HEADS * d), BF16),
                   jax.ShapeDtypeStruct((batch, seq, DA_HEADS * d), BF16),
                   *[jax.ShapeDtypeStruct(a.shape, BF16) for a in to_cast]),
        grid=(batch, SB_HEADS, n_i),
        in_specs=[
            pl.BlockSpec(lam_vecs.shape, lambda b, h, i: (0, 0)),
            q_spec(sb_cols[0]), kv_spec(sb_cols[1]), kv_spec(sb_cols[2]),
            q_spec(da_cols[0]), kv_spec(da_cols[1]), kv_spec(da_cols[2]),
            pl.BlockSpec((1, d), lambda b, h, i: (0, 0)),
            *cast_specs,
        ],
        out_specs=(out_spec, out_spec, *cast_specs),
        scratch_shapes=[pltpu.VMEM((seq, 2 * d), BF16)],
        compiler_params=_params(("parallel", "parallel", "arbitrary")),
        name="mixers",
    )(lam_vecs, proj3, proj3, proj3, proj3, proj3, proj3, sub_g, *to_cast)


def _gate_branch_kernel(x_ref, gmix_ref, osb_ref, oda_ref, wgs_ref, wgd_ref, wbs_ref, wbd_ref, *rest, n_cast):
    cast_in, m_ref, cast_out, h_scr = rest[:n_cast], rest[n_cast], rest[n_cast + 1:2 * n_cast + 1], rest[-1]

    def column_tile(h):
        gate_sb = jax.nn.sigmoid(jnp.dot(h, wgs_ref[...], preferred_element_type=F32))
        gate_da = jax.nn.sigmoid(jnp.dot(h, wgd_ref[...], preferred_element_type=F32))
        br_sb = jnp.dot(osb_ref[...], wbs_ref[...], preferred_element_type=F32)
        br_da = jnp.dot(oda_ref[...], wbd_ref[...], preferred_element_type=F32)
        m_ref[...] = (gate_sb * br_sb + gate_da * br_da).astype(m_ref.dtype)
        for src, dst in zip(cast_in, cast_out):
            dst[...] = src[...].astype(dst.dtype)

    @pl.when(pl.program_id(1) == 0)
    def _():
        x = x_ref[...]
        ms = jnp.mean(x * x, axis=-1, keepdims=True)
        h = (x * lax.rsqrt(ms + EPS) * gmix_ref[...]).astype(BF16)
        h_scr[...] = h
        column_tile(h)

    @pl.when(pl.program_id(1) > 0)
    def _():
        column_tile(h_scr[...])


def _gate_branch(x2d, gmix, o_sb, o_da, w_in_bf16, gate_col0, wbs, wbd, to_cast):
    n, d = x2d.shape
    tm, tn = min(TM_GATE, n), TN_GATE
    wsb, wda = o_sb.shape[1], o_da.shape[1]
    sb0, da0 = gate_col0 // tn, (gate_col0 + d) // tn
    nj = d // tn
    n_steps = (n // tm) * nj
    cast_specs = []
    for a in to_cast:
        assert a.shape[0] % (n_steps * 16) == 0, "cast operand rows must split into whole bf16 tiles per step"
        cast_specs.append(pl.BlockSpec((a.shape[0] // n_steps, a.shape[1]), lambda i, j: (i * nj + j, 0)))
    return pl.pallas_call(
        functools.partial(_gate_branch_kernel, n_cast=len(to_cast)),
        out_shape=(jax.ShapeDtypeStruct((n, d), BF16),
                   *[jax.ShapeDtypeStruct(a.shape, BF16) for a in to_cast]),
        grid=(n // tm, nj),
        in_specs=[
            pl.BlockSpec((tm, d), lambda i, j: (i, 0)),
            pl.BlockSpec((1, d), lambda i, j: (0, 0)),
            pl.BlockSpec((tm, wsb), lambda i, j: (i, 0)),
            pl.BlockSpec((tm, wda), lambda i, j: (i, 0)),
            pl.BlockSpec((d, tn), lambda i, j: (0, sb0 + j)),
            pl.BlockSpec((d, tn), lambda i, j: (0, da0 + j)),
            pl.BlockSpec((wsb, tn), lambda i, j: (0, j)),
            pl.BlockSpec((wda, tn), lambda i, j: (0, j)),
            *cast_specs,
        ],
        out_specs=(pl.BlockSpec((tm, tn), lambda i, j: (i, j)), *cast_specs),
        scratch_shapes=[pltpu.VMEM((tm, d), BF16)],
        compiler_params=_params(("parallel", "arbitrary")),
        name="gate_branch",
    )(x2d, gmix, o_sb, o_da, w_in_bf16, w_in_bf16, wbs, wbd, *to_cast)


def _out_proj_kernel(m_ref, x_ref, wo_ref, gffn_ref, wr_hi_ref, wr_lo_ref, x1_ref, h2_ref, lt_ref):
    x1 = x_ref[...] + jnp.dot(m_ref[...], wo_ref[...], preferred_element_type=F32)
    x1_ref[...] = x1
    ms = jnp.mean(x1 * x1, axis=-1, keepdims=True)
    h2 = x1 * lax.rsqrt(ms + EPS) * gffn_ref[...]
    h2_ref[...] = h2
    h_hi = h2.astype(BF16)
    h_lo = (h2 - h_hi.astype(F32)).astype(BF16)
    lt_ref[...] = (_nt_dot(wr_hi_ref[...], h_hi) + _nt_dot(wr_lo_ref[...], h_hi)
                   + _nt_dot(wr_hi_ref[...], h_lo))


def _out_proj(merged, x2d, wo, gffn, wr_t):
    n, d = x2d.shape
    tm = min(TM_OUT, n)
    once = pl.Buffered(1)
    wr_hi = wr_t.astype(BF16)
    wr_lo = (wr_t - wr_hi.astype(F32)).astype(BF16)
    return pl.pallas_call(
        _out_proj_kernel,
        out_shape=(jax.ShapeDtypeStruct((n, d), F32),
                   jax.ShapeDtypeStruct((n, d), F32),
                   jax.ShapeDtypeStruct((ROUTER_ROWS, n), F32)),
        grid=(n // tm,),
        in_specs=[
            pl.BlockSpec((tm, d), lambda i: (i, 0)),
            pl.BlockSpec((tm, d), lambda i: (i, 0)),
            pl.BlockSpec((d, d), lambda i: (0, 0), pipeline_mode=once),
            pl.BlockSpec((1, d), lambda i: (0, 0), pipeline_mode=once),
            pl.BlockSpec((ROUTER_ROWS, d), lambda i: (0, 0), pipeline_mode=once),
            pl.BlockSpec((ROUTER_ROWS, d), lambda i: (0, 0), pipeline_mode=once),
        ],
        out_specs=(pl.BlockSpec((tm, d), lambda i: (i, 0)),
                   pl.BlockSpec((tm, d), lambda i: (i, 0)),
                   pl.BlockSpec((ROUTER_ROWS, tm), lambda i: (0, i))),
        compiler_params=_params(("parallel",)),
        name="out_proj",
    )(merged, x2d, wo, gffn, wr_hi, wr_lo)


def _route_kernel(lt_ref, ids_ref, rank_ref, gate_ref, cnt_ref, carry_ref):
    step = pl.program_id(0)

    @pl.when(step == 0)
    def _():
        carry_ref[...] = jnp.zeros_like(carry_ref)

    t = lt_ref.shape[1]
    el = lt_ref[0:N_EXPERTS, :]
    gl = lt_ref[N_EXPERTS:N_EXPERTS + 8, :]
    grow = lax.broadcasted_iota(jnp.int32, gl.shape, 0)
    gl = jnp.where(grow < N_GROUPS, gl, -jnp.inf)
    gmax = jnp.max(gl, axis=0, keepdims=True)
    grp = jnp.min(jnp.where(gl == gmax, grow, N_GROUPS), axis=0, keepdims=True)
    p_grp = 1.0 / jnp.sum(jnp.exp(gl - gmax), axis=0, keepdims=True)

    eidx = lax.broadcasted_iota(jnp.int32, el.shape, 0)
    in_grp = (eidx // EXPERTS_PER_GROUP) == grp
    e1 = jnp.where(in_grp, el, -jnp.inf)
    top1 = jnp.max(e1, axis=0, keepdims=True)
    idx1 = jnp.min(jnp.where(e1 == top1, eidx, N_EXPERTS), axis=0, keepdims=True)
    e2 = jnp.where(eidx == idx1, -jnp.inf, e1)
    top2 = jnp.max(e2, axis=0, keepdims=True)
    idx2 = jnp.min(jnp.where(e2 == top2, eidx, N_EXPERTS), axis=0, keepdims=True)
    dlt = jnp.exp(top2 - top1)
    den = 1.0 + dlt
    gate_ref[0:1, :] = p_grp / den
    gate_ref[1:2, :] = p_grp * dlt / den
    ids_ref[0:1, :] = idx1
    ids_ref[1:2, :] = idx2

    oh1 = jnp.where(eidx == idx1, 1.0, 0.0)
    oh2 = jnp.where(eidx == idx2, 1.0, 0.0)
    rr = lax.broadcasted_iota(jnp.int32, (t, t), 0)
    cc = lax.broadcasted_iota(jnp.int32, (t, t), 1)
    before = jnp.where(rr < cc, 1.0, 0.0).astype(BF16)
    pre1 = jnp.dot(oh1.astype(BF16), before, preferred_element_type=F32)
    pre2 = jnp.dot(oh2.astype(BF16), before, preferred_element_type=F32)
    cnt1 = jnp.sum(oh1, axis=1, keepdims=True)
    cnt2 = jnp.sum(oh2, axis=1, keepdims=True)
    carry = carry_ref[:, 0:1]
    rank1 = jnp.sum(oh1 * (carry + pre1), axis=0, keepdims=True)
    rank2 = jnp.sum(oh2 * (carry + cnt1 + pre2), axis=0, keepdims=True)
    rank_ref[0:1, :] = rank1.astype(jnp.int32)
    rank_ref[1:2, :] = rank2.astype(jnp.int32)
    new_carry = carry_ref[...] + cnt1 + cnt2
    carry_ref[...] = new_carry
    cnt_ref[...] = new_carry


def _route(logits_t):
    n = logits_t.shape[1]
    t = min(T_ROUTE, n)
    return pl.pallas_call(
        _route_kernel,
        out_shape=(jax.ShapeDtypeStruct((2, n), jnp.int32),
                   jax.ShapeDtypeStruct((2, n), jnp.int32),
                   jax.ShapeDtypeStruct((2, n), F32),
                   jax.ShapeDtypeStruct((N_EXPERTS, LANES), F32)),
        grid=(n // t,),
        in_specs=[pl.BlockSpec((ROUTER_ROWS, t), lambda i: (0, i))],
        out_specs=(pl.BlockSpec((2, t), lambda i: (0, i)),
                   pl.BlockSpec((2, t), lambda i: (0, i)),
                   pl.BlockSpec((2, t), lambda i: (0, i)),
                   pl.BlockSpec((N_EXPERTS, LANES), lambda i: (0, 0))),
        scratch_shapes=[pltpu.VMEM((N_EXPERTS, LANES), F32)],
        compiler_params=_params(("arbitrary",)),
        name="route",
    )(logits_t)


def _dispatch_kernel(zs_ref, zc_ref, nu_ref, dest_ref, h2_ref, xb_hbm, zeros_ref, sem, tail_sem, *, tm, te,
                     n_tiles):
    step = pl.program_id(0)

    def tail_copy(t):
        return pltpu.make_async_copy(zeros_ref, xb_hbm.at[pl.ds(pl.multiple_of(t * te, te), te)], tail_sem)

    @pl.when(step == 0)
    def _():
        zeros_ref[...] = jnp.zeros_like(zeros_ref)

        def pad_copy(row):
            return pltpu.make_async_copy(zeros_ref.at[pl.ds(0, SUBLANES)],
                                         xb_hbm.at[pl.ds(pl.multiple_of(row, SUBLANES), SUBLANES)], sem)

        total = 0
        for e in range(N_EXPERTS):
            lax.fori_loop(0, zc_ref[e], lambda c, _, e=e: (pad_copy(zs_ref[e] + c * SUBLANES).start(), 0)[1], 0)
            total = total + zc_ref[e]
        lax.fori_loop(0, total, lambda c, _: (pad_copy(0).wait(), 0)[1], 0)
        lax.fori_loop(nu_ref[0], n_tiles, lambda t, c: (tail_copy(t).start(), c)[1], 0)

    def issue(blk, _):
        r0 = pl.multiple_of(blk * SUBLANES, SUBLANES)
        for u in range(SUBLANES):
            src = h2_ref.at[pl.ds(r0 + u, 1)]
            for k in range(2):
                pltpu.make_async_copy(src, xb_hbm.at[pl.ds(dest_ref[0, 0, k * tm + r0 + u], 1)],
                                      sem).start(priority=k)
        return 0

    lax.fori_loop(0, tm // SUBLANES, issue, 0)
    for _ in range(2):
        pltpu.make_async_copy(h2_ref, xb_hbm.at[pl.ds(0, tm)], sem).wait()

    @pl.when(step == pl.num_programs(0) - 1)
    def _():
        lax.fori_loop(nu_ref[0], n_tiles, lambda t, c: (tail_copy(t).wait(), c)[1], 0)


def _dispatch(zero_starts, zero_chunks, n_used, dest_tiles, h2, n_tiles, tm, te):
    n, d = h2.shape
    return pl.pallas_call(
        functools.partial(_dispatch_kernel, tm=tm, te=te, n_tiles=n_tiles),
        out_shape=jax.ShapeDtypeStruct((n_tiles * te, d), h2.dtype),
        grid_spec=pltpu.PrefetchScalarGridSpec(
            num_scalar_prefetch=3,
            grid=(n // tm,),
            in_specs=[
                pl.BlockSpec((1, 1, 2 * tm), lambda i, *_: (i, 0, 0), memory_space=pltpu.SMEM),
                pl.BlockSpec((tm, d), lambda i, *_: (i, 0)),
            ],
            out_specs=pl.BlockSpec(memory_space=pl.ANY),
            scratch_shapes=[pltpu.VMEM((te, d), h2.dtype), pltpu.SemaphoreType.DMA,
                            pltpu.SemaphoreType.DMA],
        ),
        compiler_params=_params(("arbitrary",)),
        name="dispatch",
    )(zero_starts, zero_chunks, n_used, dest_tiles, h2)


def _expert_kernel(te_ref, nu_ref, slot_ref, next_ref, xb_hbm, wg_hbm, wu_hbm, wd_hbm, yb_ref,
                   wg_s, wu_s, wd_s, sems, xb_buf, xb_sems, *, te):
    t = pl.program_id(0)
    used = t < nu_ref[0]
    new_run = jnp.logical_or(t == 0, te_ref[t] != te_ref[jnp.maximum(t - 1, 0)])

    def weight_copies(expert, slot):
        return [pltpu.make_async_copy(src.at[expert], dst.at[slot], sems.at[slot, i])
                for i, (src, dst) in enumerate(((wg_hbm, wg_s), (wu_hbm, wu_s), (wd_hbm, wd_s)))]

    @pl.when(t == 0)
    def _():
        for c in weight_copies(te_ref[0], 0):
            c.start()

    @pl.when(jnp.logical_and(used, new_run))
    def _():
        slot = slot_ref[t]
        for c in weight_copies(te_ref[t], slot):
            c.wait()

        @pl.when(next_ref[t] >= 0)
        def _():
            for c in weight_copies(next_ref[t], 1 - slot):
                c.start()

    def row_copy(tile):
        rows = pl.ds(pl.multiple_of(tile * te, te), te)
        slot = tile % XB_SLOTS
        return pltpu.make_async_copy(xb_hbm.at[rows], xb_buf.at[slot], xb_sems.at[slot])

    @pl.when(t == 0)
    def _():
        for ahead in range(XB_SLOTS - 1):
            @pl.when(ahead < nu_ref[0])
            def _(ahead=ahead):
                row_copy(ahead).start()

    @pl.when(used)
    def _():
        @pl.when(t + (XB_SLOTS - 1) < nu_ref[0])
        def _():
            row_copy(t + (XB_SLOTS - 1)).start()

        row_copy(t).wait()
        x = xb_buf[t % XB_SLOTS].astype(BF16)
        slot = slot_ref[t]
        g = jnp.dot(x, wg_s[slot], preferred_element_type=F32)
        u = jnp.dot(x, wu_s[slot], preferred_element_type=F32)
        hid = (g * jax.nn.sigmoid(g)) * u
        yb_ref[...] = jnp.dot(hid.astype(BF16), wd_s[slot], preferred_element_type=F32)

    @pl.when(jnp.logical_not(used))
    def _():
        yb_ref[...] = jnp.zeros_like(yb_ref)


def _experts(tile_expert, n_used, run_slot, next_expert, xb, wg, wu, wd, n_tiles, te):
    d = xb.shape[1]
    de = wg.shape[2]

    hbm = pl.BlockSpec(memory_space=pl.ANY)
    return pl.pallas_call(
        functools.partial(_expert_kernel, te=te),
        out_shape=jax.ShapeDtypeStruct((n_tiles * te, d), F32),
        grid_spec=pltpu.PrefetchScalarGridSpec(
            num_scalar_prefetch=4,
            grid=(n_tiles,),
            in_specs=[hbm, hbm, hbm, hbm],
            out_specs=pl.BlockSpec((te, d), lambda t, *_: (t, 0)),
            scratch_shapes=[pltpu.VMEM((2, d, de), BF16), pltpu.VMEM((2, d, de), BF16), pltpu.VMEM((2, de, d), BF16),
                            pltpu.SemaphoreType.DMA((2, 3)),
                            pltpu.VMEM((XB_SLOTS, te, d), xb.dtype), pltpu.SemaphoreType.DMA((XB_SLOTS,))],
        ),
        compiler_params=_params(("arbitrary",)),
        name="experts",
    )(tile_expert, n_used, run_slot, next_expert, xb, wg, wu, wd)


def _combine_kernel(dest_ref, next_dest_ref, x1_ref, gate_ref, g_ref, yb_hbm, o_ref, buf, sems, *, tm,
                    final_norm):
    i = pl.program_id(0)
    slot = i % 2

    def row_copy(idx_ref, slot, row, k):
        return pltpu.make_async_copy(yb_hbm.at[pl.ds(idx_ref[0, 0, row], 1)], buf.at[slot, pl.ds(row, 1)],
                                     sems.at[slot])

    def tile_wait(slot):
        pltpu.make_async_copy(yb_hbm.at[pl.ds(0, 2 * tm)], buf.at[slot], sems.at[slot]).wait()

    @pl.when(i == 0)
    def _():
        def issue(r, _):
            for k in range(2):
                row_copy(dest_ref, 0, k * tm + r, k).start(priority=k)
            return 0
        lax.fori_loop(0, tm, issue, 0, unroll=8)

    tile_wait(slot)
    for r in range(tm):
        for k in range(2):
            row_copy(next_dest_ref, 1 - slot, k * tm + r, k).start(priority=k)
    gates = gate_ref[...]
    g0, g1 = gates[:, 0:1], gates[:, 1:2]
    x = x1_ref[...] + (buf[slot, 0:tm, :] * g0 + buf[slot, tm:2 * tm, :] * g1)
    if final_norm:
        ms = jnp.mean(x * x, axis=-1, keepdims=True)
        x = x * lax.rsqrt(ms + EPS) * g_ref[...]
    o_ref[...] = x

    @pl.when(i == pl.num_programs(0) - 1)
    def _():
        tile_wait(1 - slot)


def _combine(dest_tiles, x1, gates, g_final, yb, tm, final_norm):
    n, d = x1.shape
    n_steps = n // tm
    return pl.pallas_call(
        functools.partial(_combine_kernel, tm=tm, final_norm=final_norm),
        out_shape=jax.ShapeDtypeStruct((n, d), F32),
        grid=(n_steps,),
        in_specs=[
            pl.BlockSpec((1, 1, 2 * tm), lambda i: (i, 0, 0), memory_space=pltpu.SMEM),
            pl.BlockSpec((1, 1, 2 * tm), lambda i: (jnp.minimum(i + 1, n_steps - 1), 0, 0),
                         memory_space=pltpu.SMEM),
            pl.BlockSpec((tm, d), lambda i: (i, 0)),
            pl.BlockSpec((tm, 2), lambda i: (i, 0)),
            pl.BlockSpec((1, d), lambda i: (0, 0)),
            pl.BlockSpec(memory_space=pl.ANY),
        ],
        out_specs=pl.BlockSpec((tm, d), lambda i: (i, 0)),
        scratch_shapes=[pltpu.VMEM((2, 2 * tm, d), F32), pltpu.SemaphoreType.DMA((2,))],
        compiler_params=_params(("arbitrary",)),
        name="combine",
    )(dest_tiles, dest_tiles, x1, gates, g_final, yb)


def _rope_tables(seq):
    half = DA_HEAD_DIM // 2
    inv_freq = ROPE_THETA ** (-(jnp.arange(half, dtype=F32) * 2.0 / DA_HEAD_DIM))
    ang = jnp.arange(seq, dtype=F32)[:, None] * inv_freq[None, :]
    cos, sin = jnp.cos(ang), jnp.sin(ang)
    cos_tab = jnp.tile(cos, (1, LANES // half))
    sin_tab = jnp.tile(jnp.concatenate([-sin, sin], axis=1), (1, LANES // DA_HEAD_DIM))
    return cos_tab, sin_tab


def _tile_indices(dest, tm):
    n = dest.shape[1]
    return dest.reshape(2, n // tm, tm).transpose(1, 0, 2).reshape(n // tm, 1, 2 * tm)


def _layer(x2d, batch, seq, layer, mix_g, w_in, lam_vecs, sub_g, w_bsb, w_bda, w_out, ffn_g,
           w_grp, w_exp, w_gate, w_up, w_down):
    n, d = x2d.shape
    sbw = SB_HEADS * SB_HEAD_DIM
    daw = DA_HEADS * DA_V_DIM
    qkv_cols = 3 * sbw + 3 * daw
    lam_init = 0.8 - 0.6 * math.exp(-0.3 * layer)

    w_in_bf16 = w_in.astype(BF16)
    cos_tab, sin_tab = _rope_tables(seq)
    rope_lo = 3 * sbw // TN_IN
    rope_hi = (3 * sbw + 2 * daw) // TN_IN
    proj = _in_proj(x2d, mix_g[None, :], w_in_bf16, qkv_cols, cos_tab, sin_tab, seq, rope_lo, rope_hi)
    proj3 = proj.reshape(batch, seq, qkv_cols)
    nsb = sbw // SB_HEAD_DIM
    nda = daw // DA_V_DIM
    n_exp, _, d_exp = w_gate.shape
    o_sb, o_da, wg_bf16, wu_bf16, wd_bf16 = _mixers(
        lam_vecs, proj3, sub_g[None, :], batch, seq, (0, nsb, 2 * nsb),
        (3 * nsb, 3 * nsb + nda, 3 * nsb + 2 * nda), lam_init,
        (w_gate.reshape(n_exp * d, d_exp), w_up.reshape(n_exp * d, d_exp), w_down.reshape(n_exp * d_exp, d)))

    wr_t = jnp.zeros((ROUTER_ROWS, d), F32)
    wr_t = wr_t.at[:N_EXPERTS].set(w_exp.T).at[N_EXPERTS:N_EXPERTS + N_GROUPS].set(w_grp.T)
    merged, = _gate_branch(x2d, mix_g[None, :], o_sb.reshape(n, sbw), o_da.reshape(n, daw), w_in_bf16, qkv_cols,
                           w_bsb.astype(BF16), w_bda.astype(BF16), ())
    x1, h2, logits_t = _out_proj(merged, x2d, w_out.astype(BF16), ffn_g[None, :], wr_t)

    ids, ranks, gates, counts = _route(logits_t)
    te = TM_EXPERT
    counts = counts[:, 0].astype(jnp.int32)
    padded = (counts + te - 1) // te * te
    pad_ends = jnp.cumsum(padded)
    pad_starts = pad_ends - padded
    expert_ids = jnp.arange(N_EXPERTS, dtype=jnp.int32)
    dest = ranks + jnp.sum(jnp.where(ids[..., None] == expert_ids, pad_starts, 0), axis=-1)
    n_tiles = (2 * n) // te + N_EXPERTS
    n_used = (pad_ends[-1] // te).astype(jnp.int32)[None]
    tile_row0 = jnp.arange(n_tiles, dtype=jnp.int32) * te
    tile_expert = jnp.minimum(jnp.sum((pad_ends[None, :] <= tile_row0[:, None]).astype(jnp.int32), axis=1),
                              N_EXPERTS - 1)
    zero_starts = ((pad_starts + counts) // SUBLANES * SUBLANES).astype(jnp.int32)
    zero_chunks = ((pad_ends - zero_starts) // SUBLANES).astype(jnp.int32)

    tmd = min(TM_DISPATCH, n)
    xb = _dispatch(zero_starts, zero_chunks, n_used, _tile_indices(dest, tmd), h2, n_tiles, tmd, te)
    tile_ids = jnp.arange(n_tiles, dtype=jnp.int32)
    new_run = jnp.logical_or(tile_ids == 0, tile_expert != jnp.roll(tile_expert, 1))
    run_slot = ((jnp.cumsum(new_run.astype(jnp.int32)) - 1) % 2).astype(jnp.int32)
    later = jnp.logical_and(tile_ids[None, :] < n_used, tile_expert[None, :] > tile_expert[:, None])
    next_expert = jnp.min(jnp.where(later, tile_expert[None, :], N_EXPERTS), axis=1)
    next_expert = jnp.where(next_expert == N_EXPERTS, -1, next_expert).astype(jnp.int32)
    yb = _experts(tile_expert, n_used, run_slot, next_expert, xb, wg_bf16.reshape(w_gate.shape),
                  wu_bf16.reshape(w_up.shape), wd_bf16.reshape(w_down.shape), n_tiles, te)
    return x1, gates, dest, yb


def kernel(x, mix_norm_g, w_in, lambda_q1, lambda_k1, lambda_q2, lambda_k2, diff_subnorm_g, w_branch_sb, w_branch_diff, w_out, ffn_norm_g, w_group_router, w_expert_router, w_gate, w_up, w_down, final_norm_g):
    batch, seq, d = x.shape
    depth = w_in.shape[0]
    n = batch * seq
    x2d = x.reshape(n, d)
    tmc = min(TM_COMBINE, n)
    for l in range(depth):
        lam_vecs = jnp.stack([lambda_q1[l], lambda_k1[l], lambda_q2[l], lambda_k2[l]]).astype(F32)
        x1, gates, dest, yb = _layer(
            x2d, batch, seq, l, mix_norm_g[l], w_in[l], lam_vecs, diff_subnorm_g[l], w_branch_sb[l],
            w_branch_diff[l], w_out[l], ffn_norm_g[l], w_group_router[l], w_expert_router[l],
            w_gate[l], w_up[l], w_down[l])
        x2d = _combine(_tile_indices(dest, tmc), x1, gates.T, final_norm_g[None, :], yb, tmc,
                       final_norm=(l == depth - 1))
    return x2d.reshape(batch, seq, d)
```

```python
import functools
import math

import jax
import jax.numpy as jnp
from jax import lax
from jax.experimental import pallas as pl
from jax.experimental.pallas import tpu as pltpu

F32 = jnp.float32
BF16 = jnp.bfloat16

CHUNK = 64
SB_HEADS = 8
SB_HEAD_DIM = 128
DA_HEADS = 8
DA_HEAD_DIM = 64
DA_V_DIM = 2 * DA_HEAD_DIM
N_GROUPS = 4
EXPERTS_PER_GROUP = 8
N_EXPERTS = N_GROUPS * EXPERTS_PER_GROUP
ROPE_THETA = 10000.0
EPS = 1e-6

LANES = 128
SUBLANES = 8
ROUTER_ROWS = 40
VMEM_LIMIT_BYTES = 56 * 1024 * 1024

SB_LOG_WEIGHT_FLOOR = -104.0
SB_MASKED_LOGIT = -1e30
LOG2_E = 1.4426950408889634

TM_IN = 1024
TN_IN = 1024
TQ_ATTN = 256
ATTN_TILES_PER_STEP = 4
TK_SB = 128
SB_NEAR_PAST = 256
TM_GATE = 1024
TN_GATE = 256
TM_OUT = 512
T_ROUTE = 512
TM_DISPATCH = 1024
TM_EXPERT = 256
XB_SLOTS = 3
TM_COMBINE = 512


def _params(sem, vmem=VMEM_LIMIT_BYTES):
    return pltpu.CompilerParams(dimension_semantics=sem, vmem_limit_bytes=vmem)


def _nt_dot(a, b, **kw):
    return lax.dot_general(a, b, (((1,), (1,)), ((), ())), preferred_element_type=F32, **kw)


def _in_proj_kernel(x_ref, g_ref, w_ref, cos_ref, sin_ref, o_ref, h_ref, *, rope_lo, rope_hi):
    j = pl.program_id(1)

    @pl.when(j == 0)
    def _():
        x = x_ref[...]
        ms = jnp.mean(x * x, axis=-1, keepdims=True)
        h = (x * lax.rsqrt(ms + EPS) * g_ref[...]).astype(BF16)
        h_ref[...] = h
        o_ref[...] = jnp.dot(h, w_ref[...], preferred_element_type=F32).astype(BF16)

    is_rope = jnp.logical_and(j >= rope_lo, j < rope_hi)

    @pl.when(is_rope)
    def _():
        y = jnp.dot(h_ref[...], w_ref[...], preferred_element_type=F32)
        cos = cos_ref[...]
        sin = sin_ref[...]
        lane = lax.broadcasted_iota(jnp.int32, cos.shape, 1)
        first_half = (lane % DA_HEAD_DIM) < (DA_HEAD_DIM // 2)
        for hh in range(y.shape[1] // LANES):
            yh = y[:, hh * LANES:(hh + 1) * LANES]
            partner = jnp.where(first_half,
                                pltpu.roll(yh, LANES - DA_HEAD_DIM // 2, 1),
                                pltpu.roll(yh, DA_HEAD_DIM // 2, 1))
            o_ref[:, hh * LANES:(hh + 1) * LANES] = (yh * cos + partner * sin).astype(BF16)

    @pl.when(jnp.logical_and(j > 0, jnp.logical_not(is_rope)))
    def _():
        o_ref[...] = jnp.dot(h_ref[...], w_ref[...], preferred_element_type=F32).astype(BF16)


def _in_proj(x2d, g, w_bf16, cols, cos_tab, sin_tab, seq, rope_lo, rope_hi):
    n, d = x2d.shape
    tm, tn = min(TM_IN, seq), TN_IN
    pos_tiles = seq // tm
    assert 0 < rope_lo <= rope_hi, "the first column tile carries the RMSNorm and must not be a rotary tile"
    return pl.pallas_call(
        functools.partial(_in_proj_kernel, rope_lo=rope_lo, rope_hi=rope_hi),
        out_shape=jax.ShapeDtypeStruct((n, cols), BF16),
        grid=(n // tm, cols // tn),
        in_specs=[
            pl.BlockSpec((tm, d), lambda i, j: (i, 0)),
            pl.BlockSpec((1, d), lambda i, j: (0, 0)),
            pl.BlockSpec((d, tn), lambda i, j: (0, j)),
            pl.BlockSpec((tm, LANES), lambda i, j: (i % pos_tiles, 0)),
            pl.BlockSpec((tm, LANES), lambda i, j: (i % pos_tiles, 0)),
        ],
        out_specs=pl.BlockSpec((tm, tn), lambda i, j: (i, j)),
        scratch_shapes=[pltpu.VMEM((tm, d), BF16)],
        compiler_params=_params(("parallel", "arbitrary")),
        name="in_proj",
    )(x2d, g, w_bf16, cos_tab, sin_tab)


def _sb_program(q_ref, k_ref, v_ref, o_ref, *, tq, tk, tiles_per_step, near_past, scale):
    row = lax.broadcasted_iota(jnp.int32, (2 * tk, 2 * tk), 0) % tk
    col = lax.broadcasted_iota(jnp.int32, (2 * tk, 2 * tk), 1)
    u2 = jnp.where(jnp.logical_or(col >= tk, row > col), 1.0, 0.0).astype(BF16)

    def span(q, start, width, diag_cols, running):
        k = k_ref[pl.ds(start, width), :]
        v = v_ref[pl.ds(start, width), :]
        z = _nt_dot(q, k) * scale
        if diag_cols is not None:
            r = lax.broadcasted_iota(jnp.int32, z.shape, 0)
            c = lax.broadcasted_iota(jnp.int32, z.shape, 1)
            z = jnp.where((c - diag_cols) < r, z, SB_MASKED_LOGIT)
        sp = jnp.maximum(z, 0.0) + jnp.log(1.0 + jnp.exp2(jnp.abs(z) * (-LOG2_E)))
        hi = sp.astype(BF16)
        lo = (sp - hi.astype(F32)).astype(BF16)
        nb = width // tk
        suffix = [None] * nb
        for blk in range(nb - 1, -1, -1):
            sl = slice(blk * tk, (blk + 1) * tk)
            cs = jnp.dot(jnp.concatenate([hi[:, sl], lo[:, sl]], axis=1), u2, preferred_element_type=F32)
            suffix[blk] = cs[:, :tk] + running
            running = running + cs[:, tk:]
        a = jnp.exp((z - sp) - jnp.concatenate(suffix, axis=1))
        return jnp.dot(a.astype(BF16), v, preferred_element_type=F32), running

    zero_run = jnp.zeros((tq, tk), F32)

    def far_keys(q, first, running, acc):
        def alive_after(running):
            return jnp.min(running) < -SB_LOG_WEIGHT_FLOOR

        def cond(c):
            j, alive, _, _ = c
            return jnp.logical_and(j >= 0, alive)

        def body(c):
            j, _, running, acc = c
            out, running = span(q, pl.multiple_of(j * tk, tk), tk, None, running)
            return j - 1, alive_after(running), running, acc + out

        return lax.while_loop(cond, body, (first, alive_after(running), running, acc))[3]

    def near(first_row, sub):
        q = q_ref[sub * tq:(sub + 1) * tq, :]
        t0 = first_row + sub * tq
        past = min(near_past, t0)
        acc, running = span(q, t0 - past, tq + past, past, zero_run)
        return acc, running, t0 - past

    def finish(near_out):
        for sub, (acc, running, start) in enumerate(near_out):
            if start > 0:
                acc = far_keys(q_ref[sub * tq:(sub + 1) * tq, :], start // tk - 1, running, acc)
            o_ref[sub * tq:(sub + 1) * tq, :] = acc.astype(o_ref.dtype)

    return near, finish


def _da_program(lam_ref, q_ref, k_ref, vext_ref, g_ref, o_ref, *, tq, tiles_per_step, lam_init):
    lv = lam_ref[...]
    lam = (jnp.exp(jnp.sum(lv[0:1] * lv[1:2], axis=-1, keepdims=True))
           - jnp.exp(jnp.sum(lv[2:3] * lv[3:4], axis=-1, keepdims=True)) + lam_init)
    lane = lax.broadcasted_iota(jnp.int32, (tq, q_ref.shape[1]), 1)
    r = lax.broadcasted_iota(jnp.int32, (tq, tq), 0)
    c = lax.broadcasted_iota(jnp.int32, (tq, tq), 1)
    diag_mask = (c // CHUNK) <= (r // CHUNK)

    def query_maps(rows):
        qs = q_ref[rows, :] * jnp.asarray(1.0 / math.sqrt(DA_HEAD_DIM), BF16)
        zero = jnp.zeros_like(qs)
        return jnp.where(lane < DA_HEAD_DIM, qs, zero), jnp.where(lane >= DA_HEAD_DIM, qs, zero)

    def softmax_av(qm, lo):
        s = _nt_dot(qm, k_ref[0:lo + tq, :])
        sd = jnp.where(diag_mask, s[:, lo:], -jnp.inf)
        s = sd if lo == 0 else jnp.concatenate([s[:, :lo], sd], axis=1)
        m = jnp.max(s, axis=-1, keepdims=True)
        acc = jnp.dot(jnp.exp(s - m).astype(BF16), vext_ref[0:lo + tq, :], preferred_element_type=F32)
        return acc[:, :DA_V_DIM] / acc[:, DA_V_DIM:]

    def run(first_row, sub):
        rows = slice(sub * tq, (sub + 1) * tq)
        q1, q2 = query_maps(rows)
        lo = first_row + sub * tq
        o = softmax_av(q1, lo) - lam * softmax_av(q2, lo)
        ms = jnp.mean(o * o, axis=-1, keepdims=True)
        o_ref[rows, :] = (o * lax.rsqrt(ms + EPS) * g_ref[...] * (1.0 - lam_init)).astype(o_ref.dtype)

    return run


def _mixers_kernel(lam_ref, qs_ref, ks_ref, vs_ref, qd_ref, kd_ref, vd_ref, g_ref, *rest, tq, tk, tiles_per_step,
                   n_steps, near_past, sb_scale, lam_init, n_cast):
    cast_in = rest[:n_cast]
    osb_ref, oda_ref = rest[n_cast], rest[n_cast + 1]
    cast_out = rest[n_cast + 2:2 * n_cast + 2]
    vext_ref = rest[-1]
    i = pl.program_id(2)

    @pl.when(i == 0)
    def _():
        vext_ref[:, :DA_V_DIM] = vd_ref[...]
        vext_ref[:, DA_V_DIM:] = jnp.ones((vext_ref.shape[0], DA_V_DIM), BF16)

    sb_near, sb_finish = _sb_program(qs_ref, ks_ref, vs_ref, osb_ref, tq=tq, tk=tk, tiles_per_step=tiles_per_step,
                                     near_past=near_past, scale=sb_scale)
    da_run = _da_program(lam_ref, qd_ref, kd_ref, vext_ref, g_ref, oda_ref, tq=tq, tiles_per_step=tiles_per_step,
                         lam_init=lam_init)
    rows = tq * tiles_per_step
    for step in range(n_steps):
        @pl.when(i == step)
        def _(step=step):
            near = []
            for sub in range(tiles_per_step):
                near.append(sb_near(step * rows, sub))
                da_run(step * rows, sub)
            for src, dst in zip(cast_in, cast_out):
                dst[...] = src[...].astype(dst.dtype)
            sb_finish(near)


def _mixers(lam_vecs, proj3, sub_g, batch, seq, sb_cols, da_cols, lam_init, to_cast):
    d = SB_HEAD_DIM
    assert DA_V_DIM == d and SB_HEADS == DA_HEADS
    tq, tk = min(TQ_ATTN, seq), min(TK_SB, seq)
    tps = min(ATTN_TILES_PER_STEP, seq // tq)
    rows = tq * tps
    n_i = seq // rows
    n_steps = batch * SB_HEADS * n_i
    cast_specs = []
    for a in to_cast:
        assert a.shape[0] % (n_steps * 16) == 0, "cast operand rows must split into whole bf16 tiles per step"
        cast_specs.append(pl.BlockSpec((a.shape[0] // n_steps, a.shape[1]),
                                       lambda b, h, i: ((b * SB_HEADS + h) * n_i + i, 0)))

    def q_spec(col):
        return pl.BlockSpec((None, rows, d), lambda b, h, i: (b, i, col + h))

    def kv_spec(col):
        return pl.BlockSpec((None, seq, d), lambda b, h, i: (b, 0, col + h))

    out_spec = pl.BlockSpec((None, rows, d), lambda b, h, i: (b, i, h))
    return pl.pallas_call(
        functools.partial(_mixers_kernel, tq=tq, tk=tk, tiles_per_step=tps, n_steps=n_i, near_past=SB_NEAR_PAST,
                          sb_scale=1.0 / math.sqrt(d), lam_init=lam_init, n_cast=len(to_cast)),
        out_shape=(jax.ShapeDtypeStruct((batch, seq, SB_HEADS * d), BF16),
                   jax.ShapeDtypeStruct((batch, seq, DA_HEADS * d), BF16),
                   *[jax.ShapeDtypeStruct(a.shape, BF16) for a in to_cast]),
        grid=(batch, SB_HEADS, n_i),
        in_specs=[
            pl.BlockSpec(lam_vecs.shape, lambda b, h, i: (0, 0)),
            q_spec(sb_cols[0]), kv_spec(sb_cols[1]), kv_spec(sb_cols[2]),
            q_spec(da_cols[0]), kv_spec(da_cols[1]), kv_spec(da_cols[2]),
            pl.BlockSpec((1, d), lambda b, h, i: (0, 0)),
            *cast_specs,
        ],
        out_specs=(out_spec, out_spec, *cast_specs),
        scratch_shapes=[pltpu.VMEM((seq, 2 * d), BF16)],
        compiler_params=_params(("parallel", "parallel", "arbitrary")),
        name="mixers",
    )(lam_vecs, proj3, proj3, proj3, proj3, proj3, proj3, sub_g, *to_cast)


def _gate_branch_kernel(x_ref, gmix_ref, osb_ref, oda_ref, wgs_ref, wgd_ref, wbs_ref, wbd_ref, *rest, n_cast):
    cast_in, m_ref, cast_out, h_scr = rest[:n_cast], rest[n_cast], rest[n_cast + 1:2 * n_cast + 1], rest[-1]

    def column_tile(h):
        gate_sb = jax.nn.sigmoid(jnp.dot(h, wgs_ref[...], preferred_element_type=F32))
        gate_da = jax.nn.sigmoid(jnp.dot(h, wgd_ref[...], preferred_element_type=F32))
        br_sb = jnp.dot(osb_ref[...], wbs_ref[...], preferred_element_type=F32)
        br_da = jnp.dot(oda_ref[...], wbd_ref[...], preferred_element_type=F32)
        m_ref[...] = (gate_sb * br_sb + gate_da * br_da).astype(m_ref.dtype)
        for src, dst in zip(cast_in, cast_out):
            dst[...] = src[...].astype(dst.dtype)

    @pl.when(pl.program_id(1) == 0)
    def _():
        x = x_ref[...]
        ms = jnp.mean(x * x, axis=-1, keepdims=True)
        h = (x * lax.rsqrt(ms + EPS) * gmix_ref[...]).astype(BF16)
        h_scr[...] = h
        column_tile(h)

    @pl.when(pl.program_id(1) > 0)
    def _():
        column_tile(h_scr[...])


def _gate_branch(x2d, gmix, o_sb, o_da, w_in_bf16, gate_col0, wbs, wbd, to_cast):
    n, d = x2d.shape
    tm, tn = min(TM_GATE, n), TN_GATE
    wsb, wda = o_sb.shape[1], o_da.shape[1]
    sb0, da0 = gate_col0 // tn, (gate_col0 + d) // tn
    nj = d // tn
    n_steps = (n // tm) * nj
    cast_specs = []
    for a in to_cast:
        assert a.shape[0] % (n_steps * 16) == 0, "cast operand rows must split into whole bf16 tiles per step"
        cast_specs.append(pl.BlockSpec((a.shape[0] // n_steps, a.shape[1]), lambda i, j: (i * nj + j, 0)))
    return pl.pallas_call(
        functools.partial(_gate_branch_kernel, n_cast=len(to_cast)),
        out_shape=(jax.ShapeDtypeStruct((n, d), BF16),
                   *[jax.ShapeDtypeStruct(a.shape, BF16) for a in to_cast]),
        grid=(n // tm, nj),
        in_specs=[
            pl.BlockSpec((tm, d), lambda i, j: (i, 0)),
            pl.BlockSpec((1, d), lambda i, j: (0, 0)),
            pl.BlockSpec((tm, wsb), lambda i, j: (i, 0)),
            pl.BlockSpec((tm, wda), lambda i, j: (i, 0)),
            pl.BlockSpec((d, tn), lambda i, j: (0, sb0 + j)),
            pl.BlockSpec((d, tn), lambda i, j: (0, da0 + j)),
            pl.BlockSpec((wsb, tn), lambda i, j: (0, j)),
            pl.BlockSpec((wda, tn), lambda i, j: (0, j)),
            *cast_specs,
        ],
        out_specs=(pl.BlockSpec((tm, tn), lambda i, j: (i, j)), *cast_specs),
        scratch_shapes=[pltpu.VMEM((tm, d), BF16)],
        compiler_params=_params(("parallel", "arbitrary")),
        name="gate_branch",
    )(x2d, gmix, o_sb, o_da, w_in_bf16, w_in_bf16, wbs, wbd, *to_cast)


def _out_proj_kernel(m_ref, x_ref, wo_ref, gffn_ref, wr_hi_ref, wr_lo_ref, x1_ref, h2_ref, lt_ref):
    x1 = x_ref[...] + jnp.dot(m_ref[...], wo_ref[...], preferred_element_type=F32)
    x1_ref[...] = x1
    ms = jnp.mean(x1 * x1, axis=-1, keepdims=True)
    h2 = x1 * lax.rsqrt(ms + EPS) * gffn_ref[...]
    h2_ref[...] = h2
    h_hi = h2.astype(BF16)
    h_lo = (h2 - h_hi.astype(F32)).astype(BF16)
    lt_ref[...] = (_nt_dot(wr_hi_ref[...], h_hi) + _nt_dot(wr_lo_ref[...], h_hi)
                   + _nt_dot(wr_hi_ref[...], h_lo))


def _out_proj(merged, x2d, wo, gffn, wr_t):
    n, d = x2d.shape
    tm = min(TM_OUT, n)
    once = pl.Buffered(1)
    wr_hi = wr_t.astype(BF16)
    wr_lo = (wr_t - wr_hi.astype(F32)).astype(BF16)
    return pl.pallas_call(
        _out_proj_kernel,
        out_shape=(jax.ShapeDtypeStruct((n, d), F32),
                   jax.ShapeDtypeStruct((n, d), F32),
                   jax.ShapeDtypeStruct((ROUTER_ROWS, n), F32)),
        grid=(n // tm,),
        in_specs=[
            pl.BlockSpec((tm, d), lambda i: (i, 0)),
            pl.BlockSpec((tm, d), lambda i: (i, 0)),
            pl.BlockSpec((d, d), lambda i: (0, 0), pipeline_mode=once),
            pl.BlockSpec((1, d), lambda i: (0, 0), pipeline_mode=once),
            pl.BlockSpec((ROUTER_ROWS, d), lambda i: (0, 0), pipeline_mode=once),
            pl.BlockSpec((ROUTER_ROWS, d), lambda i: (0, 0), pipeline_mode=once),
        ],
        out_specs=(pl.BlockSpec((tm, d), lambda i: (i, 0)),
                   pl.BlockSpec((tm, d), lambda i: (i, 0)),
                   pl.BlockSpec((ROUTER_ROWS, tm), lambda i: (0, i))),
        compiler_params=_params(("parallel",)),
        name="out_proj",
    )(merged, x2d, wo, gffn, wr_hi, wr_lo)


def _route_kernel(lt_ref, ids_ref, rank_ref, gate_ref, cnt_ref, carry_ref):
    step = pl.program_id(0)

    @pl.when(step == 0)
    def _():
        carry_ref[...] = jnp.zeros_like(carry_ref)

    t = lt_ref.shape[1]
    el = lt_ref[0:N_EXPERTS, :]
    gl = lt_ref[N_EXPERTS:N_EXPERTS + 8, :]
    grow = lax.broadcasted_iota(jnp.int32, gl.shape, 0)
    gl = jnp.where(grow < N_GROUPS, gl, -jnp.inf)
    gmax = jnp.max(gl, axis=0, keepdims=True)
    grp = jnp.min(jnp.where(gl == gmax, grow, N_GROUPS), axis=0, keepdims=True)
    p_grp = 1.0 / jnp.sum(jnp.exp(gl - gmax), axis=0, keepdims=True)

    eidx = lax.broadcasted_iota(jnp.int32, el.shape, 0)
    in_grp = (eidx // EXPERTS_PER_GROUP) == grp
    e1 = jnp.where(in_grp, el, -jnp.inf)
    top1 = jnp.max(e1, axis=0, keepdims=True)
    idx1 = jnp.min(jnp.where(e1 == top1, eidx, N_EXPERTS), axis=0, keepdims=True)
    e2 = jnp.where(eidx == idx1, -jnp.inf, e1)
    top2 = jnp.max(e2, axis=0, keepdims=True)
    idx2 = jnp.min(jnp.where(e2 == top2, eidx, N_EXPERTS), axis=0, keepdims=True)
    dlt = jnp.exp(top2 - top1)
    den = 1.0 + dlt
    gate_ref[0:1, :] = p_grp / den
    gate_ref[1:2, :] = p_grp * dlt / den
    ids_ref[0:1, :] = idx1
    ids_ref[1:2, :] = idx2

    oh1 = jnp.where(eidx == idx1, 1.0, 0.0)
    oh2 = jnp.where(eidx == idx2, 1.0, 0.0)
    rr = lax.broadcasted_iota(jnp.int32, (t, t), 0)
    cc = lax.broadcasted_iota(jnp.int32, (t, t), 1)
    before = jnp.where(rr < cc, 1.0, 0.0).astype(BF16)
    pre1 = jnp.dot(oh1.astype(BF16), before, preferred_element_type=F32)
    pre2 = jnp.dot(oh2.astype(BF16), before, preferred_element_type=F32)
    cnt1 = jnp.sum(oh1, axis=1, keepdims=True)
    cnt2 = jnp.sum(oh2, axis=1, keepdims=True)
    carry = carry_ref[:, 0:1]
    rank1 = jnp.sum(oh1 * (carry + pre1), axis=0, keepdims=True)
    rank2 = jnp.sum(oh2 * (carry + cnt1 + pre2), axis=0, keepdims=True)
    rank_ref[0:1, :] = rank1.astype(jnp.int32)
    rank_ref[1:2, :] = rank2.astype(jnp.int32)
    new_carry = carry_ref[...] + cnt1 + cnt2
    carry_ref[...] = new_carry
    cnt_ref[...] = new_carry


def _route(logits_t):
    n = logits_t.shape[1]
    t = min(T_ROUTE, n)
    return pl.pallas_call(
        _route_kernel,
        out_shape=(jax.ShapeDtypeStruct((2, n), jnp.int32),
                   jax.ShapeDtypeStruct((2, n), jnp.int32),
                   jax.ShapeDtypeStruct((2, n), F32),
                   jax.ShapeDtypeStruct((N_EXPERTS, LANES), F32)),
        grid=(n // t,),
        in_specs=[pl.BlockSpec((ROUTER_ROWS, t), lambda i: (0, i))],
        out_specs=(pl.BlockSpec((2, t), lambda i: (0, i)),
                   pl.BlockSpec((2, t), lambda i: (0, i)),
                   pl.BlockSpec((2, t), lambda i: (0, i)),
                   pl.BlockSpec((N_EXPERTS, LANES), lambda i: (0, 0))),
        scratch_shapes=[pltpu.VMEM((N_EXPERTS, LANES), F32)],
        compiler_params=_params(("arbitrary",)),
        name="route",
    )(logits_t)


def _dispatch_kernel(zs_ref, zc_ref, nu_ref, dest_ref, h2_ref, xb_hbm, zeros_ref, sem, tail_sem, *, tm, te,
                     n_tiles):
    step = pl.program_id(0)

    def tail_copy(t):
        return pltpu.make_async_copy(zeros_ref, xb_hbm.at[pl.ds(pl.multiple_of(t * te, te), te)], tail_sem)

    @pl.when(step == 0)
    def _():
        zeros_ref[...] = jnp.zeros_like(zeros_ref)

        def pad_copy(row):
            return pltpu.make_async_copy(zeros_ref.at[pl.ds(0, SUBLANES)],
                                         xb_hbm.at[pl.ds(pl.multiple_of(row, SUBLANES), SUBLANES)], sem)

        total = 0
        for e in range(N_EXPERTS):
            lax.fori_loop(0, zc_ref[e], lambda c, _, e=e: (pad_copy(zs_ref[e] + c * SUBLANES).start(), 0)[1], 0)
            total = total + zc_ref[e]
        lax.fori_loop(0, total, lambda c, _: (pad_copy(0).wait(), 0)[1], 0)
        lax.fori_loop(nu_ref[0], n_tiles, lambda t, c: (tail_copy(t).start(), c)[1], 0)

    def issue(blk, _):
        r0 = pl.multiple_of(blk * SUBLANES, SUBLANES)
        for u in range(SUBLANES):
            src = h2_ref.at[pl.ds(r0 + u, 1)]
            for k in range(2):
                pltpu.make_async_copy(src, xb_hbm.at[pl.ds(dest_ref[0, 0, k * tm + r0 + u], 1)],
                                      sem).start(priority=k)
        return 0

    lax.fori_loop(0, tm // SUBLANES, issue, 0)
    for _ in range(2):
        pltpu.make_async_copy(h2_ref, xb_hbm.at[pl.ds(0, tm)], sem).wait()

    @pl.when(step == pl.num_programs(0) - 1)
    def _():
        lax.fori_loop(nu_ref[0], n_tiles, lambda t, c: (tail_copy(t).wait(), c)[1], 0)


def _dispatch(zero_starts, zero_chunks, n_used, dest_tiles, h2, n_tiles, tm, te):
    n, d = h2.shape
    return pl.pallas_call(
        functools.partial(_dispatch_kernel, tm=tm, te=te, n_tiles=n_tiles),
        out_shape=jax.ShapeDtypeStruct((n_tiles * te, d), h2.dtype),
        grid_spec=pltpu.PrefetchScalarGridSpec(
            num_scalar_prefetch=3,
            grid=(n // tm,),
            in_specs=[
                pl.BlockSpec((1, 1, 2 * tm), lambda i, *_: (i, 0, 0), memory_space=pltpu.SMEM),
                pl.BlockSpec((tm, d), lambda i, *_: (i, 0)),
            ],
            out_specs=pl.BlockSpec(memory_space=pl.ANY),
            scratch_shapes=[pltpu.VMEM((te, d), h2.dtype), pltpu.SemaphoreType.DMA,
                            pltpu.SemaphoreType.DMA],
        ),
        compiler_params=_params(("arbitrary",)),
        name="dispatch",
    )(zero_starts, zero_chunks, n_used, dest_tiles, h2)


def _expert_kernel(te_ref, nu_ref, slot_ref, next_ref, xb_hbm, wg_hbm, wu_hbm, wd_hbm, yb_ref,
                   wg_s, wu_s, wd_s, sems, xb_buf, xb_sems, *, te):
    t = pl.program_id(0)
    used = t < nu_ref[0]
    new_run = jnp.logical_or(t == 0, te_ref[t] != te_ref[jnp.maximum(t - 1, 0)])

    def weight_copies(expert, slot):
        return [pltpu.make_async_copy(src.at[expert], dst.at[slot], sems.at[slot, i])
                for i, (src, dst) in enumerate(((wg_hbm, wg_s), (wu_hbm, wu_s), (wd_hbm, wd_s)))]

    @pl.when(t == 0)
    def _():
        for c in weight_copies(te_ref[0], 0):
            c.start()

    @pl.when(jnp.logical_and(used, new_run))
    def _():
        slot = slot_ref[t]
        for c in weight_copies(te_ref[t], slot):
            c.wait()

        @pl.when(next_ref[t] >= 0)
        def _():
            for c in weight_copies(next_ref[t], 1 - slot):
                c.start()

    def row_copy(tile):
        rows = pl.ds(pl.multiple_of(tile * te, te), te)
        slot = tile % XB_SLOTS
        return pltpu.make_async_copy(xb_hbm.at[rows], xb_buf.at[slot], xb_sems.at[slot])

    @pl.when(t == 0)
    def _():
        for ahead in range(XB_SLOTS - 1):
            @pl.when(ahead < nu_ref[0])
            def _(ahead=ahead):
                row_copy(ahead).start()

    @pl.when(used)
    def _():
        @pl.when(t + (XB_SLOTS - 1) < nu_ref[0])
        def _():
            row_copy(t + (XB_SLOTS - 1)).start()

        row_copy(t).wait()
        x = xb_buf[t % XB_SLOTS].astype(BF16)
        slot = slot_ref[t]
        g = jnp.dot(x, wg_s[slot], preferred_element_type=F32)
        u = jnp.dot(x, wu_s[slot], preferred_element_type=F32)
        hid = (g * jax.nn.sigmoid(g)) * u
        yb_ref[...] = jnp.dot(hid.astype(BF16), wd_s[slot], preferred_element_type=F32)

    @pl.when(jnp.logical_not(used))
    def _():
        yb_ref[...] = jnp.zeros_like(yb_ref)


def _experts(tile_expert, n_used, run_slot, next_expert, xb, wg, wu, wd, n_tiles, te):
    d = xb.shape[1]
    de = wg.shape[2]

    hbm = pl.BlockSpec(memory_space=pl.ANY)
    return pl.pallas_call(
        functools.partial(_expert_kernel, te=te),
        out_shape=jax.ShapeDtypeStruct((n_tiles * te, d), F32),
        grid_spec=pltpu.PrefetchScalarGridSpec(
            num_scalar_prefetch=4,
            grid=(n_tiles,),
            in_specs=[hbm, hbm, hbm, hbm],
            out_specs=pl.BlockSpec((te, d), lambda t, *_: (t, 0)),
            scratch_shapes=[pltpu.VMEM((2, d, de), BF16), pltpu.VMEM((2, d, de), BF16), pltpu.VMEM((2, de, d), BF16),
                            pltpu.SemaphoreType.DMA((2, 3)),
                            pltpu.VMEM((XB_SLOTS, te, d), xb.dtype), pltpu.SemaphoreType.DMA((XB_SLOTS,))],
        ),
        compiler_params=_params(("arbitrary",)),
        name="experts",
    )(tile_expert, n_used, run_slot, next_expert, xb, wg, wu, wd)


def _combine_kernel(dest_ref, next_dest_ref, x1_ref, gate_ref, g_ref, yb_hbm, o_ref, buf, sems, *, tm,
                    final_norm):
    i = pl.program_id(0)
    slot = i % 2

    def row_copy(idx_ref, slot, row, k):
        return pltpu.make_async_copy(yb_hbm.at[pl.ds(idx_ref[0, 0, row], 1)], buf.at[slot, pl.ds(row, 1)],
                                     sems.at[slot])

    def tile_wait(slot):
        pltpu.make_async_copy(yb_hbm.at[pl.ds(0, 2 * tm)], buf.at[slot], sems.at[slot]).wait()

    @pl.when(i == 0)
    def _():
        def issue(r, _):
            for k in range(2):
                row_copy(dest_ref, 0, k * tm + r, k).start(priority=k)
            return 0
        lax.fori_loop(0, tm, issue, 0, unroll=8)

    tile_wait(slot)
    for r in range(tm):
        for k in range(2):
            row_copy(next_dest_ref, 1 - slot, k * tm + r, k).start(priority=k)
    gates = gate_ref[...]
    g0, g1 = gates[:, 0:1], gates[:, 1:2]
    x = x1_ref[...] + (buf[slot, 0:tm, :] * g0 + buf[slot, tm:2 * tm, :] * g1)
    if final_norm:
        ms = jnp.mean(x * x, axis=-1, keepdims=True)
        x = x * lax.rsqrt(ms + EPS) * g_ref[...]
    o_ref[...] = x

    @pl.when(i == pl.num_programs(0) - 1)
    def _():
        tile_wait(1 - slot)


def _combine(dest_tiles, x1, gates, g_final, yb, tm, final_norm):
    n, d = x1.shape
    n_steps = n // tm
    return pl.pallas_call(
        functools.partial(_combine_kernel, tm=tm, final_norm=final_norm),
        out_shape=jax.ShapeDtypeStruct((n, d), F32),
        grid=(n_steps,),
        in_specs=[
            pl.BlockSpec((1, 1, 2 * tm), lambda i: (i, 0, 0), memory_space=pltpu.SMEM),
            pl.BlockSpec((1, 1, 2 * tm), lambda i: (jnp.minimum(i + 1, n_steps - 1), 0, 0),
                         memory_space=pltpu.SMEM),
            pl.BlockSpec((tm, d), lambda i: (i, 0)),
            pl.BlockSpec((tm, 2), lambda i: (i, 0)),
            pl.BlockSpec((1, d), lambda i: (0, 0)),
            pl.BlockSpec(memory_space=pl.ANY),
        ],
        out_specs=pl.BlockSpec((tm, d), lambda i: (i, 0)),
        scratch_shapes=[pltpu.VMEM((2, 2 * tm, d), F32), pltpu.SemaphoreType.DMA((2,))],
        compiler_params=_params(("arbitrary",)),
        name="combine",
    )(dest_tiles, dest_tiles, x1, gates, g_final, yb)


def _rope_tables(seq):
    half = DA_HEAD_DIM // 2
    inv_freq = ROPE_THETA ** (-(jnp.arange(half, dtype=F32) * 2.0 / DA_HEAD_DIM))
    ang = jnp.arange(seq, dtype=F32)[:, None] * inv_freq[None, :]
    cos, sin = jnp.cos(ang), jnp.sin(ang)
    cos_tab = jnp.tile(cos, (1, LANES // half))
    sin_tab = jnp.tile(jnp.concatenate([-sin, sin], axis=1), (1, LANES // DA_HEAD_DIM))
    return cos_tab, sin_tab


def _tile_indices(dest, tm):
    n = dest.shape[1]
    return dest.reshape(2, n // tm, tm).transpose(1, 0, 2).reshape(n // tm, 1, 2 * tm)


def _layer(x2d, batch, seq, layer, mix_g, w_in, lam_vecs, sub_g, w_bsb, w_bda, w_out, ffn_g,
           w_grp, w_exp, w_gate, w_up, w_down):
    n, d = x2d.shape
    sbw = SB_HEADS * SB_HEAD_DIM
    daw = DA_HEADS * DA_V_DIM
    qkv_cols = 3 * sbw + 3 * daw
    lam_init = 0.8 - 0.6 * math.exp(-0.3 * layer)

    w_in_bf16 = w_in.astype(BF16)
    cos_tab, sin_tab = _rope_tables(seq)
    rope_lo = 3 * sbw // TN_IN
    rope_hi = (3 * sbw + 2 * daw) // TN_IN
    proj = _in_proj(x2d, mix_g[None, :], w_in_bf16, qkv_cols, cos_tab, sin_tab, seq, rope_lo, rope_hi)
    proj3 = proj.reshape(batch, seq, qkv_cols)
    nsb = sbw // SB_HEAD_DIM
    nda = daw // DA_V_DIM
    n_exp, _, d_exp = w_gate.shape
    o_sb, o_da, wg_bf16, wu_bf16, wd_bf16, wo_bf16 = _mixers(
        lam_vecs, proj3, sub_g[None, :], batch, seq, (0, nsb, 2 * nsb),
        (3 * nsb, 3 * nsb + nda, 3 * nsb + 2 * nda), lam_init,
        (w_gate.reshape(n_exp * d, d_exp), w_up.reshape(n_exp * d, d_exp), w_down.reshape(n_exp * d_exp, d),
         w_out))

    wr_t = jnp.zeros((ROUTER_ROWS, d), F32)
    wr_t = wr_t.at[:N_EXPERTS].set(w_exp.T).at[N_EXPERTS:N_EXPERTS + N_GROUPS].set(w_grp.T)
    merged, = _gate_branch(x2d, mix_g[None, :], o_sb.reshape(n, sbw), o_da.reshape(n, daw), w_in_bf16, qkv_cols,
                           w_bsb.astype(BF16), w_bda.astype(BF16), ())
    x1, h2, logits_t = _out_proj(merged, x2d, wo_bf16, ffn_g[None, :], wr_t)

    ids, ranks, gates, counts = _route(logits_t)
    te = TM_EXPERT
    counts = counts[:, 0].astype(jnp.int32)
    padded = (counts + te - 1) // te * te
    pad_ends = jnp.cumsum(padded)
    pad_starts = pad_ends - padded
    expert_ids = jnp.arange(N_EXPERTS, dtype=jnp.int32)
    dest = ranks + jnp.sum(jnp.where(ids[..., None] == expert_ids, pad_starts, 0), axis=-1)
    n_tiles = (2 * n) // te + N_EXPERTS
    n_used = (pad_ends[-1] // te).astype(jnp.int32)[None]
    tile_row0 = jnp.arange(n_tiles, dtype=jnp.int32) * te
    tile_expert = jnp.minimum(jnp.sum((pad_ends[None, :] <= tile_row0[:, None]).astype(jnp.int32), axis=1),
                              N_EXPERTS - 1)
    zero_starts = ((pad_starts + counts) // SUBLANES * SUBLANES).astype(jnp.int32)
    zero_chunks = ((pad_ends - zero_starts) // SUBLANES).astype(jnp.int32)

    tmd = min(TM_DISPATCH, n)
    xb = _dispatch(zero_starts, zero_chunks, n_used, _tile_indices(dest, tmd), h2, n_tiles, tmd, te)
    tile_ids = jnp.arange(n_tiles, dtype=jnp.int32)
    new_run = jnp.logical_or(tile_ids == 0, tile_expert != jnp.roll(tile_expert, 1))
    run_slot = ((jnp.cumsum(new_run.astype(jnp.int32)) - 1) % 2).astype(jnp.int32)
    later = jnp.logical_and(tile_ids[None, :] < n_used, tile_expert[None, :] > tile_expert[:, None])
    next_expert = jnp.min(jnp.where(later, tile_expert[None, :], N_EXPERTS), axis=1)
    next_expert = jnp.where(next_expert == N_EXPERTS, -1, next_expert).astype(jnp.int32)
    yb = _experts(tile_expert, n_used, run_slot, next_expert, xb, wg_bf16.reshape(w_gate.shape),
                  wu_bf16.reshape(w_up.shape), wd_bf16.reshape(w_down.shape), n_tiles, te)
    return x1, gates, dest, yb


def kernel(x, mix_norm_g, w_in, lambda_q1, lambda_k1, lambda_q2, lambda_k2, diff_subnorm_g, w_branch_sb, w_branch_diff, w_out, ffn_norm_g, w_group_router, w_expert_router, w_gate, w_up, w_down, final_norm_g):
    batch, seq, d = x.shape
    depth = w_in.shape[0]
    n = batch * seq
    x2d = x.reshape(n, d)
    tmc = min(TM_COMBINE, n)
    for l in range(depth):
        lam_vecs = jnp.stack([lambda_q1[l], lambda_k1[l], lambda_q2[l], lambda_k2[l]]).astype(F32)
        x1, gates, dest, yb = _layer(
            x2d, batch, seq, l, mix_norm_g[l], w_in[l], lam_vecs, diff_subnorm_g[l], w_branch_sb[l],
            w_branch_diff[l], w_out[l], ffn_norm_g[l], w_group_router[l], w_expert_router[l],
            w_gate[l], w_up[l], w_down[l])
        x2d = _combine(_tile_indices(dest, tmc), x1, gates.T, final_norm_g[None, :], yb, tmc,
                       final_norm=(l == depth - 1))
    return x2d.reshape(batch, seq, d)
```

```python
import functools
import math

import jax
import jax.numpy as jnp
from jax import lax
from jax.experimental import pallas as pl
from jax.experimental.pallas import tpu as pltpu

F32 = jnp.float32
BF16 = jnp.bfloat16

CHUNK = 64
SB_HEADS = 8
SB_HEAD_DIM = 128
DA_HEADS = 8
DA_HEAD_DIM = 64
DA_V_DIM = 2 * DA_HEAD_DIM
N_GROUPS = 4
EXPERTS_PER_GROUP = 8
N_EXPERTS = N_GROUPS * EXPERTS_PER_GROUP
ROPE_THETA = 10000.0
EPS = 1e-6

LANES = 128
SUBLANES = 8
ROUTER_ROWS = 40
VMEM_LIMIT_BYTES = 56 * 1024 * 1024

SB_LOG_WEIGHT_FLOOR = -104.0
SB_MASKED_LOGIT = -1e30
LOG2_E = 1.4426950408889634

TM_IN = 1024
TN_IN = 1024
TQ_ATTN = 256
ATTN_TILES_PER_STEP = 4
TK_SB = 128
SB_NEAR_PAST = 256
TM_GATE = 1024
TN_GATE = 256
TM_OUT = 512
T_ROUTE = 512
TM_DISPATCH = 2048
TM_EXPERT = 256
XB_SLOTS = 3
TM_COMBINE = 512


def _params(sem, vmem=VMEM_LIMIT_BYTES):
    return pltpu.CompilerParams(dimension_semantics=sem, vmem_limit_bytes=vmem)


def _nt_dot(a, b, **kw):
    return lax.dot_general(a, b, (((1,), (1,)), ((), ())), preferred_element_type=F32, **kw)


def _in_proj_kernel(x_ref, g_ref, w_ref, cos_ref, sin_ref, o_ref, h_ref, *, rope_lo, rope_hi):
    j = pl.program_id(1)

    @pl.when(j == 0)
    def _():
        x = x_ref[...]
        ms = jnp.mean(x * x, axis=-1, keepdims=True)
        h = (x * lax.rsqrt(ms + EPS) * g_ref[...]).astype(BF16)
        h_ref[...] = h
        o_ref[...] = jnp.dot(h, w_ref[...], preferred_element_type=F32).astype(BF16)

    is_rope = jnp.logical_and(j >= rope_lo, j < rope_hi)

    @pl.when(is_rope)
    def _():
        y = jnp.dot(h_ref[...], w_ref[...], preferred_element_type=F32)
        cos = cos_ref[...]
        sin = sin_ref[...]
        lane = lax.broadcasted_iota(jnp.int32, cos.shape, 1)
        first_half = (lane % DA_HEAD_DIM) < (DA_HEAD_DIM // 2)
        for hh in range(y.shape[1] // LANES):
            yh = y[:, hh * LANES:(hh + 1) * LANES]
            partner = jnp.where(first_half,
                                pltpu.roll(yh, LANES - DA_HEAD_DIM // 2, 1),
                                pltpu.roll(yh, DA_HEAD_DIM // 2, 1))
            o_ref[:, hh * LANES:(hh + 1) * LANES] = (yh * cos + partner * sin).astype(BF16)

    @pl.when(jnp.logical_and(j > 0, jnp.logical_not(is_rope)))
    def _():
        o_ref[...] = jnp.dot(h_ref[...], w_ref[...], preferred_element_type=F32).astype(BF16)


def _in_proj(x2d, g, w_bf16, cols, cos_tab, sin_tab, seq, rope_lo, rope_hi):
    n, d = x2d.shape
    tm, tn = min(TM_IN, seq), TN_IN
    pos_tiles = seq // tm
    assert 0 < rope_lo <= rope_hi, "the first column tile carries the RMSNorm and must not be a rotary tile"
    return pl.pallas_call(
        functools.partial(_in_proj_kernel, rope_lo=rope_lo, rope_hi=rope_hi),
        out_shape=jax.ShapeDtypeStruct((n, cols), BF16),
        grid=(n // tm, cols // tn),
        in_specs=[
            pl.BlockSpec((tm, d), lambda i, j: (i, 0)),
            pl.BlockSpec((1, d), lambda i, j: (0, 0)),
            pl.BlockSpec((d, tn), lambda i, j: (0, j)),
            pl.BlockSpec((tm, LANES), lambda i, j: (i % pos_tiles, 0)),
            pl.BlockSpec((tm, LANES), lambda i, j: (i % pos_tiles, 0)),
        ],
        out_specs=pl.BlockSpec((tm, tn), lambda i, j: (i, j)),
        scratch_shapes=[pltpu.VMEM((tm, d), BF16)],
        compiler_params=_params(("parallel", "arbitrary")),
        name="in_proj",
    )(x2d, g, w_bf16, cos_tab, sin_tab)


def _sb_program(q_ref, k_ref, v_ref, o_ref, *, tq, tk, tiles_per_step, near_past, scale):
    row = lax.broadcasted_iota(jnp.int32, (2 * tk, 2 * tk), 0) % tk
    col = lax.broadcasted_iota(jnp.int32, (2 * tk, 2 * tk), 1)
    u2 = jnp.where(jnp.logical_or(col >= tk, row > col), 1.0, 0.0).astype(BF16)

    def span(q, start, width, diag_cols, running):
        k = k_ref[pl.ds(start, width), :]
        v = v_ref[pl.ds(start, width), :]
        z = _nt_dot(q, k) * scale
        if diag_cols is not None:
            r = lax.broadcasted_iota(jnp.int32, z.shape, 0)
            c = lax.broadcasted_iota(jnp.int32, z.shape, 1)
            z = jnp.where((c - diag_cols) < r, z, SB_MASKED_LOGIT)
        sp = jnp.maximum(z, 0.0) + jnp.log(1.0 + jnp.exp2(jnp.abs(z) * (-LOG2_E)))
        hi = sp.astype(BF16)
        lo = (sp - hi.astype(F32)).astype(BF16)
        nb = width // tk
        suffix = [None] * nb
        for blk in range(nb - 1, -1, -1):
            sl = slice(blk * tk, (blk + 1) * tk)
            cs = jnp.dot(jnp.concatenate([hi[:, sl], lo[:, sl]], axis=1), u2, preferred_element_type=F32)
            suffix[blk] = cs[:, :tk] + running
            running = running + cs[:, tk:]
        a = jnp.exp((z - sp) - jnp.concatenate(suffix, axis=1))
        return jnp.dot(a.astype(BF16), v, preferred_element_type=F32), running

    zero_run = jnp.zeros((tq, tk), F32)

    def far_keys(q, first, running, acc):
        def alive_after(running):
            return jnp.min(running) < -SB_LOG_WEIGHT_FLOOR

        def cond(c):
            j, alive, _, _ = c
            return jnp.logical_and(j >= 0, alive)

        def body(c):
            j, _, running, acc = c
            out, running = span(q, pl.multiple_of(j * tk, tk), tk, None, running)
            return j - 1, alive_after(running), running, acc + out

        return lax.while_loop(cond, body, (first, alive_after(running), running, acc))[3]

    def near(first_row, sub):
        q = q_ref[sub * tq:(sub + 1) * tq, :]
        t0 = first_row + sub * tq
        past = min(near_past, t0)
        acc, running = span(q, t0 - past, tq + past, past, zero_run)
        return acc, running, t0 - past

    def finish(near_out):
        for sub, (acc, running, start) in enumerate(near_out):
            if start > 0:
                acc = far_keys(q_ref[sub * tq:(sub + 1) * tq, :], start // tk - 1, running, acc)
            o_ref[sub * tq:(sub + 1) * tq, :] = acc.astype(o_ref.dtype)

    return near, finish


def _da_program(lam_ref, q_ref, k_ref, vext_ref, g_ref, o_ref, *, tq, tiles_per_step, lam_init):
    lv = lam_ref[...]
    lam = (jnp.exp(jnp.sum(lv[0:1] * lv[1:2], axis=-1, keepdims=True))
           - jnp.exp(jnp.sum(lv[2:3] * lv[3:4], axis=-1, keepdims=True)) + lam_init)
    lane = lax.broadcasted_iota(jnp.int32, (tq, q_ref.shape[1]), 1)
    r = lax.broadcasted_iota(jnp.int32, (tq, tq), 0)
    c = lax.broadcasted_iota(jnp.int32, (tq, tq), 1)
    diag_mask = (c // CHUNK) <= (r // CHUNK)

    def query_maps(rows):
        qs = q_ref[rows, :] * jnp.asarray(1.0 / math.sqrt(DA_HEAD_DIM), BF16)
        zero = jnp.zeros_like(qs)
        return jnp.where(lane < DA_HEAD_DIM, qs, zero), jnp.where(lane >= DA_HEAD_DIM, qs, zero)

    def softmax_av(qm, lo):
        s = _nt_dot(qm, k_ref[0:lo + tq, :])
        sd = jnp.where(diag_mask, s[:, lo:], -jnp.inf)
        s = sd if lo == 0 else jnp.concatenate([s[:, :lo], sd], axis=1)
        m = jnp.max(s, axis=-1, keepdims=True)
        acc = jnp.dot(jnp.exp(s - m).astype(BF16), vext_ref[0:lo + tq, :], preferred_element_type=F32)
        return acc[:, :DA_V_DIM] / acc[:, DA_V_DIM:]

    def run(first_row, sub):
        rows = slice(sub * tq, (sub + 1) * tq)
        q1, q2 = query_maps(rows)
        lo = first_row + sub * tq
        o = softmax_av(q1, lo) - lam * softmax_av(q2, lo)
        ms = jnp.mean(o * o, axis=-1, keepdims=True)
        o_ref[rows, :] = (o * lax.rsqrt(ms + EPS) * g_ref[...] * (1.0 - lam_init)).astype(o_ref.dtype)

    return run


def _mixers_kernel(lam_ref, qs_ref, ks_ref, vs_ref, qd_ref, kd_ref, vd_ref, g_ref, *rest, tq, tk, tiles_per_step,
                   n_steps, near_past, sb_scale, lam_init, n_cast):
    cast_in = rest[:n_cast]
    osb_ref, oda_ref = rest[n_cast], rest[n_cast + 1]
    cast_out = rest[n_cast + 2:2 * n_cast + 2]
    vext_ref = rest[-1]
    i = pl.program_id(2)

    @pl.when(i == 0)
    def _():
        vext_ref[:, :DA_V_DIM] = vd_ref[...]
        vext_ref[:, DA_V_DIM:] = jnp.ones((vext_ref.shape[0], DA_V_DIM), BF16)

    sb_near, sb_finish = _sb_program(qs_ref, ks_ref, vs_ref, osb_ref, tq=tq, tk=tk, tiles_per_step=tiles_per_step,
                                     near_past=near_past, scale=sb_scale)
    da_run = _da_program(lam_ref, qd_ref, kd_ref, vext_ref, g_ref, oda_ref, tq=tq, tiles_per_step=tiles_per_step,
                         lam_init=lam_init)
    rows = tq * tiles_per_step
    for step in range(n_steps):
        @pl.when(i == step)
        def _(step=step):
            near = []
            for sub in range(tiles_per_step):
                near.append(sb_near(step * rows, sub))
                da_run(step * rows, sub)
            for src, dst in zip(cast_in, cast_out):
                dst[...] = src[...].astype(dst.dtype)
            sb_finish(near)


def _mixers(lam_vecs, proj3, sub_g, batch, seq, sb_cols, da_cols, lam_init, to_cast):
    d = SB_HEAD_DIM
    assert DA_V_DIM == d and SB_HEADS == DA_HEADS
    tq, tk = min(TQ_ATTN, seq), min(TK_SB, seq)
    tps = min(ATTN_TILES_PER_STEP, seq // tq)
    rows = tq * tps
    n_i = seq // rows
    n_steps = batch * SB_HEADS * n_i
    cast_specs = []
    for a in to_cast:
        assert a.shape[0] % (n_steps * 16) == 0, "cast operand rows must split into whole bf16 tiles per step"
        cast_specs.append(pl.BlockSpec((a.shape[0] // n_steps, a.shape[1]),
                                       lambda b, h, i: ((b * SB_HEADS + h) * n_i + i, 0)))

    def q_spec(col):
        return pl.BlockSpec((None, rows, d), lambda b, h, i: (b, i, col + h))

    def kv_spec(col):
        return pl.BlockSpec((None, seq, d), lambda b, h, i: (b, 0, col + h))

    out_spec = pl.BlockSpec((None, rows, d), lambda b, h, i: (b, i, h))
    return pl.pallas_call(
        functools.partial(_mixers_kernel, tq=tq, tk=tk, tiles_per_step=tps, n_steps=n_i, near_past=SB_NEAR_PAST,
                          sb_scale=1.0 / math.sqrt(d), lam_init=lam_init, n_cast=len(to_cast)),
        out_shape=(jax.ShapeDtypeStruct((batch, seq, SB_HEADS * d), BF16),
                   jax.ShapeDtypeStruct((batch, seq, DA_HEADS * d), BF16),
                   *[jax.ShapeDtypeStruct(a.shape, BF16) for a in to_cast]),
        grid=(batch, SB_HEADS, n_i),
        in_specs=[
            pl.BlockSpec(lam_vecs.shape, lambda b, h, i: (0, 0)),
            q_spec(sb_cols[0]), kv_spec(sb_cols[1]), kv_spec(sb_cols[2]),
            q_spec(da_cols[0]), kv_spec(da_cols[1]), kv_spec(da_cols[2]),
            pl.BlockSpec((1, d), lambda b, h, i: (0, 0)),
            *cast_specs,
        ],
        out_specs=(out_spec, out_spec, *cast_specs),
        scratch_shapes=[pltpu.VMEM((seq, 2 * d), BF16)],
        compiler_params=_params(("parallel", "parallel", "arbitrary")),
        name="mixers",
    )(lam_vecs, proj3, proj3, proj3, proj3, proj3, proj3, sub_g, *to_cast)


def _gate_branch_kernel(x_ref, gmix_ref, osb_ref, oda_ref, wgs_ref, wgd_ref, wbs_ref, wbd_ref, m_ref, h_scr):
    def column_tile(h):
        gate_sb = jax.nn.sigmoid(jnp.dot(h, wgs_ref[...], preferred_element_type=F32))
        gate_da = jax.nn.sigmoid(jnp.dot(h, wgd_ref[...], preferred_element_type=F32))
        br_sb = jnp.dot(osb_ref[...], wbs_ref[...], preferred_element_type=F32)
        br_da = jnp.dot(oda_ref[...], wbd_ref[...], preferred_element_type=F32)
        m_ref[...] = (gate_sb * br_sb + gate_da * br_da).astype(m_ref.dtype)

    @pl.when(pl.program_id(1) == 0)
    def _():
        x = x_ref[...]
        ms = jnp.mean(x * x, axis=-1, keepdims=True)
        h = (x * lax.rsqrt(ms + EPS) * gmix_ref[...]).astype(BF16)
        h_scr[...] = h
        column_tile(h)

    @pl.when(pl.program_id(1) > 0)
    def _():
        column_tile(h_scr[...])


def _gate_branch(x2d, gmix, o_sb, o_da, w_in_bf16, gate_col0, wbs, wbd):
    n, d = x2d.shape
    tm, tn = min(TM_GATE, n), TN_GATE
    wsb, wda = o_sb.shape[1], o_da.shape[1]
    sb0, da0 = gate_col0 // tn, (gate_col0 + d) // tn
    return pl.pallas_call(
        _gate_branch_kernel,
        out_shape=jax.ShapeDtypeStruct((n, d), BF16),
        grid=(n // tm, d // tn),
        in_specs=[
            pl.BlockSpec((tm, d), lambda i, j: (i, 0)),
            pl.BlockSpec((1, d), lambda i, j: (0, 0)),
            pl.BlockSpec((tm, wsb), lambda i, j: (i, 0)),
            pl.BlockSpec((tm, wda), lambda i, j: (i, 0)),
            pl.BlockSpec((d, tn), lambda i, j: (0, sb0 + j)),
            pl.BlockSpec((d, tn), lambda i, j: (0, da0 + j)),
            pl.BlockSpec((wsb, tn), lambda i, j: (0, j)),
            pl.BlockSpec((wda, tn), lambda i, j: (0, j)),
        ],
        out_specs=pl.BlockSpec((tm, tn), lambda i, j: (i, j)),
        scratch_shapes=[pltpu.VMEM((tm, d), BF16)],
        compiler_params=_params(("parallel", "arbitrary")),
        name="gate_branch",
    )(x2d, gmix, o_sb, o_da, w_in_bf16, w_in_bf16, wbs, wbd)


def _out_proj_kernel(m_ref, x_ref, wo_ref, gffn_ref, wr_hi_ref, wr_lo_ref, x1_ref, h2_ref, lt_ref):
    x1 = x_ref[...] + jnp.dot(m_ref[...], wo_ref[...], preferred_element_type=F32)
    x1_ref[...] = x1
    ms = jnp.mean(x1 * x1, axis=-1, keepdims=True)
    h2 = x1 * lax.rsqrt(ms + EPS) * gffn_ref[...]
    h2_ref[...] = h2
    h_hi = h2.astype(BF16)
    h_lo = (h2 - h_hi.astype(F32)).astype(BF16)
    lt_ref[...] = (_nt_dot(wr_hi_ref[...], h_hi) + _nt_dot(wr_lo_ref[...], h_hi)
                   + _nt_dot(wr_hi_ref[...], h_lo))


def _out_proj(merged, x2d, wo, gffn, wr_t):
    n, d = x2d.shape
    tm = min(TM_OUT, n)
    once = pl.Buffered(1)
    wr_hi = wr_t.astype(BF16)
    wr_lo = (wr_t - wr_hi.astype(F32)).astype(BF16)
    return pl.pallas_call(
        _out_proj_kernel,
        out_shape=(jax.ShapeDtypeStruct((n, d), F32),
                   jax.ShapeDtypeStruct((n, d), F32),
                   jax.ShapeDtypeStruct((ROUTER_ROWS, n), F32)),
        grid=(n // tm,),
        in_specs=[
            pl.BlockSpec((tm, d), lambda i: (i, 0)),
            pl.BlockSpec((tm, d), lambda i: (i, 0)),
            pl.BlockSpec((d, d), lambda i: (0, 0), pipeline_mode=once),
            pl.BlockSpec((1, d), lambda i: (0, 0), pipeline_mode=once),
            pl.BlockSpec((ROUTER_ROWS, d), lambda i: (0, 0), pipeline_mode=once),
            pl.BlockSpec((ROUTER_ROWS, d), lambda i: (0, 0), pipeline_mode=once),
        ],
        out_specs=(pl.BlockSpec((tm, d), lambda i: (i, 0)),
                   pl.BlockSpec((tm, d), lambda i: (i, 0)),
                   pl.BlockSpec((ROUTER_ROWS, tm), lambda i: (0, i))),
        compiler_params=_params(("parallel",)),
        name="out_proj",
    )(merged, x2d, wo, gffn, wr_hi, wr_lo)


def _route_kernel(lt_ref, ids_ref, rank_ref, gate_ref, cnt_ref, carry_ref):
    step = pl.program_id(0)

    @pl.when(step == 0)
    def _():
        carry_ref[...] = jnp.zeros_like(carry_ref)

    t = lt_ref.shape[1]
    el = lt_ref[0:N_EXPERTS, :]
    gl = lt_ref[N_EXPERTS:N_EXPERTS + 8, :]
    grow = lax.broadcasted_iota(jnp.int32, gl.shape, 0)
    gl = jnp.where(grow < N_GROUPS, gl, -jnp.inf)
    gmax = jnp.max(gl, axis=0, keepdims=True)
    grp = jnp.min(jnp.where(gl == gmax, grow, N_GROUPS), axis=0, keepdims=True)
    p_grp = 1.0 / jnp.sum(jnp.exp(gl - gmax), axis=0, keepdims=True)

    eidx = lax.broadcasted_iota(jnp.int32, el.shape, 0)
    in_grp = (eidx // EXPERTS_PER_GROUP) == grp
    e1 = jnp.where(in_grp, el, -jnp.inf)
    top1 = jnp.max(e1, axis=0, keepdims=True)
    idx1 = jnp.min(jnp.where(e1 == top1, eidx, N_EXPERTS), axis=0, keepdims=True)
    e2 = jnp.where(eidx == idx1, -jnp.inf, e1)
    top2 = jnp.max(e2, axis=0, keepdims=True)
    idx2 = jnp.min(jnp.where(e2 == top2, eidx, N_EXPERTS), axis=0, keepdims=True)
    dlt = jnp.exp(top2 - top1)
    den = 1.0 + dlt
    gate_ref[0:1, :] = p_grp / den
    gate_ref[1:2, :] = p_grp * dlt / den
    ids_ref[0:1, :] = idx1
    ids_ref[1:2, :] = idx2

    oh1 = jnp.where(eidx == idx1, 1.0, 0.0)
    oh2 = jnp.where(eidx == idx2, 1.0, 0.0)
    rr = lax.broadcasted_iota(jnp.int32, (t, t), 0)
    cc = lax.broadcasted_iota(jnp.int32, (t, t), 1)
    before = jnp.where(rr < cc, 1.0, 0.0).astype(BF16)
    pre1 = jnp.dot(oh1.astype(BF16), before, preferred_element_type=F32)
    pre2 = jnp.dot(oh2.astype(BF16), before, preferred_element_type=F32)
    cnt1 = jnp.sum(oh1, axis=1, keepdims=True)
    cnt2 = jnp.sum(oh2, axis=1, keepdims=True)
    carry = carry_ref[:, 0:1]
    rank1 = jnp.sum(oh1 * (carry + pre1), axis=0, keepdims=True)
    rank2 = jnp.sum(oh2 * (carry + cnt1 + pre2), axis=0, keepdims=True)
    rank_ref[0:1, :] = rank1.astype(jnp.int32)
    rank_ref[1:2, :] = rank2.astype(jnp.int32)
    new_carry = carry_ref[...] + cnt1 + cnt2
    carry_ref[...] = new_carry
    cnt_ref[...] = new_carry


def _route(logits_t):
    n = logits_t.shape[1]
    t = min(T_ROUTE, n)
    return pl.pallas_call(
        _route_kernel,
        out_shape=(jax.ShapeDtypeStruct((2, n), jnp.int32),
                   jax.ShapeDtypeStruct((2, n), jnp.int32),
                   jax.ShapeDtypeStruct((2, n), F32),
                   jax.ShapeDtypeStruct((N_EXPERTS, LANES), F32)),
        grid=(n // t,),
        in_specs=[pl.BlockSpec((ROUTER_ROWS, t), lambda i: (0, i))],
        out_specs=(pl.BlockSpec((2, t), lambda i: (0, i)),
                   pl.BlockSpec((2, t), lambda i: (0, i)),
                   pl.BlockSpec((2, t), lambda i: (0, i)),
                   pl.BlockSpec((N_EXPERTS, LANES), lambda i: (0, 0))),
        scratch_shapes=[pltpu.VMEM((N_EXPERTS, LANES), F32)],
        compiler_params=_params(("arbitrary",)),
        name="route",
    )(logits_t)


def _dispatch_kernel(zs_ref, zc_ref, nu_ref, dest_ref, h2_ref, xb_hbm, zeros_ref, sem, tail_sem, *, tm, te,
                     n_tiles):
    step = pl.program_id(0)

    def tail_copy(t):
        return pltpu.make_async_copy(zeros_ref, xb_hbm.at[pl.ds(pl.multiple_of(t * te, te), te)], tail_sem)

    @pl.when(step == 0)
    def _():
        zeros_ref[...] = jnp.zeros_like(zeros_ref)

        def pad_copy(row):
            return pltpu.make_async_copy(zeros_ref.at[pl.ds(0, SUBLANES)],
                                         xb_hbm.at[pl.ds(pl.multiple_of(row, SUBLANES), SUBLANES)], sem)

        total = 0
        for e in range(N_EXPERTS):
            lax.fori_loop(0, zc_ref[e], lambda c, _, e=e: (pad_copy(zs_ref[e] + c * SUBLANES).start(), 0)[1], 0)
            total = total + zc_ref[e]
        lax.fori_loop(0, total, lambda c, _: (pad_copy(0).wait(), 0)[1], 0)
        lax.fori_loop(nu_ref[0], n_tiles, lambda t, c: (tail_copy(t).start(), c)[1], 0)

    def issue(blk, _):
        r0 = pl.multiple_of(blk * SUBLANES, SUBLANES)
        for u in range(SUBLANES):
            src = h2_ref.at[pl.ds(r0 + u, 1)]
            for k in range(2):
                pltpu.make_async_copy(src, xb_hbm.at[pl.ds(dest_ref[0, 0, k * tm + r0 + u], 1)],
                                      sem).start(priority=k)
        return 0

    lax.fori_loop(0, tm // SUBLANES, issue, 0)
    for _ in range(2):
        pltpu.make_async_copy(h2_ref, xb_hbm.at[pl.ds(0, tm)], sem).wait()

    @pl.when(step == pl.num_programs(0) - 1)
    def _():
        lax.fori_loop(nu_ref[0], n_tiles, lambda t, c: (tail_copy(t).wait(), c)[1], 0)


def _dispatch(zero_starts, zero_chunks, n_used, dest_tiles, h2, n_tiles, tm, te):
    n, d = h2.shape
    return pl.pallas_call(
        functools.partial(_dispatch_kernel, tm=tm, te=te, n_tiles=n_tiles),
        out_shape=jax.ShapeDtypeStruct((n_tiles * te, d), h2.dtype),
        grid_spec=pltpu.PrefetchScalarGridSpec(
            num_scalar_prefetch=3,
            grid=(n // tm,),
            in_specs=[
                pl.BlockSpec((1, 1, 2 * tm), lambda i, *_: (i, 0, 0), memory_space=pltpu.SMEM),
                pl.BlockSpec((tm, d), lambda i, *_: (i, 0)),
            ],
            out_specs=pl.BlockSpec(memory_space=pl.ANY),
            scratch_shapes=[pltpu.VMEM((te, d), h2.dtype), pltpu.SemaphoreType.DMA,
                            pltpu.SemaphoreType.DMA],
        ),
        compiler_params=_params(("arbitrary",)),
        name="dispatch",
    )(zero_starts, zero_chunks, n_used, dest_tiles, h2)


def _expert_kernel(te_ref, nu_ref, slot_ref, next_ref, xb_hbm, wg_hbm, wu_hbm, wd_hbm, yb_ref,
                   wg_s, wu_s, wd_s, sems, xb_buf, xb_sems, *, te):
    t = pl.program_id(0)
    used = t < nu_ref[0]
    new_run = jnp.logical_or(t == 0, te_ref[t] != te_ref[jnp.maximum(t - 1, 0)])

    def weight_copies(expert, slot):
        return [pltpu.make_async_copy(src.at[expert], dst.at[slot], sems.at[slot, i])
                for i, (src, dst) in enumerate(((wg_hbm, wg_s), (wu_hbm, wu_s), (wd_hbm, wd_s)))]

    @pl.when(t == 0)
    def _():
        for c in weight_copies(te_ref[0], 0):
            c.start()

    @pl.when(jnp.logical_and(used, new_run))
    def _():
        slot = slot_ref[t]
        for c in weight_copies(te_ref[t], slot):
            c.wait()

        @pl.when(next_ref[t] >= 0)
        def _():
            for c in weight_copies(next_ref[t], 1 - slot):
                c.start()

    def row_copy(tile):
        rows = pl.ds(pl.multiple_of(tile * te, te), te)
        slot = tile % XB_SLOTS
        return pltpu.make_async_copy(xb_hbm.at[rows], xb_buf.at[slot], xb_sems.at[slot])

    @pl.when(t == 0)
    def _():
        for ahead in range(XB_SLOTS - 1):
            @pl.when(ahead < nu_ref[0])
            def _(ahead=ahead):
                row_copy(ahead).start()

    @pl.when(used)
    def _():
        @pl.when(t + (XB_SLOTS - 1) < nu_ref[0])
        def _():
            row_copy(t + (XB_SLOTS - 1)).start()

        row_copy(t).wait()
        x = xb_buf[t % XB_SLOTS].astype(BF16)
        slot = slot_ref[t]
        g = jnp.dot(x, wg_s[slot], preferred_element_type=F32)
        u = jnp.dot(x, wu_s[slot], preferred_element_type=F32)
        hid = (g * jax.nn.sigmoid(g)) * u
        yb_ref[...] = jnp.dot(hid.astype(BF16), wd_s[slot], preferred_element_type=F32)

    @pl.when(jnp.logical_not(used))
    def _():
        yb_ref[...] = jnp.zeros_like(yb_ref)


def _experts(tile_expert, n_used, run_slot, next_expert, xb, wg, wu, wd, n_tiles, te):
    d = xb.shape[1]
    de = wg.shape[2]

    hbm = pl.BlockSpec(memory_space=pl.ANY)
    return pl.pallas_call(
        functools.partial(_expert_kernel, te=te),
        out_shape=jax.ShapeDtypeStruct((n_tiles * te, d), F32),
        grid_spec=pltpu.PrefetchScalarGridSpec(
            num_scalar_prefetch=4,
            grid=(n_tiles,),
            in_specs=[hbm, hbm, hbm, hbm],
            out_specs=pl.BlockSpec((te, d), lambda t, *_: (t, 0)),
            scratch_shapes=[pltpu.VMEM((2, d, de), BF16), pltpu.VMEM((2, d, de), BF16), pltpu.VMEM((2, de, d), BF16),
                            pltpu.SemaphoreType.DMA((2, 3)),
                            pltpu.VMEM((XB_SLOTS, te, d), xb.dtype), pltpu.SemaphoreType.DMA((XB_SLOTS,))],
        ),
        compiler_params=_params(("arbitrary",)),
        name="experts",
    )(tile_expert, n_used, run_slot, next_expert, xb, wg, wu, wd)


def _combine_kernel(dest_ref, next_dest_ref, x1_ref, gate_ref, g_ref, yb_hbm, o_ref, buf, sems, *, tm,
                    final_norm):
    i = pl.program_id(0)
    slot = i % 2

    def row_copy(idx_ref, slot, row, k):
        return pltpu.make_async_copy(yb_hbm.at[pl.ds(idx_ref[0, 0, row], 1)], buf.at[slot, pl.ds(row, 1)],
                                     sems.at[slot])

    def tile_wait(slot):
        pltpu.make_async_copy(yb_hbm.at[pl.ds(0, 2 * tm)], buf.at[slot], sems.at[slot]).wait()

    @pl.when(i == 0)
    def _():
        def issue(r, _):
            for k in range(2):
                row_copy(dest_ref, 0, k * tm + r, k).start(priority=k)
            return 0
        lax.fori_loop(0, tm, issue, 0, unroll=8)

    tile_wait(slot)
    for r in range(tm):
        for k in range(2):
            row_copy(next_dest_ref, 1 - slot, k * tm + r, k).start(priority=k)
    gates = gate_ref[...]
    g0, g1 = gates[:, 0:1], gates[:, 1:2]
    x = x1_ref[...] + (buf[slot, 0:tm, :] * g0 + buf[slot, tm:2 * tm, :] * g1)
    if final_norm:
        ms = jnp.mean(x * x, axis=-1, keepdims=True)
        x = x * lax.rsqrt(ms + EPS) * g_ref[...]
    o_ref[...] = x

    @pl.when(i == pl.num_programs(0) - 1)
    def _():
        tile_wait(1 - slot)


def _combine(dest_tiles, x1, gates, g_final, yb, tm, final_norm):
    n, d = x1.shape
    n_steps = n // tm
    return pl.pallas_call(
        functools.partial(_combine_kernel, tm=tm, final_norm=final_norm),
        out_shape=jax.ShapeDtypeStruct((n, d), F32),
        grid=(n_steps,),
        in_specs=[
            pl.BlockSpec((1, 1, 2 * tm), lambda i: (i, 0, 0), memory_space=pltpu.SMEM),
            pl.BlockSpec((1, 1, 2 * tm), lambda i: (jnp.minimum(i + 1, n_steps - 1), 0, 0),
                         memory_space=pltpu.SMEM),
            pl.BlockSpec((tm, d), lambda i: (i, 0)),
            pl.BlockSpec((tm, 2), lambda i: (i, 0)),
            pl.BlockSpec((1, d), lambda i: (0, 0)),
            pl.BlockSpec(memory_space=pl.ANY),
        ],
        out_specs=pl.BlockSpec((tm, d), lambda i: (i, 0)),
        scratch_shapes=[pltpu.VMEM((2, 2 * tm, d), F32), pltpu.SemaphoreType.DMA((2,))],
        compiler_params=_params(("arbitrary",)),
        name="combine",
    )(dest_tiles, dest_tiles, x1, gates, g_final, yb)


def _rope_tables(seq):
    half = DA_HEAD_DIM // 2
    inv_freq = ROPE_THETA ** (-(jnp.arange(half, dtype=F32) * 2.0 / DA_HEAD_DIM))
    ang = jnp.arange(seq, dtype=F32)[:, None] * inv_freq[None, :]
    cos, sin = jnp.cos(ang), jnp.sin(ang)
    cos_tab = jnp.tile(cos, (1, LANES // half))
    sin_tab = jnp.tile(jnp.concatenate([-sin, sin], axis=1), (1, LANES // DA_HEAD_DIM))
    return cos_tab, sin_tab


def _tile_indices(dest, tm):
    n = dest.shape[1]
    return dest.reshape(2, n // tm, tm).transpose(1, 0, 2).reshape(n // tm, 1, 2 * tm)


def _layer(x2d, batch, seq, layer, mix_g, w_in, lam_vecs, sub_g, w_bsb, w_bda, w_out, ffn_g,
           w_grp, w_exp, w_gate, w_up, w_down):
    n, d = x2d.shape
    sbw = SB_HEADS * SB_HEAD_DIM
    daw = DA_HEADS * DA_V_DIM
    qkv_cols = 3 * sbw + 3 * daw
    lam_init = 0.8 - 0.6 * math.exp(-0.3 * layer)

    w_in_bf16 = w_in.astype(BF16)
    cos_tab, sin_tab = _rope_tables(seq)
    rope_lo = 3 * sbw // TN_IN
    rope_hi = (3 * sbw + 2 * daw) // TN_IN
    proj = _in_proj(x2d, mix_g[None, :], w_in_bf16, qkv_cols, cos_tab, sin_tab, seq, rope_lo, rope_hi)
    proj3 = proj.reshape(batch, seq, qkv_cols)
    nsb = sbw // SB_HEAD_DIM
    nda = daw // DA_V_DIM
    n_exp, _, d_exp = w_gate.shape
    o_sb, o_da, wg_bf16, wu_bf16, wd_bf16, wo_bf16 = _mixers(
        lam_vecs, proj3, sub_g[None, :], batch, seq, (0, nsb, 2 * nsb),
        (3 * nsb, 3 * nsb + nda, 3 * nsb + 2 * nda), lam_init,
        (w_gate.reshape(n_exp * d, d_exp), w_up.reshape(n_exp * d, d_exp), w_down.reshape(n_exp * d_exp, d),
         w_out))

    wr_t = jnp.zeros((ROUTER_ROWS, d), F32)
    wr_t = wr_t.at[:N_EXPERTS].set(w_exp.T).at[N_EXPERTS:N_EXPERTS + N_GROUPS].set(w_grp.T)
    merged = _gate_branch(x2d, mix_g[None, :], o_sb.reshape(n, sbw), o_da.reshape(n, daw), w_in_bf16, qkv_cols,
                          w_bsb.astype(BF16), w_bda.astype(BF16))
    x1, h2, logits_t = _out_proj(merged, x2d, wo_bf16, ffn_g[None, :], wr_t)

    ids, ranks, gates, counts = _route(logits_t)
    te = TM_EXPERT
    counts = counts[:, 0].astype(jnp.int32)
    padded = (counts + te - 1) // te * te
    pad_ends = jnp.cumsum(padded)
    pad_starts = pad_ends - padded
    expert_ids = jnp.arange(N_EXPERTS, dtype=jnp.int32)
    dest = ranks + jnp.sum(jnp.where(ids[..., None] == expert_ids, pad_starts, 0), axis=-1)
    n_tiles = (2 * n) // te + N_EXPERTS
    n_used = (pad_ends[-1] // te).astype(jnp.int32)[None]
    tile_row0 = jnp.arange(n_tiles, dtype=jnp.int32) * te
    tile_expert = jnp.minimum(jnp.sum((pad_ends[None, :] <= tile_row0[:, None]).astype(jnp.int32), axis=1),
                              N_EXPERTS - 1)
    zero_starts = ((pad_starts + counts) // SUBLANES * SUBLANES).astype(jnp.int32)
    zero_chunks = ((pad_ends - zero_starts) // SUBLANES).astype(jnp.int32)

    tmd = min(TM_DISPATCH, n)
    xb = _dispatch(zero_starts, zero_chunks, n_used, _tile_indices(dest, tmd), h2, n_tiles, tmd, te)
    tile_ids = jnp.arange(n_tiles, dtype=jnp.int32)
    new_run = jnp.logical_or(tile_ids == 0, tile_expert != jnp.roll(tile_expert, 1))
    run_slot = ((jnp.cumsum(new_run.astype(jnp.int32)) - 1) % 2).astype(jnp.int32)
    later = jnp.logical_and(tile_ids[None, :] < n_used, tile_expert[None, :] > tile_expert[:, None])
    next_expert = jnp.min(jnp.where(later, tile_expert[None, :], N_EXPERTS), axis=1)
    next_expert = jnp.where(next_expert == N_EXPERTS, -1, next_expert).astype(jnp.int32)
    yb = _experts(tile_expert, n_used, run_slot, next_expert, xb, wg_bf16.reshape(w_gate.shape),
                  wu_bf16.reshape(w_up.shape), wd_bf16.reshape(w_down.shape), n_tiles, te)
    return x1, gates, dest, yb


def kernel(x, mix_norm_g, w_in, lambda_q1, lambda_k1, lambda_q2, lambda_k2, diff_subnorm_g, w_branch_sb, w_branch_diff, w_out, ffn_norm_g, w_group_router, w_expert_router, w_gate, w_up, w_down, final_norm_g):
    batch, seq, d = x.shape
    depth = w_in.shape[0]
    n = batch * seq
    x2d = x.reshape(n, d)
    tmc = min(TM_COMBINE, n)
    for l in range(depth):
        lam_vecs = jnp.stack([lambda_q1[l], lambda_k1[l], lambda_q2[l], lambda_k2[l]]).astype(F32)
        x1, gates, dest, yb = _layer(
            x2d, batch, seq, l, mix_norm_g[l], w_in[l], lam_vecs, diff_subnorm_g[l], w_branch_sb[l],
            w_branch_diff[l], w_out[l], ffn_norm_g[l], w_group_router[l], w_expert_router[l],
            w_gate[l], w_up[l], w_down[l])
        x2d = _combine(_tile_indices(dest, tmc), x1, gates.T, final_norm_g[None, :], yb, tmc,
                       final_norm=(l == depth - 1))
    return x2d.reshape(batch, seq, d)
```

```python
import functools
import math

import jax
import jax.numpy as jnp
from jax import lax
from jax.experimental import pallas as pl
from jax.experimental.pallas import tpu as pltpu

F32 = jnp.float32
BF16 = jnp.bfloat16

CHUNK = 64
SB_HEADS = 8
SB_HEAD_DIM = 128
DA_HEADS = 8
DA_HEAD_DIM = 64
DA_V_DIM = 2 * DA_HEAD_DIM
N_GROUPS = 4
EXPERTS_PER_GROUP = 8
N_EXPERTS = N_GROUPS * EXPERTS_PER_GROUP
ROPE_THETA = 10000.0
EPS = 1e-6

LANES = 128
SUBLANES = 8
ROUTER_ROWS = 40
VMEM_LIMIT_BYTES = 56 * 1024 * 1024

SB_LOG_WEIGHT_FLOOR = -104.0
SB_MASKED_LOGIT = -1e30
LOG2_E = 1.4426950408889634

TM_IN = 1024
TN_IN = 1024
TQ_ATTN = 256
ATTN_TILES_PER_STEP = 4
TK_SB = 128
SB_NEAR_PAST = 256
TM_GATE = 1024
TN_GATE = 512
TM_OUT = 512
T_ROUTE = 512
TM_DISPATCH = 2048
TM_EXPERT = 256
XB_SLOTS = 3
TM_COMBINE = 512


def _params(sem, vmem=VMEM_LIMIT_BYTES):
    return pltpu.CompilerParams(dimension_semantics=sem, vmem_limit_bytes=vmem)


def _nt_dot(a, b, **kw):
    return lax.dot_general(a, b, (((1,), (1,)), ((), ())), preferred_element_type=F32, **kw)


def _in_proj_kernel(x_ref, g_ref, w_ref, cos_ref, sin_ref, o_ref, h_ref, *, rope_lo, rope_hi):
    j = pl.program_id(1)

    @pl.when(j == 0)
    def _():
        x = x_ref[...]
        ms = jnp.mean(x * x, axis=-1, keepdims=True)
        h = (x * lax.rsqrt(ms + EPS) * g_ref[...]).astype(BF16)
        h_ref[...] = h
        o_ref[...] = jnp.dot(h, w_ref[...], preferred_element_type=F32).astype(BF16)

    is_rope = jnp.logical_and(j >= rope_lo, j < rope_hi)

    @pl.when(is_rope)
    def _():
        y = jnp.dot(h_ref[...], w_ref[...], preferred_element_type=F32)
        cos = cos_ref[...]
        sin = sin_ref[...]
        lane = lax.broadcasted_iota(jnp.int32, cos.shape, 1)
        first_half = (lane % DA_HEAD_DIM) < (DA_HEAD_DIM // 2)
        for hh in range(y.shape[1] // LANES):
            yh = y[:, hh * LANES:(hh + 1) * LANES]
            partner = jnp.where(first_half,
                                pltpu.roll(yh, LANES - DA_HEAD_DIM // 2, 1),
                                pltpu.roll(yh, DA_HEAD_DIM // 2, 1))
            o_ref[:, hh * LANES:(hh + 1) * LANES] = (yh * cos + partner * sin).astype(BF16)

    @pl.when(jnp.logical_and(j > 0, jnp.logical_not(is_rope)))
    def _():
        o_ref[...] = jnp.dot(h_ref[...], w_ref[...], preferred_element_type=F32).astype(BF16)


def _in_proj(x2d, g, w_bf16, cols, cos_tab, sin_tab, seq, rope_lo, rope_hi):
    n, d = x2d.shape
    tm, tn = min(TM_IN, seq), TN_IN
    pos_tiles = seq // tm
    assert 0 < rope_lo <= rope_hi, "the first column tile carries the RMSNorm and must not be a rotary tile"
    return pl.pallas_call(
        functools.partial(_in_proj_kernel, rope_lo=rope_lo, rope_hi=rope_hi),
        out_shape=jax.ShapeDtypeStruct((n, cols), BF16),
        grid=(n // tm, cols // tn),
        in_specs=[
            pl.BlockSpec((tm, d), lambda i, j: (i, 0)),
            pl.BlockSpec((1, d), lambda i, j: (0, 0)),
            pl.BlockSpec((d, tn), lambda i, j: (0, j)),
            pl.BlockSpec((tm, LANES), lambda i, j: (i % pos_tiles, 0)),
            pl.BlockSpec((tm, LANES), lambda i, j: (i % pos_tiles, 0)),
        ],
        out_specs=pl.BlockSpec((tm, tn), lambda i, j: (i, j)),
        scratch_shapes=[pltpu.VMEM((tm, d), BF16)],
        compiler_params=_params(("parallel", "arbitrary")),
        name="in_proj",
    )(x2d, g, w_bf16, cos_tab, sin_tab)


def _sb_program(q_ref, k_ref, v_ref, o_ref, *, tq, tk, tiles_per_step, near_past, scale):
    row = lax.broadcasted_iota(jnp.int32, (2 * tk, 2 * tk), 0) % tk
    col = lax.broadcasted_iota(jnp.int32, (2 * tk, 2 * tk), 1)
    u2 = jnp.where(jnp.logical_or(col >= tk, row > col), 1.0, 0.0).astype(BF16)

    def span(q, start, width, diag_cols, running):
        k = k_ref[pl.ds(start, width), :]
        v = v_ref[pl.ds(start, width), :]
        z = _nt_dot(q, k) * scale
        if diag_cols is not None:
            r = lax.broadcasted_iota(jnp.int32, z.shape, 0)
            c = lax.broadcasted_iota(jnp.int32, z.shape, 1)
            z = jnp.where((c - diag_cols) < r, z, SB_MASKED_LOGIT)
        sp = jnp.maximum(z, 0.0) + jnp.log(1.0 + jnp.exp2(jnp.abs(z) * (-LOG2_E)))
        hi = sp.astype(BF16)
        lo = (sp - hi.astype(F32)).astype(BF16)
        nb = width // tk
        suffix = [None] * nb
        for blk in range(nb - 1, -1, -1):
            sl = slice(blk * tk, (blk + 1) * tk)
            cs = jnp.dot(jnp.concatenate([hi[:, sl], lo[:, sl]], axis=1), u2, preferred_element_type=F32)
            suffix[blk] = cs[:, :tk] + running
            running = running + cs[:, tk:]
        a = jnp.exp((z - sp) - jnp.concatenate(suffix, axis=1))
        return jnp.dot(a.astype(BF16), v, preferred_element_type=F32), running

    zero_run = jnp.zeros((tq, tk), F32)

    def far_keys(q, first, running, acc):
        def alive_after(running):
            return jnp.min(running) < -SB_LOG_WEIGHT_FLOOR

        def cond(c):
            j, alive, _, _ = c
            return jnp.logical_and(j >= 0, alive)

        def body(c):
            j, _, running, acc = c
            out, running = span(q, pl.multiple_of(j * tk, tk), tk, None, running)
            return j - 1, alive_after(running), running, acc + out

        return lax.while_loop(cond, body, (first, alive_after(running), running, acc))[3]

    def near(first_row, sub):
        q = q_ref[sub * tq:(sub + 1) * tq, :]
        t0 = first_row + sub * tq
        past = min(near_past, t0)
        acc, running = span(q, t0 - past, tq + past, past, zero_run)
        return acc, running, t0 - past

    def finish(near_out):
        for sub, (acc, running, start) in enumerate(near_out):
            if start > 0:
                acc = far_keys(q_ref[sub * tq:(sub + 1) * tq, :], start // tk - 1, running, acc)
            o_ref[sub * tq:(sub + 1) * tq, :] = acc.astype(o_ref.dtype)

    return near, finish


def _da_program(lam_ref, q_ref, k_ref, vext_ref, g_ref, o_ref, *, tq, tiles_per_step, lam_init):
    lv = lam_ref[...]
    lam = (jnp.exp(jnp.sum(lv[0:1] * lv[1:2], axis=-1, keepdims=True))
           - jnp.exp(jnp.sum(lv[2:3] * lv[3:4], axis=-1, keepdims=True)) + lam_init)
    lane = lax.broadcasted_iota(jnp.int32, (tq, q_ref.shape[1]), 1)
    r = lax.broadcasted_iota(jnp.int32, (tq, tq), 0)
    c = lax.broadcasted_iota(jnp.int32, (tq, tq), 1)
    diag_mask = (c // CHUNK) <= (r // CHUNK)

    def query_maps(rows):
        qs = q_ref[rows, :] * jnp.asarray(1.0 / math.sqrt(DA_HEAD_DIM), BF16)
        zero = jnp.zeros_like(qs)
        return jnp.where(lane < DA_HEAD_DIM, qs, zero), jnp.where(lane >= DA_HEAD_DIM, qs, zero)

    def softmax_av(qm, lo):
        s = _nt_dot(qm, k_ref[0:lo + tq, :])
        sd = jnp.where(diag_mask, s[:, lo:], -jnp.inf)
        s = sd if lo == 0 else jnp.concatenate([s[:, :lo], sd], axis=1)
        m = jnp.max(s, axis=-1, keepdims=True)
        acc = jnp.dot(jnp.exp(s - m).astype(BF16), vext_ref[0:lo + tq, :], preferred_element_type=F32)
        return acc[:, :DA_V_DIM] / acc[:, DA_V_DIM:]

    def run(first_row, sub):
        rows = slice(sub * tq, (sub + 1) * tq)
        q1, q2 = query_maps(rows)
        lo = first_row + sub * tq
        o = softmax_av(q1, lo) - lam * softmax_av(q2, lo)
        ms = jnp.mean(o * o, axis=-1, keepdims=True)
        o_ref[rows, :] = (o * lax.rsqrt(ms + EPS) * g_ref[...] * (1.0 - lam_init)).astype(o_ref.dtype)

    return run


def _mixers_kernel(lam_ref, qs_ref, ks_ref, vs_ref, qd_ref, kd_ref, vd_ref, g_ref, *rest, tq, tk, tiles_per_step,
                   n_steps, near_past, sb_scale, lam_init, n_cast):
    cast_in = rest[:n_cast]
    osb_ref, oda_ref = rest[n_cast], rest[n_cast + 1]
    cast_out = rest[n_cast + 2:2 * n_cast + 2]
    vext_ref = rest[-1]
    i = pl.program_id(2)

    @pl.when(i == 0)
    def _():
        vext_ref[:, :DA_V_DIM] = vd_ref[...]
        vext_ref[:, DA_V_DIM:] = jnp.ones((vext_ref.shape[0], DA_V_DIM), BF16)

    sb_near, sb_finish = _sb_program(qs_ref, ks_ref, vs_ref, osb_ref, tq=tq, tk=tk, tiles_per_step=tiles_per_step,
                                     near_past=near_past, scale=sb_scale)
    da_run = _da_program(lam_ref, qd_ref, kd_ref, vext_ref, g_ref, oda_ref, tq=tq, tiles_per_step=tiles_per_step,
                         lam_init=lam_init)
    rows = tq * tiles_per_step
    for step in range(n_steps):
        @pl.when(i == step)
        def _(step=step):
            near = []
            for sub in range(tiles_per_step):
                near.append(sb_near(step * rows, sub))
                da_run(step * rows, sub)
            for src, dst in zip(cast_in, cast_out):
                dst[...] = src[...].astype(dst.dtype)
            sb_finish(near)


def _mixers(lam_vecs, proj3, sub_g, batch, seq, sb_cols, da_cols, lam_init, to_cast):
    d = SB_HEAD_DIM
    assert DA_V_DIM == d and SB_HEADS == DA_HEADS
    tq, tk = min(TQ_ATTN, seq), min(TK_SB, seq)
    tps = min(ATTN_TILES_PER_STEP, seq // tq)
    rows = tq * tps
    n_i = seq // rows
    n_steps = batch * SB_HEADS * n_i
    cast_specs = []
    for a in to_cast:
        assert a.shape[0] % (n_steps * 16) == 0, "cast operand rows must split into whole bf16 tiles per step"
        cast_specs.append(pl.BlockSpec((a.shape[0] // n_steps, a.shape[1]),
                                       lambda b, h, i: ((b * SB_HEADS + h) * n_i + i, 0)))

    def q_spec(col):
        return pl.BlockSpec((None, rows, d), lambda b, h, i: (b, i, col + h))

    def kv_spec(col):
        return pl.BlockSpec((None, seq, d), lambda b, h, i: (b, 0, col + h))

    out_spec = pl.BlockSpec((None, rows, d), lambda b, h, i: (b, i, h))
    return pl.pallas_call(
        functools.partial(_mixers_kernel, tq=tq, tk=tk, tiles_per_step=tps, n_steps=n_i, near_past=SB_NEAR_PAST,
                          sb_scale=1.0 / math.sqrt(d), lam_init=lam_init, n_cast=len(to_cast)),
        out_shape=(jax.ShapeDtypeStruct((batch, seq, SB_HEADS * d), BF16),
                   jax.ShapeDtypeStruct((batch, seq, DA_HEADS * d), BF16),
                   *[jax.ShapeDtypeStruct(a.shape, BF16) for a in to_cast]),
        grid=(batch, SB_HEADS, n_i),
        in_specs=[
            pl.BlockSpec(lam_vecs.shape, lambda b, h, i: (0, 0)),
            q_spec(sb_cols[0]), kv_spec(sb_cols[1]), kv_spec(sb_cols[2]),
            q_spec(da_cols[0]), kv_spec(da_cols[1]), kv_spec(da_cols[2]),
            pl.BlockSpec((1, d), lambda b, h, i: (0, 0)),
            *cast_specs,
        ],
        out_specs=(out_spec, out_spec, *cast_specs),
        scratch_shapes=[pltpu.VMEM((seq, 2 * d), BF16)],
        compiler_params=_params(("parallel", "parallel", "arbitrary")),
        name="mixers",
    )(lam_vecs, proj3, proj3, proj3, proj3, proj3, proj3, sub_g, *to_cast)


def _gate_branch_kernel(x_ref, gmix_ref, osb_ref, oda_ref, wgs_ref, wgd_ref, wbs_ref, wbd_ref, m_ref, h_scr):
    def column_tile(h):
        gate_sb = jax.nn.sigmoid(jnp.dot(h, wgs_ref[...], preferred_element_type=F32))
        gate_da = jax.nn.sigmoid(jnp.dot(h, wgd_ref[...], preferred_element_type=F32))
        br_sb = jnp.dot(osb_ref[...], wbs_ref[...], preferred_element_type=F32)
        br_da = jnp.dot(oda_ref[...], wbd_ref[...], preferred_element_type=F32)
        m_ref[...] = (gate_sb * br_sb + gate_da * br_da).astype(m_ref.dtype)

    @pl.when(pl.program_id(1) == 0)
    def _():
        x = x_ref[...]
        ms = jnp.mean(x * x, axis=-1, keepdims=True)
        h = (x * lax.rsqrt(ms + EPS) * gmix_ref[...]).astype(BF16)
        h_scr[...] = h
        column_tile(h)

    @pl.when(pl.program_id(1) > 0)
    def _():
        column_tile(h_scr[...])


def _gate_branch(x2d, gmix, o_sb, o_da, w_in_bf16, gate_col0, wbs, wbd):
    n, d = x2d.shape
    tm, tn = min(TM_GATE, n), TN_GATE
    wsb, wda = o_sb.shape[1], o_da.shape[1]
    sb0, da0 = gate_col0 // tn, (gate_col0 + d) // tn
    return pl.pallas_call(
        _gate_branch_kernel,
        out_shape=jax.ShapeDtypeStruct((n, d), BF16),
        grid=(n // tm, d // tn),
        in_specs=[
            pl.BlockSpec((tm, d), lambda i, j: (i, 0)),
            pl.BlockSpec((1, d), lambda i, j: (0, 0)),
            pl.BlockSpec((tm, wsb), lambda i, j: (i, 0)),
            pl.BlockSpec((tm, wda), lambda i, j: (i, 0)),
            pl.BlockSpec((d, tn), lambda i, j: (0, sb0 + j)),
            pl.BlockSpec((d, tn), lambda i, j: (0, da0 + j)),
            pl.BlockSpec((wsb, tn), lambda i, j: (0, j)),
            pl.BlockSpec((wda, tn), lambda i, j: (0, j)),
        ],
        out_specs=pl.BlockSpec((tm, tn), lambda i, j: (i, j)),
        scratch_shapes=[pltpu.VMEM((tm, d), BF16)],
        compiler_params=_params(("parallel", "arbitrary")),
        name="gate_branch",
    )(x2d, gmix, o_sb, o_da, w_in_bf16, w_in_bf16, wbs, wbd)


def _out_proj_kernel(m_ref, x_ref, wo_ref, gffn_ref, wr_hi_ref, wr_lo_ref, x1_ref, h2_ref, lt_ref):
    x1 = x_ref[...] + jnp.dot(m_ref[...], wo_ref[...], preferred_element_type=F32)
    x1_ref[...] = x1
    ms = jnp.mean(x1 * x1, axis=-1, keepdims=True)
    h2 = x1 * lax.rsqrt(ms + EPS) * gffn_ref[...]
    h2_ref[...] = h2
    h_hi = h2.astype(BF16)
    h_lo = (h2 - h_hi.astype(F32)).astype(BF16)
    lt_ref[...] = (_nt_dot(wr_hi_ref[...], h_hi) + _nt_dot(wr_lo_ref[...], h_hi)
                   + _nt_dot(wr_hi_ref[...], h_lo))


def _out_proj(merged, x2d, wo, gffn, wr_t):
    n, d = x2d.shape
    tm = min(TM_OUT, n)
    once = pl.Buffered(1)
    wr_hi = wr_t.astype(BF16)
    wr_lo = (wr_t - wr_hi.astype(F32)).astype(BF16)
    return pl.pallas_call(
        _out_proj_kernel,
        out_shape=(jax.ShapeDtypeStruct((n, d), F32),
                   jax.ShapeDtypeStruct((n, d), F32),
                   jax.ShapeDtypeStruct((ROUTER_ROWS, n), F32)),
        grid=(n // tm,),
        in_specs=[
            pl.BlockSpec((tm, d), lambda i: (i, 0)),
            pl.BlockSpec((tm, d), lambda i: (i, 0)),
            pl.BlockSpec((d, d), lambda i: (0, 0), pipeline_mode=once),
            pl.BlockSpec((1, d), lambda i: (0, 0), pipeline_mode=once),
            pl.BlockSpec((ROUTER_ROWS, d), lambda i: (0, 0), pipeline_mode=once),
            pl.BlockSpec((ROUTER_ROWS, d), lambda i: (0, 0), pipeline_mode=once),
        ],
        out_specs=(pl.BlockSpec((tm, d), lambda i: (i, 0)),
                   pl.BlockSpec((tm, d), lambda i: (i, 0)),
                   pl.BlockSpec((ROUTER_ROWS, tm), lambda i: (0, i))),
        compiler_params=_params(("parallel",)),
        name="out_proj",
    )(merged, x2d, wo, gffn, wr_hi, wr_lo)


def _route_kernel(lt_ref, ids_ref, rank_ref, gate_ref, cnt_ref, carry_ref):
    step = pl.program_id(0)

    @pl.when(step == 0)
    def _():
        carry_ref[...] = jnp.zeros_like(carry_ref)

    t = lt_ref.shape[1]
    el = lt_ref[0:N_EXPERTS, :]
    gl = lt_ref[N_EXPERTS:N_EXPERTS + 8, :]
    grow = lax.broadcasted_iota(jnp.int32, gl.shape, 0)
    gl = jnp.where(grow < N_GROUPS, gl, -jnp.inf)
    gmax = jnp.max(gl, axis=0, keepdims=True)
    grp = jnp.min(jnp.where(gl == gmax, grow, N_GROUPS), axis=0, keepdims=True)
    p_grp = 1.0 / jnp.sum(jnp.exp(gl - gmax), axis=0, keepdims=True)

    eidx = lax.broadcasted_iota(jnp.int32, el.shape, 0)
    in_grp = (eidx // EXPERTS_PER_GROUP) == grp
    e1 = jnp.where(in_grp, el, -jnp.inf)
    top1 = jnp.max(e1, axis=0, keepdims=True)
    idx1 = jnp.min(jnp.where(e1 == top1, eidx, N_EXPERTS), axis=0, keepdims=True)
    e2 = jnp.where(eidx == idx1, -jnp.inf, e1)
    top2 = jnp.max(e2, axis=0, keepdims=True)
    idx2 = jnp.min(jnp.where(e2 == top2, eidx, N_EXPERTS), axis=0, keepdims=True)
    dlt = jnp.exp(top2 - top1)
    den = 1.0 + dlt
    gate_ref[0:1, :] = p_grp / den
    gate_ref[1:2, :] = p_grp * dlt / den
    ids_ref[0:1, :] = idx1
    ids_ref[1:2, :] = idx2

    oh1 = jnp.where(eidx == idx1, 1.0, 0.0)
    oh2 = jnp.where(eidx == idx2, 1.0, 0.0)
    rr = lax.broadcasted_iota(jnp.int32, (t, t), 0)
    cc = lax.broadcasted_iota(jnp.int32, (t, t), 1)
    before = jnp.where(rr < cc, 1.0, 0.0).astype(BF16)
    pre1 = jnp.dot(oh1.astype(BF16), before, preferred_element_type=F32)
    pre2 = jnp.dot(oh2.astype(BF16), before, preferred_element_type=F32)
    cnt1 = jnp.sum(oh1, axis=1, keepdims=True)
    cnt2 = jnp.sum(oh2, axis=1, keepdims=True)
    carry = carry_ref[:, 0:1]
    rank1 = jnp.sum(oh1 * (carry + pre1), axis=0, keepdims=True)
    rank2 = jnp.sum(oh2 * (carry + cnt1 + pre2), axis=0, keepdims=True)
    rank_ref[0:1, :] = rank1.astype(jnp.int32)
    rank_ref[1:2, :] = rank2.astype(jnp.int32)
    new_carry = carry_ref[...] + cnt1 + cnt2
    carry_ref[...] = new_carry
    cnt_ref[...] = new_carry


def _route(logits_t):
    n = logits_t.shape[1]
    t = min(T_ROUTE, n)
    return pl.pallas_call(
        _route_kernel,
        out_shape=(jax.ShapeDtypeStruct((2, n), jnp.int32),
                   jax.ShapeDtypeStruct((2, n), jnp.int32),
                   jax.ShapeDtypeStruct((2, n), F32),
                   jax.ShapeDtypeStruct((N_EXPERTS, LANES), F32)),
        grid=(n // t,),
        in_specs=[pl.BlockSpec((ROUTER_ROWS, t), lambda i: (0, i))],
        out_specs=(pl.BlockSpec((2, t), lambda i: (0, i)),
                   pl.BlockSpec((2, t), lambda i: (0, i)),
                   pl.BlockSpec((2, t), lambda i: (0, i)),
                   pl.BlockSpec((N_EXPERTS, LANES), lambda i: (0, 0))),
        scratch_shapes=[pltpu.VMEM((N_EXPERTS, LANES), F32)],
        compiler_params=_params(("arbitrary",)),
        name="route",
    )(logits_t)


def _dispatch_kernel(zs_ref, zc_ref, nu_ref, dest_ref, h2_ref, xb_hbm, zeros_ref, sem, tail_sem, *, tm, te,
                     n_tiles):
    step = pl.program_id(0)

    def tail_copy(t):
        return pltpu.make_async_copy(zeros_ref, xb_hbm.at[pl.ds(pl.multiple_of(t * te, te), te)], tail_sem)

    @pl.when(step == 0)
    def _():
        zeros_ref[...] = jnp.zeros_like(zeros_ref)

        def pad_copy(row):
            return pltpu.make_async_copy(zeros_ref.at[pl.ds(0, SUBLANES)],
                                         xb_hbm.at[pl.ds(pl.multiple_of(row, SUBLANES), SUBLANES)], sem)

        total = 0
        for e in range(N_EXPERTS):
            lax.fori_loop(0, zc_ref[e], lambda c, _, e=e: (pad_copy(zs_ref[e] + c * SUBLANES).start(), 0)[1], 0)
            total = total + zc_ref[e]
        lax.fori_loop(0, total, lambda c, _: (pad_copy(0).wait(), 0)[1], 0)
        lax.fori_loop(nu_ref[0], n_tiles, lambda t, c: (tail_copy(t).start(), c)[1], 0)

    def issue(blk, _):
        r0 = pl.multiple_of(blk * SUBLANES, SUBLANES)
        for u in range(SUBLANES):
            src = h2_ref.at[pl.ds(r0 + u, 1)]
            for k in range(2):
                pltpu.make_async_copy(src, xb_hbm.at[pl.ds(dest_ref[0, 0, k * tm + r0 + u], 1)],
                                      sem).start(priority=k)
        return 0

    lax.fori_loop(0, tm // SUBLANES, issue, 0)
    for _ in range(2):
        pltpu.make_async_copy(h2_ref, xb_hbm.at[pl.ds(0, tm)], sem).wait()

    @pl.when(step == pl.num_programs(0) - 1)
    def _():
        lax.fori_loop(nu_ref[0], n_tiles, lambda t, c: (tail_copy(t).wait(), c)[1], 0)


def _dispatch(zero_starts, zero_chunks, n_used, dest_tiles, h2, n_tiles, tm, te):
    n, d = h2.shape
    return pl.pallas_call(
        functools.partial(_dispatch_kernel, tm=tm, te=te, n_tiles=n_tiles),
        out_shape=jax.ShapeDtypeStruct((n_tiles * te, d), h2.dtype),
        grid_spec=pltpu.PrefetchScalarGridSpec(
            num_scalar_prefetch=3,
            grid=(n // tm,),
            in_specs=[
                pl.BlockSpec((1, 1, 2 * tm), lambda i, *_: (i, 0, 0), memory_space=pltpu.SMEM),
                pl.BlockSpec((tm, d), lambda i, *_: (i, 0)),
            ],
            out_specs=pl.BlockSpec(memory_space=pl.ANY),
            scratch_shapes=[pltpu.VMEM((te, d), h2.dtype), pltpu.SemaphoreType.DMA,
                            pltpu.SemaphoreType.DMA],
        ),
        compiler_params=_params(("arbitrary",)),
        name="dispatch",
    )(zero_starts, zero_chunks, n_used, dest_tiles, h2)


def _expert_kernel(te_ref, nu_ref, slot_ref, next_ref, xb_hbm, wg_hbm, wu_hbm, wd_hbm, yb_ref,
                   wg_s, wu_s, wd_s, sems, xb_buf, xb_sems, *, te):
    t = pl.program_id(0)
    used = t < nu_ref[0]
    new_run = jnp.logical_or(t == 0, te_ref[t] != te_ref[jnp.maximum(t - 1, 0)])

    def weight_copies(expert, slot):
        return [pltpu.make_async_copy(src.at[expert], dst.at[slot], sems.at[slot, i])
                for i, (src, dst) in enumerate(((wg_hbm, wg_s), (wu_hbm, wu_s), (wd_hbm, wd_s)))]

    @pl.when(t == 0)
    def _():
        for c in weight_copies(te_ref[0], 0):
            c.start()

    @pl.when(jnp.logical_and(used, new_run))
    def _():
        slot = slot_ref[t]
        for c in weight_copies(te_ref[t], slot):
            c.wait()

        @pl.when(next_ref[t] >= 0)
        def _():
            for c in weight_copies(next_ref[t], 1 - slot):
                c.start()

    def row_copy(tile):
        rows = pl.ds(pl.multiple_of(tile * te, te), te)
        slot = tile % XB_SLOTS
        return pltpu.make_async_copy(xb_hbm.at[rows], xb_buf.at[slot], xb_sems.at[slot])

    @pl.when(t == 0)
    def _():
        for ahead in range(XB_SLOTS - 1):
            @pl.when(ahead < nu_ref[0])
            def _(ahead=ahead):
                row_copy(ahead).start()

    @pl.when(used)
    def _():
        @pl.when(t + (XB_SLOTS - 1) < nu_ref[0])
        def _():
            row_copy(t + (XB_SLOTS - 1)).start()

        row_copy(t).wait()
        x = xb_buf[t % XB_SLOTS].astype(BF16)
        slot = slot_ref[t]
        g = jnp.dot(x, wg_s[slot], preferred_element_type=F32)
        u = jnp.dot(x, wu_s[slot], preferred_element_type=F32)
        hid = (g * jax.nn.sigmoid(g)) * u
        yb_ref[...] = jnp.dot(hid.astype(BF16), wd_s[slot], preferred_element_type=F32)

    @pl.when(jnp.logical_not(used))
    def _():
        yb_ref[...] = jnp.zeros_like(yb_ref)


def _experts(tile_expert, n_used, run_slot, next_expert, xb, wg, wu, wd, n_tiles, te):
    d = xb.shape[1]
    de = wg.shape[2]

    hbm = pl.BlockSpec(memory_space=pl.ANY)
    return pl.pallas_call(
        functools.partial(_expert_kernel, te=te),
        out_shape=jax.ShapeDtypeStruct((n_tiles * te, d), F32),
        grid_spec=pltpu.PrefetchScalarGridSpec(
            num_scalar_prefetch=4,
            grid=(n_tiles,),
            in_specs=[hbm, hbm, hbm, hbm],
            out_specs=pl.BlockSpec((te, d), lambda t, *_: (t, 0)),
            scratch_shapes=[pltpu.VMEM((2, d, de), BF16), pltpu.VMEM((2, d, de), BF16), pltpu.VMEM((2, de, d), BF16),
                            pltpu.SemaphoreType.DMA((2, 3)),
                            pltpu.VMEM((XB_SLOTS, te, d), xb.dtype), pltpu.SemaphoreType.DMA((XB_SLOTS,))],
        ),
        compiler_params=_params(("arbitrary",)),
        name="experts",
    )(tile_expert, n_used, run_slot, next_expert, xb, wg, wu, wd)


def _combine_kernel(dest_ref, next_dest_ref, x1_ref, gate_ref, g_ref, yb_hbm, o_ref, buf, sems, *, tm,
                    final_norm):
    i = pl.program_id(0)
    slot = i % 2

    def row_copy(idx_ref, slot, row, k):
        return pltpu.make_async_copy(yb_hbm.at[pl.ds(idx_ref[0, 0, row], 1)], buf.at[slot, pl.ds(row, 1)],
                                     sems.at[slot])

    def tile_wait(slot):
        pltpu.make_async_copy(yb_hbm.at[pl.ds(0, 2 * tm)], buf.at[slot], sems.at[slot]).wait()

    @pl.when(i == 0)
    def _():
        def issue(r, _):
            for k in range(2):
                row_copy(dest_ref, 0, k * tm + r, k).start(priority=k)
            return 0
        lax.fori_loop(0, tm, issue, 0, unroll=8)

    tile_wait(slot)
    for r in range(tm):
        for k in range(2):
            row_copy(next_dest_ref, 1 - slot, k * tm + r, k).start(priority=k)
    gates = gate_ref[...]
    g0, g1 = gates[:, 0:1], gates[:, 1:2]
    x = x1_ref[...] + (buf[slot, 0:tm, :] * g0 + buf[slot, tm:2 * tm, :] * g1)
    if final_norm:
        ms = jnp.mean(x * x, axis=-1, keepdims=True)
        x = x * lax.rsqrt(ms + EPS) * g_ref[...]
    o_ref[...] = x

    @pl.when(i == pl.num_programs(0) - 1)
    def _():
        tile_wait(1 - slot)


def _combine(dest_tiles, x1, gates, g_final, yb, tm, final_norm):
    n, d = x1.shape
    n_steps = n // tm
    return pl.pallas_call(
        functools.partial(_combine_kernel, tm=tm, final_norm=final_norm),
        out_shape=jax.ShapeDtypeStruct((n, d), F32),
        grid=(n_steps,),
        in_specs=[
            pl.BlockSpec((1, 1, 2 * tm), lambda i: (i, 0, 0), memory_space=pltpu.SMEM),
            pl.BlockSpec((1, 1, 2 * tm), lambda i: (jnp.minimum(i + 1, n_steps - 1), 0, 0),
                         memory_space=pltpu.SMEM),
            pl.BlockSpec((tm, d), lambda i: (i, 0)),
            pl.BlockSpec((tm, 2), lambda i: (i, 0)),
            pl.BlockSpec((1, d), lambda i: (0, 0)),
            pl.BlockSpec(memory_space=pl.ANY),
        ],
        out_specs=pl.BlockSpec((tm, d), lambda i: (i, 0)),
        scratch_shapes=[pltpu.VMEM((2, 2 * tm, d), F32), pltpu.SemaphoreType.DMA((2,))],
        compiler_params=_params(("arbitrary",)),
        name="combine",
    )(dest_tiles, dest_tiles, x1, gates, g_final, yb)


def _rope_tables(seq):
    half = DA_HEAD_DIM // 2
    inv_freq = ROPE_THETA ** (-(jnp.arange(half, dtype=F32) * 2.0 / DA_HEAD_DIM))
    ang = jnp.arange(seq, dtype=F32)[:, None] * inv_freq[None, :]
    cos, sin = jnp.cos(ang), jnp.sin(ang)
    cos_tab = jnp.tile(cos, (1, LANES // half))
    sin_tab = jnp.tile(jnp.concatenate([-sin, sin], axis=1), (1, LANES // DA_HEAD_DIM))
    return cos_tab, sin_tab


def _tile_indices(dest, tm):
    n = dest.shape[1]
    return dest.reshape(2, n // tm, tm).transpose(1, 0, 2).reshape(n // tm, 1, 2 * tm)


def _layer(x2d, batch, seq, layer, mix_g, w_in, lam_vecs, sub_g, w_bsb, w_bda, w_out, ffn_g,
           w_grp, w_exp, w_gate, w_up, w_down):
    n, d = x2d.shape
    sbw = SB_HEADS * SB_HEAD_DIM
    daw = DA_HEADS * DA_V_DIM
    qkv_cols = 3 * sbw + 3 * daw
    lam_init = 0.8 - 0.6 * math.exp(-0.3 * layer)

    w_in_bf16 = w_in.astype(BF16)
    cos_tab, sin_tab = _rope_tables(seq)
    rope_lo = 3 * sbw // TN_IN
    rope_hi = (3 * sbw + 2 * daw) // TN_IN
    proj = _in_proj(x2d, mix_g[None, :], w_in_bf16, qkv_cols, cos_tab, sin_tab, seq, rope_lo, rope_hi)
    proj3 = proj.reshape(batch, seq, qkv_cols)
    nsb = sbw // SB_HEAD_DIM
    nda = daw // DA_V_DIM
    n_exp, _, d_exp = w_gate.shape
    o_sb, o_da, wg_bf16, wu_bf16, wd_bf16, wo_bf16 = _mixers(
        lam_vecs, proj3, sub_g[None, :], batch, seq, (0, nsb, 2 * nsb),
        (3 * nsb, 3 * nsb + nda, 3 * nsb + 2 * nda), lam_init,
        (w_gate.reshape(n_exp * d, d_exp), w_up.reshape(n_exp * d, d_exp), w_down.reshape(n_exp * d_exp, d),
         w_out))

    wr_t = jnp.zeros((ROUTER_ROWS, d), F32)
    wr_t = wr_t.at[:N_EXPERTS].set(w_exp.T).at[N_EXPERTS:N_EXPERTS + N_GROUPS].set(w_grp.T)
    merged = _gate_branch(x2d, mix_g[None, :], o_sb.reshape(n, sbw), o_da.reshape(n, daw), w_in_bf16, qkv_cols,
                          w_bsb.astype(BF16), w_bda.astype(BF16))
    x1, h2, logits_t = _out_proj(merged, x2d, wo_bf16, ffn_g[None, :], wr_t)

    ids, ranks, gates, counts = _route(logits_t)
    te = TM_EXPERT
    counts = counts[:, 0].astype(jnp.int32)
    padded = (counts + te - 1) // te * te
    pad_ends = jnp.cumsum(padded)
    pad_starts = pad_ends - padded
    expert_ids = jnp.arange(N_EXPERTS, dtype=jnp.int32)
    dest = ranks + jnp.sum(jnp.where(ids[..., None] == expert_ids, pad_starts, 0), axis=-1)
    n_tiles = (2 * n) // te + N_EXPERTS
    n_used = (pad_ends[-1] // te).astype(jnp.int32)[None]
    tile_row0 = jnp.arange(n_tiles, dtype=jnp.int32) * te
    tile_expert = jnp.minimum(jnp.sum((pad_ends[None, :] <= tile_row0[:, None]).astype(jnp.int32), axis=1),
                              N_EXPERTS - 1)
    zero_starts = ((pad_starts + counts) // SUBLANES * SUBLANES).astype(jnp.int32)
    zero_chunks = ((pad_ends - zero_starts) // SUBLANES).astype(jnp.int32)

    tmd = min(TM_DISPATCH, n)
    xb = _dispatch(zero_starts, zero_chunks, n_used, _tile_indices(dest, tmd), h2, n_tiles, tmd, te)
    tile_ids = jnp.arange(n_tiles, dtype=jnp.int32)
    new_run = jnp.logical_or(tile_ids == 0, tile_expert != jnp.roll(tile_expert, 1))
    run_slot = ((jnp.cumsum(new_run.astype(jnp.int32)) - 1) % 2).astype(jnp.int32)
    later = jnp.logical_and(tile_ids[None, :] < n_used, tile_expert[None, :] > tile_expert[:, None])
    next_expert = jnp.min(jnp.where(later, tile_expert[None, :], N_EXPERTS), axis=1)
    next_expert = jnp.where(next_expert == N_EXPERTS, -1, next_expert).astype(jnp.int32)
    yb = _experts(tile_expert, n_used, run_slot, next_expert, xb, wg_bf16.reshape(w_gate.shape),
                  wu_bf16.reshape(w_up.shape), wd_bf16.reshape(w_down.shape), n_tiles, te)
    return x1, gates, dest, yb


def kernel(x, mix_norm_g, w_in, lambda_q1, lambda_k1, lambda_q2, lambda_k2, diff_subnorm_g, w_branch_sb, w_branch_diff, w_out, ffn_norm_g, w_group_router, w_expert_router, w_gate, w_up, w_down, final_norm_g):
    batch, seq, d = x.shape
    depth = w_in.shape[0]
    n = batch * seq
    x2d = x.reshape(n, d)
    tmc = min(TM_COMBINE, n)
    for l in range(depth):
        lam_vecs = jnp.stack([lambda_q1[l], lambda_k1[l], lambda_q2[l], lambda_k2[l]]).astype(F32)
        x1, gates, dest, yb = _layer(
            x2d, batch, seq, l, mix_norm_g[l], w_in[l], lam_vecs, diff_subnorm_g[l], w_branch_sb[l],
            w_branch_diff[l], w_out[l], ffn_norm_g[l], w_group_router[l], w_expert_router[l],
            w_gate[l], w_up[l], w_down[l])
        x2d = _combine(_tile_indices(dest, tmc), x1, gates.T, final_norm_g[None, :], yb, tmc,
                       final_norm=(l == depth - 1))
    return x2d.reshape(batch, seq, d)
```
